```python
import math
import jax, jax.numpy as jnp
from jax import lax
import numpy as np

D_MODEL = 1024
BATCH = 8
SEQ = 4096
DEPTH = 2

GRID_W = 64
CTX_LEN = 256
EPS = 1e-6

A_W = D_MODEL // 4
A_GROUPS = 4
A_HORIZ = A_W // 2
G_HEADS = 4
G_DK = 128
G_DV = 128
QK_W = G_HEADS * G_DK
G_W = G_HEADS * G_DV
GDN_CHUNK = 64
C_GROUPS = 4
C_W = D_MODEL // 4
C_GD = C_W // C_GROUPS
C_CHUNK = 128
MIX_W = A_W + G_W + C_W
FFN_HIDDEN = ((8 * D_MODEL + 3 * 256 - 1) // (3 * 256)) * 256

OFF_A_B = 0
OFF_A_C = OFF_A_B + A_W
OFF_A_H = OFF_A_C + A_W
OFF_Q = OFF_A_H + A_W
OFF_K = OFF_Q + QK_W
OFF_V = OFF_K + QK_W
OFF_AB = OFF_V + G_W
OFF_Z = OFF_AB + 4 * G_HEADS
OFF_CU = OFF_Z + G_W
OFF_CV = OFF_CU + C_W
IN_COLS = OFF_CV + C_W

kernel_name = 'hybrid_conv_gdn_sgu_dit_block'


def _rmsnorm(a, g):
    a32 = a.astype(jnp.float32)
    y = a32 * lax.rsqrt(jnp.mean(a32 * a32, axis=-1, keepdims=True) + EPS) * g.astype(jnp.float32)
    return y.astype(a.dtype)


def _modulate(h, shift, scale):
    return h * (1 + scale) + shift


def _l2norm(a):
    a32 = a.astype(jnp.float32)
    return a32 * lax.rsqrt(jnp.sum(a32 * a32, axis=-1, keepdims=True) + EPS)


def _heads(a, dh):
    bn, t, _ = a.shape
    return a.reshape(bn, t, -1, dh).transpose(0, 2, 1, 3)


def _conv3_seq(a, w):
    ap = jnp.pad(a, ((0, 0), (1, 1), (0, 0)))
    return w[0] * ap[:, :-2] + w[1] * ap[:, 1:-1] + w[2] * ap[:, 2:]


def _conv3_grid(a, w):
    bn, t, ch = a.shape
    rows = t // GRID_W
    g = a.reshape(bn, rows, GRID_W, ch)
    wh, wv = w[:, :A_HORIZ], w[:, A_HORIZ:]
    gh = jnp.pad(g[..., :A_HORIZ], ((0, 0), (0, 0), (1, 1), (0, 0)))
    yh = wh[0] * gh[:, :, :-2] + wh[1] * gh[:, :, 1:-1] + wh[2] * gh[:, :, 2:]
    gv = jnp.pad(g[..., A_HORIZ:], ((0, 0), (1, 1), (0, 0), (0, 0)))
    yv = wv[0] * gv[:, :-2] + wv[1] * gv[:, 1:-1] + wv[2] * gv[:, 2:]
    return jnp.concatenate([yh, yv], axis=-1).reshape(bn, t, ch)


def _delta_update(s, w_c, u_c, kt_c, gl_c):
    v_new = u_c - jnp.einsum('bhck,bhkv->bhcv', w_c, s)
    s_new = s * jnp.exp(gl_c)[..., None, None] + jnp.einsum('bhck,bhcv->bhkv', kt_c, v_new)
    return v_new, s_new


def _gdn_scan(q, k, v, g, beta, s0):
    bn, h, t, _ = k.shape
    dv = v.shape[-1]
    n = t // GDN_CHUNK

    def chunks(a):
        a = a.astype(jnp.float32).reshape(bn, h, n, GDN_CHUNK, *a.shape[3:])
        return jnp.moveaxis(a, 2, 0)

    kc, vc, gc, bc = chunks(k), chunks(v), chunks(g), chunks(beta)
    gcum = jnp.cumsum(gc, axis=-1)
    idx = jnp.arange(GDN_CHUNK)
    causal = idx[:, None] >= idx[None, :]
    strict = idx[:, None] > idx[None, :]
    dmat = jnp.exp(jnp.where(causal, gcum[..., :, None] - gcum[..., None, :], -jnp.inf))
    kkt = jnp.einsum('nbhck,nbhdk->nbhcd', kc, kc)
    m = jnp.eye(GDN_CHUNK, dtype=jnp.float32) + jnp.where(strict, kkt * dmat * bc[..., None], 0.0)
    rhs = jnp.concatenate([vc * bc[..., None], kc * (bc * jnp.exp(gcum))[..., None]], axis=-1)
    sol = lax.linalg.triangular_solve(m, rhs, left_side=True, lower=True, unit_diagonal=True)
    u, w = sol[..., :dv], sol[..., dv:]
    g_last = gcum[..., -1]
    k_tail = kc * jnp.exp(g_last[..., None] - gcum)[..., None]
    s0 = s0.astype(jnp.float32)

    if q is None:
        def step_state(s, xs):
            w_c, u_c, kt_c, gl_c = xs
            _, s_new = _delta_update(s, w_c, u_c, kt_c, gl_c)
            return s_new, None
        s_fin, _ = lax.scan(step_state, s0, (w, u, k_tail, g_last))
        return None, s_fin

    qc = chunks(q)
    qk = jnp.einsum('nbhck,nbhdk->nbhcd', qc, kc) * dmat
    q_dec = qc * jnp.exp(gcum)[..., None]

    def step(s, xs):
        w_c, u_c, kt_c, gl_c, qd_c, qk_c = xs
        v_new, s_new = _delta_update(s, w_c, u_c, kt_c, gl_c)
        o = jnp.einsum('bhck,bhkv->bhcv', qd_c, s) + jnp.einsum('bhcd,bhdv->bhcv', qk_c, v_new)
        return s_new, o

    s_fin, o = lax.scan(step, s0, (w, u, k_tail, g_last, q_dec, qk))
    o = jnp.moveaxis(o, 0, 2).reshape(bn, h, t, dv)
    return o, s_fin


def _flip_t(a):
    return None if a is None else jnp.flip(a, axis=2)


def _gdn_bidir(q, k, v, g, beta, s0f, s0b):
    of, sf = _gdn_scan(q, k, v, g[0], beta[0], s0f)
    ob, sb = _gdn_scan(_flip_t(q), _flip_t(k), _flip_t(v), _flip_t(g[1]), _flip_t(beta[1]), s0b)
    o = None if q is None else of + _flip_t(ob)
    return o, sf, sb


def _gdn_kv(k_raw, v_raw, ab, conv_k, conv_v, a_log, dt_bias):
    bn, t, _ = k_raw.shape
    k = _l2norm(_heads(jax.nn.silu(_conv3_seq(k_raw, conv_k)), G_DK))
    v = _heads(jax.nn.silu(_conv3_seq(v_raw, conv_v)), G_DV).astype(jnp.float32)
    ab = ab.astype(jnp.float32).reshape(bn, t, 4, G_HEADS)
    beta = jax.nn.sigmoid(ab[:, :, :2])
    g = -jnp.exp(a_log.astype(jnp.float32)) * jax.nn.softplus(ab[:, :, 2:] + dt_bias.astype(jnp.float32))
    return k, v, jnp.transpose(g, (2, 0, 3, 1)), jnp.transpose(beta, (2, 0, 3, 1))


def _gated_out(o, z, gw):
    bn, t, _ = z.shape
    o = jnp.transpose(o, (0, 2, 1, 3))
    o = o * lax.rsqrt(jnp.mean(o * o, axis=-1, keepdims=True) + EPS) * gw.astype(jnp.float32)
    o = o * jax.nn.silu(z.astype(jnp.float32).reshape(bn, t, G_HEADS, G_DV))
    return o.reshape(bn, t, G_W).astype(z.dtype)


def _chunk_sgu(u, v, ln_g, ln_b, w_s, b_s):
    u = jax.nn.gelu(u, approximate=False)
    v32 = jax.nn.gelu(v, approximate=False).astype(jnp.float32)
    mu = jnp.mean(v32, axis=-1, keepdims=True)
    var = jnp.mean(jnp.square(v32 - mu), axis=-1, keepdims=True)
    v = ((v32 - mu) * lax.rsqrt(var + EPS) * ln_g.astype(jnp.float32) + ln_b.astype(jnp.float32)).astype(u.dtype)
    bn, t, _ = v.shape
    vc = v.reshape(bn, t // C_CHUNK, C_CHUNK, C_GROUPS, C_GD)
    mixed = jnp.einsum('gpq,bnqgc->bnpgc', w_s, vc) + b_s.T[None, None, :, :, None]
    return u * mixed.reshape(bn, t, C_W)


def _mixers(p, conv_fn, conv_a, conv_qkv, a_log, dt_bias, g_onorm, ln_g, ln_b, w_s, b_s, s0f, s0b):
    y_a = p[..., OFF_A_B:OFF_A_C] * conv_fn(p[..., OFF_A_C:OFF_A_H] * p[..., OFF_A_H:OFF_Q], conv_a)
    q = _l2norm(_heads(jax.nn.silu(_conv3_seq(p[..., OFF_Q:OFF_K], conv_qkv[:, :QK_W])), G_DK)) * (G_DK ** -0.5)
    k, v, g, beta = _gdn_kv(p[..., OFF_K:OFF_V], p[..., OFF_V:OFF_AB], p[..., OFF_AB:OFF_Z],
                            conv_qkv[:, QK_W:2 * QK_W], conv_qkv[:, 2 * QK_W:], a_log, dt_bias)
    o, sf, sb = _gdn_bidir(q, k, v, g, beta, s0f, s0b)
    y_b = _gated_out(o, p[..., OFF_Z:OFF_CU], g_onorm)
    y_c = _chunk_sgu(p[..., OFF_CU:OFF_CV], p[..., OFF_CV:IN_COLS], ln_g, ln_b, w_s, b_s)
    return jnp.concatenate([y_a, y_b, y_c], axis=-1), sf, sb


def _ffn_sublayer(s, shift, scale, gate, g_pre, g_post, w1, w2):
    h = _modulate(_rmsnorm(s, g_pre), shift, scale)
    gu = h @ w1
    y = (jax.nn.silu(gu[..., :FFN_HIDDEN]) * gu[..., FFN_HIDDEN:]) @ w2
    return s + gate * _rmsnorm(y, g_post)


def setup_inputs(seed: int = 0) -> dict:
    key = jax.random.key(seed)
    ks = jax.random.split(key, 24)
    L = DEPTH

    def nrm(k, shape, s):
        return jax.random.normal(k, shape, jnp.float32) * s

    dt = jnp.exp(jax.random.uniform(ks[14], (L, 2, G_HEADS), jnp.float32, math.log(1e-3), math.log(1e-1)))
    return {
        'x': nrm(ks[0], (BATCH, SEQ, D_MODEL), 1.0),
        'c': nrm(ks[1], (BATCH, D_MODEL), 1.0),
        'ctx': nrm(ks[2], (BATCH, CTX_LEN, D_MODEL), 1.0),
        'c_ctx': nrm(ks[3], (D_MODEL,), 1.0),
        'w_mod': nrm(ks[4], (L, D_MODEL, 6 * D_MODEL), 0.5 * D_MODEL ** -0.5),
        'b_mod': nrm(ks[5], (L, 6 * D_MODEL), 0.02),
        'g_pre_mix': 1.0 + nrm(ks[6], (L, D_MODEL), 0.02),
        'g_post_mix': 1.0 + nrm(ks[7], (L, D_MODEL), 0.02),
        'g_pre_ffn': 1.0 + nrm(ks[8], (L, D_MODEL), 0.02),
        'g_post_ffn': 1.0 + nrm(ks[9], (L, D_MODEL), 0.02),
        'w_in': nrm(ks[10], (L, D_MODEL, IN_COLS), D_MODEL ** -0.5),
        'conv_a': nrm(ks[11], (L, 3, A_W), 3 ** -0.5),
        'conv_qkv': nrm(ks[12], (L, 3, 2 * QK_W + G_W), 3 ** -0.5),
        'a_log': jnp.log(jax.random.uniform(ks[13], (L, 2, G_HEADS), jnp.float32, 1.0, 16.0)),
        'dt_bias': dt + jnp.log(-jnp.expm1(-dt)),
        'g_onorm': 1.0 + nrm(ks[15], (L, G_DV), 0.02),
        'ln_c_g': 1.0 + nrm(ks[16], (L, C_W), 0.02),
        'ln_c_b': nrm(ks[17], (L, C_W), 0.02),
        'w_s': nrm(ks[18], (L, C_GROUPS, C_CHUNK, C_CHUNK), C_CHUNK ** -0.5),
        'b_s': 1.0 + nrm(ks[19], (L, C_GROUPS, C_CHUNK), 0.1),
        'w_o': nrm(ks[20], (L, MIX_W, D_MODEL), MIX_W ** -0.5),
        'w_ffn_in': nrm(ks[21], (L, D_MODEL, 2 * FFN_HIDDEN), D_MODEL ** -0.5),
        'w_ffn_out': nrm(ks[22], (L, FFN_HIDDEN, D_MODEL), FFN_HIDDEN ** -0.5),
    }


def reference(x, c, ctx, c_ctx, w_mod, b_mod, g_pre_mix, g_post_mix, g_pre_ffn, g_post_ffn, w_in,
              conv_a, conv_qkv, a_log, dt_bias, g_onorm, ln_c_g, ln_c_b, w_s, b_s, w_o, w_ffn_in, w_ffn_out):
    bn = x.shape[0]
    s_zero = jnp.zeros((bn, G_HEADS, G_DK, G_DV), jnp.float32)
    for l in range(DEPTH):
        last = l == DEPTH - 1
        mx = jnp.split((jax.nn.silu(c) @ w_mod[l] + b_mod[l])[:, None, :], 6, axis=-1)
        mc = jnp.split(jax.nn.silu(c_ctx) @ w_mod[l] + b_mod[l], 6, axis=-1)

        hc = _modulate(_rmsnorm(ctx, g_pre_mix[l]), mc[0], mc[1])
        if last:
            pkv = hc @ w_in[l][:, OFF_K:OFF_Z]
            k_c, v_c, g_c, b_c = _gdn_kv(pkv[..., :QK_W], pkv[..., QK_W:QK_W + G_W], pkv[..., QK_W + G_W:],
                                         conv_qkv[l][:, QK_W:2 * QK_W], conv_qkv[l][:, 2 * QK_W:],
                                         a_log[l], dt_bias[l])
            _, sf, sb = _gdn_bidir(None, k_c, v_c, g_c, b_c, s_zero, s_zero)
        else:
            yc, sf, sb = _mixers(hc @ w_in[l], _conv3_seq, conv_a[l], conv_qkv[l], a_log[l], dt_bias[l],
                                 g_onorm[l], ln_c_g[l], ln_c_b[l], w_s[l], b_s[l], s_zero, s_zero)

        hx = _modulate(_rmsnorm(x, g_pre_mix[l]), mx[0], mx[1])
        yx, _, _ = _mixers(hx @ w_in[l], _conv3_grid, conv_a[l], conv_qkv[l], a_log[l], dt_bias[l],
                           g_onorm[l], ln_c_g[l], ln_c_b[l], w_s[l], b_s[l], sf, sb)
        x = x + mx[2] * _rmsnorm(yx @ w_o[l], g_post_mix[l])
        x = _ffn_sublayer(x, mx[3], mx[4], mx[5], g_pre_ffn[l], g_post_ffn[l], w_ffn_in[l], w_ffn_out[l])

        if not last:
            ctx = ctx + mc[2] * _rmsnorm(yc @ w_o[l], g_post_mix[l])
            ctx = _ffn_sublayer(ctx, mc[3], mc[4], mc[5], g_pre_ffn[l], g_post_ffn[l], w_ffn_in[l], w_ffn_out[l])
    return x
```

```python
import functools

import jax
import jax.numpy as jnp
from jax import lax
from jax.experimental import pallas as pl
from jax.experimental.pallas import tpu as pltpu

F32 = jnp.float32
BF16 = jnp.bfloat16
HIGHEST = lax.Precision.HIGHEST

EPS = 1e-6
GRID_W = 64
N_HEADS = 4
HEAD_W = 128
GDN_W = N_HEADS * HEAD_W
GDN_CHUNK = 64
A_W = 256
A_HORIZ = 128
C_W = 256
C_GROUPS = 4
C_GD = C_W // C_GROUPS
C_CHUNK = 128
LANES = 128
VMEM_LIMIT = 56 * 1024 * 1024


def _cparams(*sem):
    return pltpu.CompilerParams(dimension_semantics=sem, vmem_limit_bytes=VMEM_LIMIT)


def _bdot(a, b):
    return jnp.dot(a.astype(BF16), b.astype(BF16), preferred_element_type=F32)


def _bdot_nt(a, b):
    return lax.dot_general(a.astype(BF16), b.astype(BF16), (((1,), (1,)), ((), ())),
                           preferred_element_type=F32)


def _split(a):
    hi = a.astype(BF16)
    return hi, (a - hi.astype(F32)).astype(BF16)


def _dot3(a, b):
    ah, al = _split(a)
    bh, bl = _split(b)
    d = functools.partial(jnp.dot, preferred_element_type=F32)
    return d(ah, bh) + (d(ah, bl) + d(al, bh))


def _hdot(a, b):
    return jnp.dot(a, b, precision=HIGHEST, preferred_element_type=F32)


def _rms(x, g):
    return x * lax.rsqrt(jnp.mean(x * x, axis=-1, keepdims=True) + EPS) * g


def _silu(x):
    return x * jax.nn.sigmoid(x)


def _gelu(x):
    return 0.5 * x * (1.0 + lax.erf(x * 0.7071067811865476))


def _softplus(x):
    return jnp.maximum(x, 0.0) + jnp.log1p(jnp.exp(-jnp.abs(x)))


def _mod_kernel(c_ref, w_ref, b_ref, o_ref):
    o_ref[...] = _hdot(_silu(c_ref[...]), w_ref[...]) + b_ref[...]


def _modulation(cc, w_mod, b_mod):
    depth, d, six_d = w_mod.shape
    r = cc.shape[0]
    return pl.pallas_call(
        _mod_kernel,
        grid=(depth, six_d // d),
        in_specs=[pl.BlockSpec((r, d), lambda l, j: (0, 0)),
                  pl.BlockSpec((None, d, d), lambda l, j: (l, 0, j)),
                  pl.BlockSpec((None, 1, d), lambda l, j: (l, 0, j))],
        out_specs=pl.BlockSpec((None, r, d), lambda l, j: (l, 0, j)),
        out_shape=jax.ShapeDtypeStruct((depth, r, six_d), F32),
        compiler_params=_cparams("parallel", "parallel"),
        name="modulation",
    )(cc, w_mod, b_mod.reshape(depth, 1, six_d))


def _in_kernel(x_ref, mod_ref, g_ref, wq, wz, wa, wc, wab, oq, oz, oa, oc, oab):
    h = _rms(x_ref[...], g_ref[...]) * (1.0 + mod_ref[1:2, :]) + mod_ref[0:1, :]
    hb = h.astype(BF16)
    for w, o in ((wq, oq), (wz, oz), (wa, oa), (wc, oc)):
        o[...] = jnp.dot(hb, w[...], preferred_element_type=F32).astype(BF16)
    oab[...] = jnp.dot(hb, wab[...], preferred_element_type=F32)


def _row_tile(t):
    return min(t, 512)


def _in_proj(x, mod, g_pre, ws):
    b, t, d = x.shape
    tm = _row_tile(t)
    widths = [w.shape[1] for w in ws]
    const = lambda bi, j: (0, 0)
    tok = lambda bi, j: (bi, j, 0)
    return pl.pallas_call(
        _in_kernel,
        grid=(b, t // tm),
        in_specs=[pl.BlockSpec((None, tm, d), tok),
                  pl.BlockSpec((None, 8, d), lambda bi, j: (bi, 0, 0)),
                  pl.BlockSpec((1, d), const)]
                 + [pl.BlockSpec((d, n), const) for n in widths],
        out_specs=[pl.BlockSpec((None, tm, n), tok) for n in widths],
        out_shape=[jax.ShapeDtypeStruct((b, t, n), BF16) for n in widths[:-1]]
                  + [jax.ShapeDtypeStruct((b, t, widths[-1]), F32)],
        compiler_params=_cparams("parallel", "parallel"),
        name="in_proj",
    )(x, mod, g_pre, *ws)


def _sgu(pc, lng, lnb, ws, bsm):
    tt = pc.shape[0]
    u = _gelu(pc[:, :C_W])
    v = _gelu(pc[:, C_W:])
    mu = jnp.mean(v, axis=-1, keepdims=True)
    vc = v - mu
    var = jnp.mean(vc * vc, axis=-1, keepdims=True)
    vb = (vc * lax.rsqrt(var + EPS) * lng + lnb).astype(BF16)
    grp = lax.broadcasted_iota(jnp.int32, (C_CHUNK, C_W), 1) // C_GD
    outs = []
    for n in range(tt // C_CHUNK):
        r = jnp.dot(ws, vb[n * C_CHUNK:(n + 1) * C_CHUNK, :], preferred_element_type=F32)
        mixed = r[3 * C_CHUNK:]
        for g in (2, 1, 0):
            mixed = jnp.where(grp == g, r[g * C_CHUNK:(g + 1) * C_CHUNK], mixed)
        outs.append(mixed + bsm)
    return u * jnp.concatenate(outs, axis=0)


def _mix_grid_kernel(pa_ref, prev_ref, next_ref, pc_ref, cw_ref, lng_ref, lnb_ref, ws_ref, bsm_ref,
                     ya_ref, yc_ref, *, nj):
    j = pl.program_id(1)
    pa = pa_ref[...].astype(F32)
    tt = pa.shape[0]
    cw = cw_ref[...]
    gate = pa[:, :A_W]
    m = pa[:, A_W:2 * A_W] * pa[:, 2 * A_W:]
    col = lax.broadcasted_iota(jnp.int32, (tt, A_HORIZ), 0) % GRID_W
    mh = m[:, :A_HORIZ]
    left = jnp.where(col == 0, 0.0, pltpu.roll(mh, 1, 0))
    right = jnp.where(col == GRID_W - 1, 0.0, pltpu.roll(mh, tt - 1, 0))
    yh = cw[0:1, :A_HORIZ] * left + cw[1:2, :A_HORIZ] * mh + cw[2:3, :A_HORIZ] * right
    mv = m[:, A_HORIZ:]
    pv = prev_ref[...].astype(F32)
    nv = next_ref[...].astype(F32)
    has_prev = jnp.where(j > 0, 1.0, 0.0)
    has_next = jnp.where(j < nj - 1, 1.0, 0.0)
    pm = pv[:, A_W + A_HORIZ:2 * A_W] * pv[:, 2 * A_W + A_HORIZ:] * has_prev
    nm = nv[:, A_W + A_HORIZ:2 * A_W] * nv[:, 2 * A_W + A_HORIZ:] * has_next
    up = jnp.concatenate([pm, mv[:tt - GRID_W]], axis=0)
    down = jnp.concatenate([mv[GRID_W:], nm], axis=0)
    yv = cw[0:1, A_HORIZ:] * up + cw[1:2, A_HORIZ:] * mv + cw[2:3, A_HORIZ:] * down
    ya_ref[...] = (gate * jnp.concatenate([yh, yv], axis=-1)).astype(BF16)
    yc_ref[...] = _sgu(pc_ref[...].astype(F32), lng_ref[...], lnb_ref[...], ws_ref[...],
                       bsm_ref[...]).astype(BF16)


def _mix_seq_kernel(pa_ref, pc_ref, cw_ref, lng_ref, lnb_ref, ws_ref, bsm_ref, ya_ref, yc_ref):
    pa = pa_ref[...].astype(F32)
    tt = pa.shape[0]
    cw = cw_ref[...]
    m = pa[:, A_W:2 * A_W] * pa[:, 2 * A_W:]
    row = lax.broadcasted_iota(jnp.int32, (tt, A_W), 0)
    left = jnp.where(row == 0, 0.0, pltpu.roll(m, 1, 0))
    right = jnp.where(row == tt - 1, 0.0, pltpu.roll(m, tt - 1, 0))
    y = cw[0:1] * left + cw[1:2] * m + cw[2:3] * right
    ya_ref[...] = (pa[:, :A_W] * y).astype(BF16)
    yc_ref[...] = _sgu(pc_ref[...].astype(F32), lng_ref[...], lnb_ref[...], ws_ref[...],
                       bsm_ref[...]).astype(BF16)


def _mixers_ac(pa, pc, conv_a, ln_g, ln_b, ws, bsm, *, grid_conv):
    b, t, _ = pa.shape
    const = lambda bi, j: (0, 0)
    tok = lambda bi, j: (bi, j, 0)
    small = [pl.BlockSpec(conv_a.shape, const), pl.BlockSpec(ln_g.shape, const),
             pl.BlockSpec(ln_b.shape, const), pl.BlockSpec(ws.shape, const), pl.BlockSpec(bsm.shape, const)]
    outs = dict(out_shape=[jax.ShapeDtypeStruct((b, t, A_W), BF16), jax.ShapeDtypeStruct((b, t, C_W), BF16)],
                compiler_params=_cparams("parallel", "parallel"))
    if not grid_conv:
        return pl.pallas_call(
            _mix_seq_kernel, grid=(b, 1),
            in_specs=[pl.BlockSpec((None, t, 3 * A_W), tok), pl.BlockSpec((None, t, 2 * C_W), tok)] + small,
            out_specs=[pl.BlockSpec((None, t, A_W), tok), pl.BlockSpec((None, t, C_W), tok)],
            name="mixers_ac_seq", **outs,
        )(pa, pc, conv_a, ln_g, ln_b, ws, bsm)
    tt = _row_tile(t)
    nj = t // tt
    rows = tt // GRID_W
    last = t // GRID_W - 1
    return pl.pallas_call(
        functools.partial(_mix_grid_kernel, nj=nj), grid=(b, nj),
        in_specs=[pl.BlockSpec((None, tt, 3 * A_W), tok),
                  pl.BlockSpec((None, GRID_W, 3 * A_W), lambda bi, j: (bi, jnp.maximum(j * rows - 1, 0), 0)),
                  pl.BlockSpec((None, GRID_W, 3 * A_W), lambda bi, j: (bi, jnp.minimum((j + 1) * rows, last), 0)),
                  pl.BlockSpec((None, tt, 2 * C_W), tok)] + small,
        out_specs=[pl.BlockSpec((None, tt, A_W), tok), pl.BlockSpec((None, tt, C_W), tok)],
        name="mixers_ac_grid", **outs,
    )(pa, pa, pa, pc, conv_a, ln_g, ln_b, ws, bsm)


HALO = 16


def _unit_tri_inverse(ld, ii, jj):
    x = jnp.where(ii == jj, 1.0, 0.0) - jnp.where((ii >> 1) == (jj >> 1), ld, 0.0)
    for k in range(1, 6):
        off = jnp.where(((ii >> (k + 1)) == (jj >> (k + 1))) & ((ii >> k) != (jj >> k)), ld, 0.0)
        x = x - _dot3(x, _dot3(off, x))
    return x


def _gdn_prep_kernel(pq_ref, pqp_ref, pqn_ref, ab_ref, cw_ref, alog_t, dtb_t, alog_n, dtb_n,
                     l1_ref, l2_ref, u_ref, dec_ref, q_s, k_s, v_s, *, cpt, nt):
    j = pl.program_id(1)
    tt = cpt * GDN_CHUNK
    x = pq_ref[...].astype(F32)
    w3 = 3 * GDN_W
    has_prev = jnp.where(j > 0, 1.0, 0.0)
    has_next = jnp.where(j < nt - 1, 1.0, 0.0)
    prev_row = pqp_ref[HALO - 1:HALO, :].astype(F32) * has_prev
    next_row = pqn_ref[0:1, :].astype(F32) * has_next
    row = lax.broadcasted_iota(jnp.int32, (tt, w3), 0)
    xm1 = jnp.where(row == 0, prev_row, pltpu.roll(x, 1, 0))
    xp1 = jnp.where(row == tt - 1, next_row, pltpu.roll(x, tt - 1, 0))
    cw = cw_ref[...]
    s = _silu(cw[0:1] * xm1 + cw[1:2] * x + cw[2:3] * xp1)
    for h in range(N_HEADS):
        lo, hi = h * HEAD_W, (h + 1) * HEAD_W
        qh = s[:, lo:hi]
        q_s[:, lo:hi] = qh * (lax.rsqrt(jnp.sum(qh * qh, axis=-1, keepdims=True) + EPS) * HEAD_W ** -0.5)
        kh = s[:, GDN_W + lo:GDN_W + hi]
        k_s[:, lo:hi] = kh * lax.rsqrt(jnp.sum(kh * kh, axis=-1, keepdims=True) + EPS)
    v_s[...] = s[:, 2 * GDN_W:]

    cc = GDN_CHUNK
    ii = lax.broadcasted_iota(jnp.int32, (cc, cc), 0)
    jj = lax.broadcasted_iota(jnp.int32, (cc, cc), 1)
    ones_le = jnp.where(ii <= jj, 1.0, 0.0)
    ones_ge = jnp.where(ii >= jj, 1.0, 0.0)

    def chunk(c, carry):
        r0 = pl.multiple_of(c * cc, cc)
        kc = k_s[pl.ds(r0, cc), :]
        qc = q_s[pl.ds(r0, cc), :]
        vc = v_s[pl.ds(r0, cc), :]
        abn = ab_ref[pl.ds(r0, cc), :]
        g_n = -jnp.exp(alog_n[...]) * _softplus(abn + dtb_n[...])
        gam_n = (_hdot(ones_ge, g_n), _hdot(ones_le, g_n))
        abt = abn.T[0:16, :]
        beta = jax.nn.sigmoid(abt[0:8, :])
        g_t = -jnp.exp(alog_t[...]) * _softplus(abt[8:16, :] + dtb_t[...])
        gam_t = (_hdot(g_t, ones_le), _hdot(g_t, ones_ge))
        kkts, qks, kts = [], [], []
        for h in range(N_HEADS):
            kh = kc[:, h * HEAD_W:(h + 1) * HEAD_W]
            kkts.append(_bdot_nt(kh, kh))
            qks.append(_bdot_nt(qc[:, h * HEAD_W:(h + 1) * HEAD_W], kh))
            kts.append(kh.T)
        for d in range(2):
            incl = (ii >= jj) if d == 0 else (ii <= jj)
            strict = (ii > jj) if d == 0 else (ii < jj)
            last = cc - 1 if d == 0 else 0
            us, decs, tops, bots = [], [], [], []
            for h in range(N_HEADS):
                r = 4 * d + h
                kh = kc[:, h * HEAD_W:(h + 1) * HEAD_W]
                qh = qc[:, h * HEAD_W:(h + 1) * HEAD_W]
                vh = vc[:, h * HEAD_W:(h + 1) * HEAD_W]
                grow = gam_t[d][r:r + 1, :]
                brow = beta[r:r + 1, :]
                gcol = jnp.broadcast_to(gam_n[d][:, 8 + r:9 + r], (cc, HEAD_W))
                dm = jnp.exp(jnp.where(incl, gcol[:, :cc] - grow, -1e30))
                ld = jnp.where(strict, kkts[h] * dm, 0.0) * brow
                tt_inv = _unit_tri_inverse(ld, ii, jj)
                wk = _dot3(tt_inv * jnp.exp(grow), kh)
                us.append(_dot3(tt_inv, vh))
                qd = qh * jnp.exp(gcol)
                l1_ref[d, c, h] = jnp.concatenate([wk, qd], axis=0).astype(BF16)
                tops.append(jnp.where(incl, qks[h] * dm, 0.0) * brow)
                glast = grow[:, last:last + 1]
                bots.append(kts[h] * (jnp.exp(glast - grow) * brow))
                decs.append(jnp.broadcast_to(jnp.exp(glast), (1, HEAD_W)))
            u_ref[d, c] = jnp.concatenate(us, axis=-1)
            dec_ref[d, c] = jnp.concatenate(decs, axis=-1)
            for p in range(N_HEADS // 2):
                top = jnp.concatenate([tops[2 * p], tops[2 * p + 1]], axis=-1)
                bot = jnp.concatenate([bots[2 * p], bots[2 * p + 1]], axis=-1)
                l2_ref[d, c, p] = jnp.concatenate([top, bot], axis=0).astype(BF16)
        return carry

    lax.fori_loop(0, cpt, chunk, 0)


def _gdn_tile_chunks(t):
    return min(t // GDN_CHUNK, 8)


def _gdn_prep(pq, pab, conv_qkv, a_log, dt_bias):
    b, t, w3 = pq.shape
    nc = t // GDN_CHUNK
    cpt = _gdn_tile_chunks(t)
    tt = cpt * GDN_CHUNK
    nt = t // tt
    hb = tt // HALO
    a8 = a_log.reshape(8, 1)
    d8 = dt_bias.reshape(8, 1)
    alog_t = jnp.broadcast_to(a8, (8, GDN_CHUNK))
    dtb_t = jnp.broadcast_to(d8, (8, GDN_CHUNK))
    alog_n = jnp.zeros((1, LANES), F32).at[0, 8:16].set(a8[:, 0])
    dtb_n = jnp.zeros((1, LANES), F32).at[0, 8:16].set(d8[:, 0])
    const = lambda bi, j: (0, 0)
    tok = lambda bi, j: (bi, j, 0)
    return pl.pallas_call(
        functools.partial(_gdn_prep_kernel, cpt=cpt, nt=nt),
        grid=(b, nt),
        in_specs=[pl.BlockSpec((None, tt, w3), tok),
                  pl.BlockSpec((None, HALO, w3), lambda bi, j: (bi, jnp.maximum(j * hb - 1, 0), 0)),
                  pl.BlockSpec((None, HALO, w3), lambda bi, j: (bi, jnp.minimum((j + 1) * hb, t // HALO - 1), 0)),
                  pl.BlockSpec((None, tt, LANES), tok),
                  pl.BlockSpec((3, w3), const),
                  pl.BlockSpec((8, GDN_CHUNK), const), pl.BlockSpec((8, GDN_CHUNK), const),
                  pl.BlockSpec((1, LANES), const), pl.BlockSpec((1, LANES), const)],
        out_specs=[pl.BlockSpec((None, 2, cpt, N_HEADS, 2 * GDN_CHUNK, HEAD_W), lambda bi, j: (bi, 0, j, 0, 0, 0)),
                   pl.BlockSpec((None, 2, cpt, N_HEADS // 2, 3 * GDN_CHUNK, HEAD_W), lambda bi, j: (bi, 0, j, 0, 0, 0)),
                   pl.BlockSpec((None, 2, cpt, GDN_CHUNK, GDN_W), lambda bi, j: (bi, 0, j, 0, 0)),
                   pl.BlockSpec((None, 2, cpt, 1, GDN_W), lambda bi, j: (bi, 0, j, 0, 0))],
        out_shape=[jax.ShapeDtypeStruct((b, 2, nc, N_HEADS, 2 * GDN_CHUNK, HEAD_W), BF16),
                   jax.ShapeDtypeStruct((b, 2, nc, N_HEADS // 2, 3 * GDN_CHUNK, HEAD_W), BF16),
                   jax.ShapeDtypeStruct((b, 2, nc, GDN_CHUNK, GDN_W), F32),
                   jax.ShapeDtypeStruct((b, 2, nc, 1, GDN_W), F32)],
        scratch_shapes=[pltpu.VMEM((tt, GDN_W), F32)] * 3,
        compiler_params=_cparams("parallel", "parallel"),
        name="gdn_prep",
    )(pq, pq, pq, pab, conv_qkv, alog_t, dtb_t, alog_n, dtb_n)


def _gdn_scan_kernel(l1f, l1b, l2f, l2b, uf, ub, decf, decb, s0_ref, of_ref, ob_ref, sfin_ref, s_ref,
                     *, cpg, nj):
    j = pl.program_id(1)

    @pl.when(j == 0)
    def _():
        s_ref[...] = s0_ref[...]

    zero = jnp.zeros((GDN_CHUNK, HEAD_W), BF16)

    def step(i, carry):
        dirs = ((l1f, l2f, uf, decf, of_ref, i), (l1b, l2b, ub, decb, ob_ref, cpg - 1 - i))
        for d, (l1, l2, u, dec, o_ref, c) in enumerate(dirs):
            vts, qss = [], []
            for h in range(N_HEADS):
                sh = s_ref[d, :, h * HEAD_W:(h + 1) * HEAD_W].astype(BF16)
                r1 = jnp.dot(l1[c, h], sh, preferred_element_type=F32)
                vts.append((u[c, :, h * HEAD_W:(h + 1) * HEAD_W] - r1[:GDN_CHUNK]).astype(BF16))
                qss.append(r1[GDN_CHUNK:])
            intra = []
            for p in range(N_HEADS // 2):
                rhs = jnp.concatenate([jnp.concatenate([vts[2 * p], zero], axis=-1),
                                       jnp.concatenate([zero, vts[2 * p + 1]], axis=-1)], axis=0)
                r2 = jnp.dot(l2[c, p], rhs, preferred_element_type=F32)
                intra.append(r2[:GDN_CHUNK])
                lo, hi = 2 * p * HEAD_W, 2 * (p + 1) * HEAD_W
                s_ref[d, :, lo:hi] = s_ref[d, :, lo:hi] * dec[c][:, lo:hi] + r2[GDN_CHUNK:]
            o = jnp.concatenate(qss, axis=-1) + jnp.concatenate(intra, axis=-1)
            o_ref[pl.ds(pl.multiple_of(c * GDN_CHUNK, GDN_CHUNK), GDN_CHUNK), :] = o.astype(o_ref.dtype)
        return carry

    lax.fori_loop(0, cpg, step, 0)

    @pl.when(j == nj - 1)
    def _():
        sfin_ref[...] = s_ref[...]


def _gdn_scan(l1, l2, u, dec, s0):
    b, _, nc = l1.shape[:3]
    cpg = min(nc, 8)
    nj = nc // cpg
    t = nc * GDN_CHUNK
    fwd5 = lambda bi, j: (bi, 0, j, 0, 0, 0)
    bwd5 = lambda bi, j: (bi, 1, nj - 1 - j, 0, 0, 0)
    fwd4 = lambda bi, j: (bi, 0, j, 0, 0)
    bwd4 = lambda bi, j: (bi, 1, nj - 1 - j, 0, 0)
    b1 = (None, None, cpg, N_HEADS, 2 * GDN_CHUNK, HEAD_W)
    b2 = (None, None, cpg, N_HEADS // 2, 3 * GDN_CHUNK, HEAD_W)
    bu = (None, None, cpg, GDN_CHUNK, GDN_W)
    bd = (None, None, cpg, 1, GDN_W)
    st = pl.BlockSpec((None, 2, HEAD_W, GDN_W), lambda bi, j: (bi, 0, 0, 0))
    return pl.pallas_call(
        functools.partial(_gdn_scan_kernel, cpg=cpg, nj=nj),
        grid=(b, nj),
        in_specs=[pl.BlockSpec(b1, fwd5), pl.BlockSpec(b1, bwd5), pl.BlockSpec(b2, fwd5), pl.BlockSpec(b2, bwd5),
                  pl.BlockSpec(bu, fwd4), pl.BlockSpec(bu, bwd4), pl.BlockSpec(bd, fwd4), pl.BlockSpec(bd, bwd4), st],
        out_specs=[pl.BlockSpec((None, cpg * GDN_CHUNK, GDN_W), lambda bi, j: (bi, j, 0)),
                   pl.BlockSpec((None, cpg * GDN_CHUNK, GDN_W), lambda bi, j: (bi, nj - 1 - j, 0)), st],
        out_shape=[jax.ShapeDtypeStruct((b, t, GDN_W), BF16), jax.ShapeDtypeStruct((b, t, GDN_W), BF16),
                   jax.ShapeDtypeStruct((b, 2, HEAD_W, GDN_W), F32)],
        scratch_shapes=[pltpu.VMEM((2, HEAD_W, GDN_W), F32)],
        compiler_params=_cparams("parallel", "arbitrary"),
        name="gdn_scan",
    )(l1, l1, l2, l2, u, u, dec, dec, s0)


def _out_kernel(x_ref, of_ref, ob_ref, z_ref, ya_ref, yc_ref, mod_ref, gon_ref, wo_ref, gpost_ref, o_ref):
    o = of_ref[...].astype(F32) + ob_ref[...].astype(F32)
    z = z_ref[...].astype(F32)
    ybs = []
    for h in range(N_HEADS):
        oh = o[:, h * HEAD_W:(h + 1) * HEAD_W]
        oh = oh * lax.rsqrt(jnp.mean(oh * oh, axis=-1, keepdims=True) + EPS) * gon_ref[...]
        ybs.append((oh * _silu(z[:, h * HEAD_W:(h + 1) * HEAD_W])).astype(BF16))
    y = jnp.concatenate([ya_ref[...]] + ybs + [yc_ref[...]], axis=-1)
    r = jnp.dot(y, wo_ref[...], preferred_element_type=F32)
    o_ref[...] = x_ref[...] + mod_ref[2:3, :] * _rms(r, gpost_ref[...])


def _out_proj(x, of, ob, z, ya, yc, mod, g_onorm, w_o, g_post):
    b, t, d = x.shape
    tm = _row_tile(t)
    const = lambda bi, j: (0, 0)
    tok = lambda bi, j: (bi, j, 0)
    return pl.pallas_call(
        _out_kernel,
        grid=(b, t // tm),
        in_specs=[pl.BlockSpec((None, tm, d), tok),
                  pl.BlockSpec((None, tm, GDN_W), tok), pl.BlockSpec((None, tm, GDN_W), tok),
                  pl.BlockSpec((None, tm, GDN_W), tok),
                  pl.BlockSpec((None, tm, A_W), tok), pl.BlockSpec((None, tm, C_W), tok),
                  pl.BlockSpec((None, 8, d), lambda bi, j: (bi, 0, 0)),
                  pl.BlockSpec((1, HEAD_W), const), pl.BlockSpec(w_o.shape, const), pl.BlockSpec((1, d), const)],
        out_specs=pl.BlockSpec((None, tm, d), tok),
        out_shape=jax.ShapeDtypeStruct((b, t, d), F32),
        compiler_params=_cparams("parallel", "parallel"),
        name="out_proj",
    )(x, of, ob, z, ya, yc, mod, g_onorm, w_o, g_post)


FFN_TILE = 1408


def _ffn_kernel(x_ref, mod_ref, gpre_ref, w1_ref, w2_ref, gpost_ref, o_ref, *, hidden):
    x = x_ref[...]
    hb = (_rms(x, gpre_ref[...]) * (1.0 + mod_ref[4:5, :]) + mod_ref[3:4, :]).astype(BF16)
    acc = None
    for c0 in range(0, hidden, FFN_TILE):
        g = jnp.dot(hb, w1_ref[:, c0:c0 + FFN_TILE], preferred_element_type=F32)
        u = jnp.dot(hb, w1_ref[:, hidden + c0:hidden + c0 + FFN_TILE], preferred_element_type=F32)
        part = jnp.dot((_silu(g) * u).astype(BF16), w2_ref[c0:c0 + FFN_TILE, :], preferred_element_type=F32)
        acc = part if acc is None else acc + part
    o_ref[...] = x + mod_ref[5:6, :] * _rms(acc, gpost_ref[...])


def _ffn(x, mod, g_pre, w1, w2, g_post):
    b, t, d = x.shape
    hidden = w2.shape[0]
    tm = _row_tile(t)
    const = lambda bi, j: (0, 0)
    tok = lambda bi, j: (bi, j, 0)
    return pl.pallas_call(
        functools.partial(_ffn_kernel, hidden=hidden),
        grid=(b, t // tm),
        in_specs=[pl.BlockSpec((None, tm, d), tok),
                  pl.BlockSpec((None, 8, d), lambda bi, j: (bi, 0, 0)),
                  pl.BlockSpec((1, d), const),
                  pl.BlockSpec(w1.shape, const, pipeline_mode=pl.Buffered(1)),
                  pl.BlockSpec(w2.shape, const, pipeline_mode=pl.Buffered(1)),
                  pl.BlockSpec((1, d), const)],
        out_specs=pl.BlockSpec((None, tm, d), tok),
        out_shape=jax.ShapeDtypeStruct((b, t, d), F32),
        compiler_params=_cparams("parallel", "parallel"),
        name="ffn",
    )(x, mod, g_pre, w1, w2, g_post)


OFF_A, OFF_Q, OFF_AB, OFF_Z, OFF_C = 0, 3 * A_W, 3 * A_W + 3 * GDN_W, 3 * A_W + 3 * GDN_W + 16, 3 * A_W + 4 * GDN_W + 16


def _mod_rows(rows, b, d):
    m = jnp.broadcast_to(rows.reshape(-1, 6, d), (b, 6, d))
    return jnp.pad(m, ((0, 0), (0, 2), (0, 0)))


def kernel(x, c, ctx, c_ctx, w_mod, b_mod, g_pre_mix, g_post_mix, g_pre_ffn, g_post_ffn, w_in, conv_a, conv_qkv,
           a_log, dt_bias, g_onorm, ln_c_g, ln_c_b, w_s, b_s, w_o, w_ffn_in, w_ffn_out):
    b, t, d = x.shape
    depth = w_mod.shape[0]
    cc = jnp.concatenate([c, c_ctx[None, :], jnp.zeros((16 - b - 1, d), F32)], axis=0)
    mod = _modulation(cc, w_mod, b_mod)
    s_zero = jnp.zeros((b, 2, HEAD_W, GDN_W), F32)
    for l in range(depth):
        last = l == depth - 1
        modx = _mod_rows(mod[l, :b], b, d)
        modc = _mod_rows(mod[l, b:b + 1], b, d)
        wl = w_in[l].astype(BF16)
        ws_in = [wl[:, OFF_Q:OFF_AB], wl[:, OFF_Z:OFF_C], wl[:, OFF_A:OFF_Q], wl[:, OFF_C:],
                 jnp.pad(wl[:, OFF_AB:OFF_Z], ((0, 0), (0, LANES - 16)))]
        gpm, gqm = g_pre_mix[l][None, :], g_post_mix[l][None, :]
        gpf, gqf = g_pre_ffn[l][None, :], g_post_ffn[l][None, :]
        sgu_w = w_s[l].reshape(C_GROUPS * C_CHUNK, C_CHUNK).astype(BF16)
        sgu_b = jnp.repeat(b_s[l].T, C_GD, axis=1)
        lng, lnb = ln_c_g[l][None, :], ln_c_b[l][None, :]
        wo = w_o[l].astype(BF16)
        w1, w2 = w_ffn_in[l].astype(BF16), w_ffn_out[l].astype(BF16)
        gon = g_onorm[l][None, :]

        def mixer_inputs(stream, m):
            pq, pz, pa, pc, pab = _in_proj(stream, m, gpm, ws_in)
            return pq, pz, pa, pc, _gdn_prep(pq, pab, conv_qkv[l], a_log[l], dt_bias[l])

        def finish(stream, m, of, ob, pz, pa, pc, grid_conv):
            ya, yc = _mixers_ac(pa, pc, conv_a[l], lng, lnb, sgu_w, sgu_b, grid_conv=grid_conv)
            stream = _out_proj(stream, of, ob, pz, ya, yc, m, gon, wo, gqm)
            return _ffn(stream, m, gpf, w1, w2, gqf)

        pq, pz, pa, pc, ops = mixer_inputs(ctx, modc)
        of, ob, s_ctx = _gdn_scan(*ops, s_zero)
        if not last:
            ctx = finish(ctx, modc, of, ob, pz, pa, pc, False)
        pq, pz, pa, pc, ops = mixer_inputs(x, modx)
        of, ob, _ = _gdn_scan(*ops, s_ctx)
        x = finish(x, modx, of, ob, pz, pa, pc, True)
    return x
```

```python
import functools

import jax
import jax.numpy as jnp
from jax import lax
from jax.experimental import pallas as pl
from jax.experimental.pallas import tpu as pltpu

F32 = jnp.float32
BF16 = jnp.bfloat16
HIGHEST = lax.Precision.HIGHEST

EPS = 1e-6
GRID_W = 64
N_HEADS = 4
HEAD_W = 128
GDN_W = N_HEADS * HEAD_W
GDN_CHUNK = 64
A_W = 256
A_HORIZ = 128
C_W = 256
C_GROUPS = 4
C_GD = C_W // C_GROUPS
C_CHUNK = 128
LANES = 128
VMEM_LIMIT = 56 * 1024 * 1024


def _cparams(*sem):
    return pltpu.CompilerParams(dimension_semantics=sem, vmem_limit_bytes=VMEM_LIMIT)


def _hdot(a, b):
    return jnp.dot(a, b, precision=HIGHEST, preferred_element_type=F32)


def _rms(x, g):
    return x * lax.rsqrt(jnp.mean(x * x, axis=-1, keepdims=True) + EPS) * g


def _silu(x):
    return x * jax.nn.sigmoid(x)


def _gelu(x):
    return 0.5 * x * (1.0 + lax.erf(x * 0.7071067811865476))


def _softplus(x):
    return jnp.maximum(x, 0.0) + jnp.log1p(jnp.exp(-jnp.abs(x)))


def _mod_kernel(c_ref, w_ref, b_ref, o_ref):
    o_ref[...] = _hdot(_silu(c_ref[...]), w_ref[...]) + b_ref[...]


def _modulation(cc, w_mod, b_mod):
    depth, d, six_d = w_mod.shape
    r = cc.shape[0]
    return pl.pallas_call(
        _mod_kernel,
        grid=(depth, six_d // d),
        in_specs=[pl.BlockSpec((r, d), lambda l, j: (0, 0)),
                  pl.BlockSpec((None, d, d), lambda l, j: (l, 0, j)),
                  pl.BlockSpec((None, 1, d), lambda l, j: (l, 0, j))],
        out_specs=pl.BlockSpec((None, r, d), lambda l, j: (l, 0, j)),
        out_shape=jax.ShapeDtypeStruct((depth, r, six_d), F32),
        compiler_params=_cparams("parallel", "parallel"),
        name="modulation",
    )(cc, w_mod, b_mod.reshape(depth, 1, six_d))


def _in_kernel(x_ref, mod_ref, g_ref, wq, wz, wa, wc, wab, oq, oz, oa, oc, oab):
    h = _rms(x_ref[...], g_ref[...]) * (1.0 + mod_ref[1:2, :]) + mod_ref[0:1, :]
    hb = h.astype(BF16)
    for w, o in ((wq, oq), (wz, oz), (wa, oa), (wc, oc)):
        o[...] = jnp.dot(hb, w[...], preferred_element_type=F32).astype(BF16)
    oab[...] = jnp.dot(hb, wab[...], preferred_element_type=F32)


def _row_tile(t):
    return min(t, 512)


def _in_proj(x, mod, g_pre, ws):
    b, t, d = x.shape
    tm = _row_tile(t)
    widths = [w.shape[1] for w in ws]
    const = lambda bi, j: (0, 0)
    tok = lambda bi, j: (bi, j, 0)
    return pl.pallas_call(
        _in_kernel,
        grid=(b, t // tm),
        in_specs=[pl.BlockSpec((None, tm, d), tok),
                  pl.BlockSpec((None, 8, d), lambda bi, j: (bi, 0, 0)),
                  pl.BlockSpec((1, d), const)]
                 + [pl.BlockSpec((d, n), const) for n in widths],
        out_specs=[pl.BlockSpec((None, tm, n), tok) for n in widths],
        out_shape=[jax.ShapeDtypeStruct((b, t, n), BF16) for n in widths[:-1]]
                  + [jax.ShapeDtypeStruct((b, t, widths[-1]), F32)],
        compiler_params=_cparams("parallel", "parallel"),
        name="in_proj",
    )(x, mod, g_pre, *ws)


def _sgu(pc, lng, lnb, ws, bsm):
    tt = pc.shape[0]
    u = _gelu(pc[:, :C_W])
    v = _gelu(pc[:, C_W:])
    mu = jnp.mean(v, axis=-1, keepdims=True)
    vc = v - mu
    var = jnp.mean(vc * vc, axis=-1, keepdims=True)
    vb = (vc * lax.rsqrt(var + EPS) * lng + lnb).astype(BF16)
    grp = lax.broadcasted_iota(jnp.int32, (C_CHUNK, C_W), 1) // C_GD
    outs = []
    for n in range(tt // C_CHUNK):
        r = jnp.dot(ws, vb[n * C_CHUNK:(n + 1) * C_CHUNK, :], preferred_element_type=F32)
        mixed = r[3 * C_CHUNK:]
        for g in (2, 1, 0):
            mixed = jnp.where(grp == g, r[g * C_CHUNK:(g + 1) * C_CHUNK], mixed)
        outs.append(mixed + bsm)
    return u * jnp.concatenate(outs, axis=0)


def _mix_grid_kernel(pa_ref, prev_ref, next_ref, pc_ref, cw_ref, lng_ref, lnb_ref, ws_ref, bsm_ref,
                     ya_ref, yc_ref, *, nj):
    j = pl.program_id(1)
    pa = pa_ref[...].astype(F32)
    tt = pa.shape[0]
    cw = cw_ref[...]
    gate = pa[:, :A_W]
    m = pa[:, A_W:2 * A_W] * pa[:, 2 * A_W:]
    col = lax.broadcasted_iota(jnp.int32, (tt, A_HORIZ), 0) % GRID_W
    mh = m[:, :A_HORIZ]
    left = jnp.where(col == 0, 0.0, pltpu.roll(mh, 1, 0))
    right = jnp.where(col == GRID_W - 1, 0.0, pltpu.roll(mh, tt - 1, 0))
    yh = cw[0:1, :A_HORIZ] * left + cw[1:2, :A_HORIZ] * mh + cw[2:3, :A_HORIZ] * right
    mv = m[:, A_HORIZ:]
    pv = prev_ref[...].astype(F32)
    nv = next_ref[...].astype(F32)
    has_prev = jnp.where(j > 0, 1.0, 0.0)
    has_next = jnp.where(j < nj - 1, 1.0, 0.0)
    pm = pv[:, A_W + A_HORIZ:2 * A_W] * pv[:, 2 * A_W + A_HORIZ:] * has_prev
    nm = nv[:, A_W + A_HORIZ:2 * A_W] * nv[:, 2 * A_W + A_HORIZ:] * has_next
    up = jnp.concatenate([pm, mv[:tt - GRID_W]], axis=0)
    down = jnp.concatenate([mv[GRID_W:], nm], axis=0)
    yv = cw[0:1, A_HORIZ:] * up + cw[1:2, A_HORIZ:] * mv + cw[2:3, A_HORIZ:] * down
    ya_ref[...] = (gate * jnp.concatenate([yh, yv], axis=-1)).astype(BF16)
    yc_ref[...] = _sgu(pc_ref[...].astype(F32), lng_ref[...], lnb_ref[...], ws_ref[...],
                       bsm_ref[...]).astype(BF16)


def _mix_seq_kernel(pa_ref, pc_ref, cw_ref, lng_ref, lnb_ref, ws_ref, bsm_ref, ya_ref, yc_ref):
    pa = pa_ref[...].astype(F32)
    tt = pa.shape[0]
    cw = cw_ref[...]
    m = pa[:, A_W:2 * A_W] * pa[:, 2 * A_W:]
    row = lax.broadcasted_iota(jnp.int32, (tt, A_W), 0)
    left = jnp.where(row == 0, 0.0, pltpu.roll(m, 1, 0))
    right = jnp.where(row == tt - 1, 0.0, pltpu.roll(m, tt - 1, 0))
    y = cw[0:1] * left + cw[1:2] * m + cw[2:3] * right
    ya_ref[...] = (pa[:, :A_W] * y).astype(BF16)
    yc_ref[...] = _sgu(pc_ref[...].astype(F32), lng_ref[...], lnb_ref[...], ws_ref[...],
                       bsm_ref[...]).astype(BF16)


def _mixers_ac(pa, pc, conv_a, ln_g, ln_b, ws, bsm, *, grid_conv):
    b, t, _ = pa.shape
    const = lambda bi, j: (0, 0)
    tok = lambda bi, j: (bi, j, 0)
    small = [pl.BlockSpec(conv_a.shape, const), pl.BlockSpec(ln_g.shape, const),
             pl.BlockSpec(ln_b.shape, const), pl.BlockSpec(ws.shape, const), pl.BlockSpec(bsm.shape, const)]
    outs = dict(out_shape=[jax.ShapeDtypeStruct((b, t, A_W), BF16), jax.ShapeDtypeStruct((b, t, C_W), BF16)],
                compiler_params=_cparams("parallel", "parallel"))
    if not grid_conv:
        return pl.pallas_call(
            _mix_seq_kernel, grid=(b, 1),
            in_specs=[pl.BlockSpec((None, t, 3 * A_W), tok), pl.BlockSpec((None, t, 2 * C_W), tok)] + small,
            out_specs=[pl.BlockSpec((None, t, A_W), tok), pl.BlockSpec((None, t, C_W), tok)],
            name="mixers_ac_seq", **outs,
        )(pa, pc, conv_a, ln_g, ln_b, ws, bsm)
    tt = _row_tile(t)
    nj = t // tt
    rows = tt // GRID_W
    last = t // GRID_W - 1
    return pl.pallas_call(
        functools.partial(_mix_grid_kernel, nj=nj), grid=(b, nj),
        in_specs=[pl.BlockSpec((None, tt, 3 * A_W), tok),
                  pl.BlockSpec((None, GRID_W, 3 * A_W), lambda bi, j: (bi, jnp.maximum(j * rows - 1, 0), 0)),
                  pl.BlockSpec((None, GRID_W, 3 * A_W), lambda bi, j: (bi, jnp.minimum((j + 1) * rows, last), 0)),
                  pl.BlockSpec((None, tt, 2 * C_W), tok)] + small,
        out_specs=[pl.BlockSpec((None, tt, A_W), tok), pl.BlockSpec((None, tt, C_W), tok)],
        name="mixers_ac_grid", **outs,
    )(pa, pa, pa, pc, conv_a, ln_g, ln_b, ws, bsm)


HALO = 16
PAIRS = N_HEADS // 2
PAIR_W = 2 * HEAD_W
CHUNKS_PER_STEP = 2


def _block_diag2(y):
    lane = lax.broadcasted_iota(jnp.int32, y.shape, 1)
    return jnp.concatenate([jnp.where(lane < GDN_CHUNK, y, 0.0), jnp.where(lane >= GDN_CHUNK, y, 0.0)],
                           axis=0).astype(BF16)


def _pdot(a, b):
    return jnp.dot(a.astype(BF16), _block_diag2(b), preferred_element_type=F32)


def _unit_tri_inverses(lds, ii, jl):
    xs = [jnp.where(ii == jl, 1.0, 0.0) - jnp.where((ii >> 1) == (jl >> 1), ld, 0.0) for ld in lds]
    for k in range(1, 6):
        mk = ((ii >> (k + 1)) == (jl >> (k + 1))) & ((ii >> k) != (jl >> k))
        ts = [_pdot(jnp.where(mk, ld, 0.0), x) for ld, x in zip(lds, xs)]
        xs = [x - _pdot(x, t) for x, t in zip(xs, ts)]
    return xs


def _gdn_prep_kernel(pq_ref, pqp_ref, pqn_ref, ab_ref, cw_ref, alog_t, dtb_t, alog_n, dtb_n,
                     l1_ref, l2_ref, u_ref, dec_ref, q_s, k_s, v_s, *, cpt, nt):
    j = pl.program_id(1)
    tt = cpt * GDN_CHUNK
    x = pq_ref[...].astype(F32)
    w3 = 3 * GDN_W
    has_prev = jnp.where(j > 0, 1.0, 0.0)
    has_next = jnp.where(j < nt - 1, 1.0, 0.0)
    prev_row = pqp_ref[HALO - 1:HALO, :].astype(F32) * has_prev
    next_row = pqn_ref[0:1, :].astype(F32) * has_next
    row = lax.broadcasted_iota(jnp.int32, (tt, w3), 0)
    xm1 = jnp.where(row == 0, prev_row, pltpu.roll(x, 1, 0))
    xp1 = jnp.where(row == tt - 1, next_row, pltpu.roll(x, tt - 1, 0))
    cw = cw_ref[...]
    s = _silu(cw[0:1] * xm1 + cw[1:2] * x + cw[2:3] * xp1)
    for h in range(N_HEADS):
        lo, hi = h * HEAD_W, (h + 1) * HEAD_W
        qh = s[:, lo:hi]
        q_s[:, lo:hi] = qh * (lax.rsqrt(jnp.sum(qh * qh, axis=-1, keepdims=True) + EPS) * HEAD_W ** -0.5)
        kh = s[:, GDN_W + lo:GDN_W + hi]
        k_s[:, lo:hi] = kh * lax.rsqrt(jnp.sum(kh * kh, axis=-1, keepdims=True) + EPS)
    v_s[...] = s[:, 2 * GDN_W:]

    cc = GDN_CHUNK
    ii = lax.broadcasted_iota(jnp.int32, (cc, LANES), 0)
    lane = lax.broadcasted_iota(jnp.int32, (cc, LANES), 1)
    jl = lane & (cc - 1)
    first_head = lane < cc
    lane_row = lax.broadcasted_iota(jnp.int32, (1, LANES), 1)
    i64 = lax.broadcasted_iota(jnp.int32, (cc, cc), 0)
    j64 = lax.broadcasted_iota(jnp.int32, (cc, cc), 1)
    ones_le = jnp.where(i64 <= j64, 1.0, 0.0)
    ones_ge = jnp.where(i64 >= j64, 1.0, 0.0)
    i128 = lax.broadcasted_iota(jnp.int32, (LANES, LANES), 0)
    j128 = lax.broadcasted_iota(jnp.int32, (LANES, LANES), 1)
    same_head = (i128 >> 6) == (j128 >> 6)
    bd_le = jnp.where(same_head & ((i128 & 63) <= (j128 & 63)), 1.0, 0.0)
    bd_ge = jnp.where(same_head & ((i128 & 63) >= (j128 & 63)), 1.0, 0.0)
    zero_k = jnp.zeros((cc, HEAD_W), BF16)

    def setup(c):
        r0 = pl.multiple_of(c * cc, cc)
        kc = k_s[pl.ds(r0, cc), :]
        qc = q_s[pl.ds(r0, cc), :]
        vb = v_s[pl.ds(r0, cc), :].astype(BF16)
        kb = kc.astype(BF16)
        abn = ab_ref[pl.ds(r0, cc), :]
        g_n = -jnp.exp(alog_n[...]) * _softplus(abn + dtb_n[...])
        gam_n = (_hdot(ones_ge, g_n), _hdot(ones_le, g_n))
        gt = jnp.concatenate([abn, pltpu.roll(abn, LANES - 8, 1)], axis=0).T[0:8, :]
        beta = jax.nn.sigmoid(gt[0:4, :])
        g_t = -jnp.exp(alog_t[...]) * _softplus(gt[4:8, :] + dtb_t[...])
        gam_t = (_hdot(g_t, bd_le), _hdot(g_t, bd_ge))
        pairs = []
        for p in range(PAIRS):
            lo = p * PAIR_W
            k0, k1 = kb[:, lo:lo + HEAD_W], kb[:, lo + HEAD_W:lo + PAIR_W]
            kbd = jnp.concatenate([jnp.concatenate([k0, zero_k], axis=1),
                                   jnp.concatenate([zero_k, k1], axis=1)], axis=0)
            vbd = jnp.concatenate([jnp.concatenate([vb[:, lo:lo + HEAD_W], zero_k], axis=1),
                                   jnp.concatenate([zero_k, vb[:, lo + HEAD_W:lo + PAIR_W]], axis=1)], axis=0)
            kq = lax.dot_general(jnp.concatenate([kb[:, lo:lo + PAIR_W], qc[:, lo:lo + PAIR_W].astype(BF16)], axis=0),
                                 kbd, (((1,), (1,)), ((), ())), preferred_element_type=F32)
            kt = jnp.concatenate([kc[:, lo:lo + HEAD_W], kc[:, lo + HEAD_W:lo + PAIR_W]], axis=0).T
            pairs.append((kbd, vbd, kq[:cc], kq[cc:], kt, qc[:, lo:lo + HEAD_W], qc[:, lo + HEAD_W:lo + PAIR_W]))
        probs = []
        for d in range(2):
            incl = (ii >= jl) if d == 0 else (ii <= jl)
            strict = (ii > jl) if d == 0 else (ii < jl)
            for p in range(PAIRS):
                kbd, vbd, kkt, qk, kt, q0, q1 = pairs[p]
                r = 2 * d + p
                grow = gam_t[d][r:r + 1, :]
                brow = beta[r:r + 1, :]
                l0 = 2 * (2 + d) + p
                gc0 = jnp.broadcast_to(gam_n[d][:, l0:l0 + 1], (cc, LANES))
                gc1 = jnp.broadcast_to(gam_n[d][:, 8 + l0:9 + l0], (cc, LANES))
                dm = jnp.exp(jnp.where(incl, jnp.where(first_head, gc0, gc1) - grow, -1e30))
                ld = jnp.where(strict, kkt * dm, 0.0) * brow
                top = jnp.where(incl, qk * dm, 0.0) * brow
                probs.append(dict(c=c, d=d, p=p, ld=ld, top=top, grow=grow, brow=brow, gc0=gc0, gc1=gc1,
                                  kbd=kbd, vbd=vbd, kt=kt, q0=q0, q1=q1))
        return probs

    def emit(pr, x):
        c, d, p, grow, brow = pr["c"], pr["d"], pr["p"], pr["grow"], pr["brow"]
        wk = jnp.dot((x * jnp.exp(grow)).astype(BF16), pr["kbd"], preferred_element_type=F32)
        u_ref[d, c, :, p * PAIR_W:(p + 1) * PAIR_W] = jnp.dot(x.astype(BF16), pr["vbd"], preferred_element_type=F32)
        l1_ref[d, c, 2 * p] = jnp.concatenate([wk[:, :HEAD_W], pr["q0"] * jnp.exp(pr["gc0"])], axis=0).astype(BF16)
        l1_ref[d, c, 2 * p + 1] = jnp.concatenate([wk[:, HEAD_W:], pr["q1"] * jnp.exp(pr["gc1"])], axis=0).astype(BF16)
        last = cc - 1 if d == 0 else 0
        gl0 = grow[:, last:last + 1]
        gl1 = grow[:, cc + last:cc + last + 1]
        tail = jnp.exp(jnp.where(lane_row < cc, gl0, gl1) - grow) * brow
        l2_ref[d, c, p] = jnp.concatenate([pr["top"], pr["kt"] * tail], axis=0).astype(BF16)
        dec_ref[d, c, :, p * PAIR_W:(p + 1) * PAIR_W] = jnp.concatenate(
            [jnp.broadcast_to(jnp.exp(gl0), (1, HEAD_W)), jnp.broadcast_to(jnp.exp(gl1), (1, HEAD_W))], axis=1)

    step_chunks = min(cpt, CHUNKS_PER_STEP)

    def step(i, carry):
        probs = []
        for n in range(step_chunks):
            probs += setup(i * step_chunks + n)
        xs = _unit_tri_inverses([pr["ld"] for pr in probs], ii, jl)
        for pr, xi in zip(probs, xs):
            emit(pr, xi)
        return carry

    lax.fori_loop(0, cpt // step_chunks, step, 0)


def _gdn_tile_chunks(t):
    return min(t // GDN_CHUNK, 8)


def _gdn_prep(pq, pab, conv_qkv, a_log, dt_bias):
    b, t, w3 = pq.shape
    nc = t // GDN_CHUNK
    cpt = _gdn_tile_chunks(t)
    tt = cpt * GDN_CHUNK
    nt = t // tt
    hb = tt // HALO
    alog_t = jnp.repeat(a_log.reshape(2 * PAIRS, 2), GDN_CHUNK, axis=1)
    dtb_t = jnp.repeat(dt_bias.reshape(2 * PAIRS, 2), GDN_CHUNK, axis=1)
    lanes_n = jnp.array([[8 * (h % 2) + 2 * (2 + d) + h // 2 for h in range(N_HEADS)] for d in range(2)])
    alog_n = jnp.zeros((1, LANES), F32).at[0, lanes_n].set(a_log)
    dtb_n = jnp.zeros((1, LANES), F32).at[0, lanes_n].set(dt_bias)
    const = lambda bi, j: (0, 0)
    tok = lambda bi, j: (bi, j, 0)
    return pl.pallas_call(
        functools.partial(_gdn_prep_kernel, cpt=cpt, nt=nt),
        grid=(b, nt),
        in_specs=[pl.BlockSpec((None, tt, w3), tok),
                  pl.BlockSpec((None, HALO, w3), lambda bi, j: (bi, jnp.maximum(j * hb - 1, 0), 0)),
                  pl.BlockSpec((None, HALO, w3), lambda bi, j: (bi, jnp.minimum((j + 1) * hb, t // HALO - 1), 0)),
                  pl.BlockSpec((None, tt, LANES), tok),
                  pl.BlockSpec((3, w3), const),
                  pl.BlockSpec((2 * PAIRS, LANES), const), pl.BlockSpec((2 * PAIRS, LANES), const),
                  pl.BlockSpec((1, LANES), const), pl.BlockSpec((1, LANES), const)],
        out_specs=[pl.BlockSpec((None, 2, cpt, N_HEADS, 2 * GDN_CHUNK, HEAD_W), lambda bi, j: (bi, 0, j, 0, 0, 0)),
                   pl.BlockSpec((None, 2, cpt, PAIRS, 3 * GDN_CHUNK, HEAD_W), lambda bi, j: (bi, 0, j, 0, 0, 0)),
                   pl.BlockSpec((None, 2, cpt, GDN_CHUNK, GDN_W), lambda bi, j: (bi, 0, j, 0, 0)),
                   pl.BlockSpec((None, 2, cpt, 1, GDN_W), lambda bi, j: (bi, 0, j, 0, 0))],
        out_shape=[jax.ShapeDtypeStruct((b, 2, nc, N_HEADS, 2 * GDN_CHUNK, HEAD_W), BF16),
                   jax.ShapeDtypeStruct((b, 2, nc, PAIRS, 3 * GDN_CHUNK, HEAD_W), BF16),
                   jax.ShapeDtypeStruct((b, 2, nc, GDN_CHUNK, GDN_W), F32),
                   jax.ShapeDtypeStruct((b, 2, nc, 1, GDN_W), F32)],
        scratch_shapes=[pltpu.VMEM((tt, GDN_W), F32)] * 3,
        compiler_params=_cparams("parallel", "parallel"),
        name="gdn_prep",
    )(pq, pq, pq, pab, conv_qkv, alog_t, dtb_t, alog_n, dtb_n)


def _gdn_scan_kernel(l1f, l1b, l2f, l2b, uf, ub, decf, decb, s0_ref, of_ref, ob_ref, sfin_ref, s_ref,
                     *, cpg, nj):
    j = pl.program_id(1)

    @pl.when(j == 0)
    def _():
        s_ref[...] = s0_ref[...]

    zero = jnp.zeros((GDN_CHUNK, HEAD_W), BF16)

    def step(i, carry):
        dirs = ((l1f, l2f, uf, decf, of_ref, i), (l1b, l2b, ub, decb, ob_ref, cpg - 1 - i))
        for d, (l1, l2, u, dec, o_ref, c) in enumerate(dirs):
            vts, qss = [], []
            for h in range(N_HEADS):
                sh = s_ref[d, :, h * HEAD_W:(h + 1) * HEAD_W].astype(BF16)
                r1 = jnp.dot(l1[c, h], sh, preferred_element_type=F32)
                vts.append((u[c, :, h * HEAD_W:(h + 1) * HEAD_W] - r1[:GDN_CHUNK]).astype(BF16))
                qss.append(r1[GDN_CHUNK:])
            intra = []
            for p in range(N_HEADS // 2):
                rhs = jnp.concatenate([jnp.concatenate([vts[2 * p], zero], axis=-1),
                                       jnp.concatenate([zero, vts[2 * p + 1]], axis=-1)], axis=0)
                r2 = jnp.dot(l2[c, p], rhs, preferred_element_type=F32)
                intra.append(r2[:GDN_CHUNK])
                lo, hi = 2 * p * HEAD_W, 2 * (p + 1) * HEAD_W
                s_ref[d, :, lo:hi] = s_ref[d, :, lo:hi] * dec[c][:, lo:hi] + r2[GDN_CHUNK:]
            o = jnp.concatenate(qss, axis=-1) + jnp.concatenate(intra, axis=-1)
            o_ref[pl.ds(pl.multiple_of(c * GDN_CHUNK, GDN_CHUNK), GDN_CHUNK), :] = o.astype(o_ref.dtype)
        return carry

    lax.fori_loop(0, cpg, step, 0)

    @pl.when(j == nj - 1)
    def _():
        sfin_ref[...] = s_ref[...]


def _gdn_scan(l1, l2, u, dec, s0):
    b, _, nc = l1.shape[:3]
    cpg = min(nc, 8)
    nj = nc // cpg
    t = nc * GDN_CHUNK
    fwd5 = lambda bi, j: (bi, 0, j, 0, 0, 0)
    bwd5 = lambda bi, j: (bi, 1, nj - 1 - j, 0, 0, 0)
    fwd4 = lambda bi, j: (bi, 0, j, 0, 0)
    bwd4 = lambda bi, j: (bi, 1, nj - 1 - j, 0, 0)
    b1 = (None, None, cpg, N_HEADS, 2 * GDN_CHUNK, HEAD_W)
    b2 = (None, None, cpg, N_HEADS // 2, 3 * GDN_CHUNK, HEAD_W)
    bu = (None, None, cpg, GDN_CHUNK, GDN_W)
    bd = (None, None, cpg, 1, GDN_W)
    st = pl.BlockSpec((None, 2, HEAD_W, GDN_W), lambda bi, j: (bi, 0, 0, 0))
    return pl.pallas_call(
        functools.partial(_gdn_scan_kernel, cpg=cpg, nj=nj),
        grid=(b, nj),
        in_specs=[pl.BlockSpec(b1, fwd5), pl.BlockSpec(b1, bwd5), pl.BlockSpec(b2, fwd5), pl.BlockSpec(b2, bwd5),
                  pl.BlockSpec(bu, fwd4), pl.BlockSpec(bu, bwd4), pl.BlockSpec(bd, fwd4), pl.BlockSpec(bd, bwd4), st],
        out_specs=[pl.BlockSpec((None, cpg * GDN_CHUNK, GDN_W), lambda bi, j: (bi, j, 0)),
                   pl.BlockSpec((None, cpg * GDN_CHUNK, GDN_W), lambda bi, j: (bi, nj - 1 - j, 0)), st],
        out_shape=[jax.ShapeDtypeStruct((b, t, GDN_W), BF16), jax.ShapeDtypeStruct((b, t, GDN_W), BF16),
                   jax.ShapeDtypeStruct((b, 2, HEAD_W, GDN_W), F32)],
        scratch_shapes=[pltpu.VMEM((2, HEAD_W, GDN_W), F32)],
        compiler_params=_cparams("parallel", "arbitrary"),
        name="gdn_scan",
    )(l1, l1, l2, l2, u, u, dec, dec, s0)


def _out_kernel(x_ref, of_ref, ob_ref, z_ref, ya_ref, yc_ref, mod_ref, gon_ref, wo_ref, gpost_ref, o_ref):
    o = of_ref[...].astype(F32) + ob_ref[...].astype(F32)
    z = z_ref[...].astype(F32)
    ybs = []
    for h in range(N_HEADS):
        oh = o[:, h * HEAD_W:(h + 1) * HEAD_W]
        oh = oh * lax.rsqrt(jnp.mean(oh * oh, axis=-1, keepdims=True) + EPS) * gon_ref[...]
        ybs.append((oh * _silu(z[:, h * HEAD_W:(h + 1) * HEAD_W])).astype(BF16))
    y = jnp.concatenate([ya_ref[...]] + ybs + [yc_ref[...]], axis=-1)
    r = jnp.dot(y, wo_ref[...], preferred_element_type=F32)
    o_ref[...] = x_ref[...] + mod_ref[2:3, :] * _rms(r, gpost_ref[...])


def _out_proj(x, of, ob, z, ya, yc, mod, g_onorm, w_o, g_post):
    b, t, d = x.shape
    tm = _row_tile(t)
    const = lambda bi, j: (0, 0)
    tok = lambda bi, j: (bi, j, 0)
    return pl.pallas_call(
        _out_kernel,
        grid=(b, t // tm),
        in_specs=[pl.BlockSpec((None, tm, d), tok),
                  pl.BlockSpec((None, tm, GDN_W), tok), pl.BlockSpec((None, tm, GDN_W), tok),
                  pl.BlockSpec((None, tm, GDN_W), tok),
                  pl.BlockSpec((None, tm, A_W), tok), pl.BlockSpec((None, tm, C_W), tok),
                  pl.BlockSpec((None, 8, d), lambda bi, j: (bi, 0, 0)),
                  pl.BlockSpec((1, HEAD_W), const), pl.BlockSpec(w_o.shape, const), pl.BlockSpec((1, d), const)],
        out_specs=pl.BlockSpec((None, tm, d), tok),
        out_shape=jax.ShapeDtypeStruct((b, t, d), F32),
        compiler_params=_cparams("parallel", "parallel"),
        name="out_proj",
    )(x, of, ob, z, ya, yc, mod, g_onorm, w_o, g_post)


FFN_TILE = 1408


def _ffn_kernel(x_ref, mod_ref, gpre_ref, w1_ref, w2_ref, gpost_ref, o_ref, *, hidden):
    x = x_ref[...]
    hb = (_rms(x, gpre_ref[...]) * (1.0 + mod_ref[4:5, :]) + mod_ref[3:4, :]).astype(BF16)
    acc = None
    for c0 in range(0, hidden, FFN_TILE):
        g = jnp.dot(hb, w1_ref[:, c0:c0 + FFN_TILE], preferred_element_type=F32)
        u = jnp.dot(hb, w1_ref[:, hidden + c0:hidden + c0 + FFN_TILE], preferred_element_type=F32)
        part = jnp.dot((_silu(g) * u).astype(BF16), w2_ref[c0:c0 + FFN_TILE, :], preferred_element_type=F32)
        acc = part if acc is None else acc + part
    o_ref[...] = x + mod_ref[5:6, :] * _rms(acc, gpost_ref[...])


def _ffn(x, mod, g_pre, w1, w2, g_post):
    b, t, d = x.shape
    hidden = w2.shape[0]
    tm = _row_tile(t)
    const = lambda bi, j: (0, 0)
    tok = lambda bi, j: (bi, j, 0)
    return pl.pallas_call(
        functools.partial(_ffn_kernel, hidden=hidden),
        grid=(b, t // tm),
        in_specs=[pl.BlockSpec((None, tm, d), tok),
                  pl.BlockSpec((None, 8, d), lambda bi, j: (bi, 0, 0)),
                  pl.BlockSpec((1, d), const),
                  pl.BlockSpec(w1.shape, const, pipeline_mode=pl.Buffered(1)),
                  pl.BlockSpec(w2.shape, const, pipeline_mode=pl.Buffered(1)),
                  pl.BlockSpec((1, d), const)],
        out_specs=pl.BlockSpec((None, tm, d), tok),
        out_shape=jax.ShapeDtypeStruct((b, t, d), F32),
        compiler_params=_cparams("parallel", "parallel"),
        name="ffn",
    )(x, mod, g_pre, w1, w2, g_post)


OFF_A, OFF_Q, OFF_AB, OFF_Z, OFF_C = 0, 3 * A_W, 3 * A_W + 3 * GDN_W, 3 * A_W + 3 * GDN_W + 16, 3 * A_W + 4 * GDN_W + 16
GATE_PERM = [4 * ((n % 8) // 2) + 2 * (n % 2) + n // 8 for n in range(16)]


def _mod_rows(rows, b, d):
    m = jnp.broadcast_to(rows.reshape(-1, 6, d), (b, 6, d))
    return jnp.pad(m, ((0, 0), (0, 2), (0, 0)))


def kernel(x, c, ctx, c_ctx, w_mod, b_mod, g_pre_mix, g_post_mix, g_pre_ffn, g_post_ffn, w_in, conv_a, conv_qkv,
           a_log, dt_bias, g_onorm, ln_c_g, ln_c_b, w_s, b_s, w_o, w_ffn_in, w_ffn_out):
    b, t, d = x.shape
    depth = w_mod.shape[0]
    cc = jnp.concatenate([c, c_ctx[None, :], jnp.zeros((16 - b - 1, d), F32)], axis=0)
    mod = _modulation(cc, w_mod, b_mod)
    s_zero = jnp.zeros((b, 2, HEAD_W, GDN_W), F32)
    for l in range(depth):
        last = l == depth - 1
        modx = _mod_rows(mod[l, :b], b, d)
        modc = _mod_rows(mod[l, b:b + 1], b, d)
        wl = w_in[l].astype(BF16)
        w_gate = wl[:, OFF_AB:OFF_Z][:, jnp.array(GATE_PERM)]
        ws_in = [wl[:, OFF_Q:OFF_AB], wl[:, OFF_Z:OFF_C], wl[:, OFF_A:OFF_Q], wl[:, OFF_C:],
                 jnp.pad(w_gate, ((0, 0), (0, LANES - 16)))]
        gpm, gqm = g_pre_mix[l][None, :], g_post_mix[l][None, :]
        gpf, gqf = g_pre_ffn[l][None, :], g_post_ffn[l][None, :]
        sgu_w = w_s[l].reshape(C_GROUPS * C_CHUNK, C_CHUNK).astype(BF16)
        sgu_b = jnp.repeat(b_s[l].T, C_GD, axis=1)
        lng, lnb = ln_c_g[l][None, :], ln_c_b[l][None, :]
        wo = w_o[l].astype(BF16)
        w1, w2 = w_ffn_in[l].astype(BF16), w_ffn_out[l].astype(BF16)
        gon = g_onorm[l][None, :]

        def mixer_inputs(stream, m):
            pq, pz, pa, pc, pab = _in_proj(stream, m, gpm, ws_in)
            return pq, pz, pa, pc, _gdn_prep(pq, pab, conv_qkv[l], a_log[l], dt_bias[l])

        def finish(stream, m, of, ob, pz, pa, pc, grid_conv):
            ya, yc = _mixers_ac(pa, pc, conv_a[l], lng, lnb, sgu_w, sgu_b, grid_conv=grid_conv)
            stream = _out_proj(stream, of, ob, pz, ya, yc, m, gon, wo, gqm)
            return _ffn(stream, m, gpf, w1, w2, gqf)

        pq, pz, pa, pc, ops = mixer_inputs(ctx, modc)
        of, ob, s_ctx = _gdn_scan(*ops, s_zero)
        if not last:
            ctx = finish(ctx, modc, of, ob, pz, pa, pc, False)
        pq, pz, pa, pc, ops = mixer_inputs(x, modx)
        of, ob, _ = _gdn_scan(*ops, s_ctx)
        x = finish(x, modx, of, ob, pz, pa, pc, True)
    return x
```

```python
import functools

import jax
import jax.numpy as jnp
from jax import lax
from jax.experimental import pallas as pl
from jax.experimental.pallas import tpu as pltpu

F32 = jnp.float32
BF16 = jnp.bfloat16
HIGHEST = lax.Precision.HIGHEST

EPS = 1e-6
GRID_W = 64
N_HEADS = 4
HEAD_W = 128
GDN_W = N_HEADS * HEAD_W
GDN_CHUNK = 64
A_W = 256
A_HORIZ = 128
C_W = 256
C_GROUPS = 4
C_GD = C_W // C_GROUPS
C_CHUNK = 128
LANES = 128
VMEM_LIMIT = 56 * 1024 * 1024


def _cparams(*sem):
    return pltpu.CompilerParams(dimension_semantics=sem, vmem_limit_bytes=VMEM_LIMIT)


def _hdot(a, b):
    return jnp.dot(a, b, precision=HIGHEST, preferred_element_type=F32)


def _rms(x, g):
    return x * lax.rsqrt(jnp.mean(x * x, axis=-1, keepdims=True) + EPS) * g


def _silu(x):
    return x * jax.nn.sigmoid(x)


def _gelu(x):
    return 0.5 * x * (1.0 + lax.erf(x * 0.7071067811865476))


def _softplus(x):
    return jnp.maximum(x, 0.0) + jnp.log1p(jnp.exp(-jnp.abs(x)))


def _mod_kernel(c_ref, w_ref, b_ref, o_ref):
    o_ref[...] = _hdot(_silu(c_ref[...]), w_ref[...]) + b_ref[...]


def _modulation(cc, w_mod, b_mod):
    depth, d, six_d = w_mod.shape
    r = cc.shape[0]
    return pl.pallas_call(
        _mod_kernel,
        grid=(depth, six_d // d),
        in_specs=[pl.BlockSpec((r, d), lambda l, j: (0, 0)),
                  pl.BlockSpec((None, d, d), lambda l, j: (l, 0, j)),
                  pl.BlockSpec((None, 1, d), lambda l, j: (l, 0, j))],
        out_specs=pl.BlockSpec((None, r, d), lambda l, j: (l, 0, j)),
        out_shape=jax.ShapeDtypeStruct((depth, r, six_d), F32),
        compiler_params=_cparams("parallel", "parallel"),
        name="modulation",
    )(cc, w_mod, b_mod.reshape(depth, 1, six_d))


def _in_kernel(x_ref, mod_ref, g_ref, wq, wz, wa, wc, wab, oq, oz, oa, oc, oab):
    h = _rms(x_ref[...], g_ref[...]) * (1.0 + mod_ref[1:2, :]) + mod_ref[0:1, :]
    hb = h.astype(BF16)
    for w, o in ((wq, oq), (wz, oz), (wa, oa), (wc, oc)):
        o[...] = jnp.dot(hb, w[...], preferred_element_type=F32).astype(BF16)
    oab[...] = jnp.dot(hb, wab[...], preferred_element_type=F32)


def _row_tile(t):
    return min(t, 512)


def _in_proj(x, mod, g_pre, ws):
    b, t, d = x.shape
    tm = _row_tile(t)
    widths = [w.shape[1] for w in ws]
    const = lambda bi, j: (0, 0)
    tok = lambda bi, j: (bi, j, 0)
    return pl.pallas_call(
        _in_kernel,
        grid=(b, t // tm),
        in_specs=[pl.BlockSpec((None, tm, d), tok),
                  pl.BlockSpec((None, 8, d), lambda bi, j: (bi, 0, 0)),
                  pl.BlockSpec((1, d), const)]
                 + [pl.BlockSpec((d, n), const) for n in widths],
        out_specs=[pl.BlockSpec((None, tm, n), tok) for n in widths],
        out_shape=[jax.ShapeDtypeStruct((b, t, n), BF16) for n in widths[:-1]]
                  + [jax.ShapeDtypeStruct((b, t, widths[-1]), F32)],
        compiler_params=_cparams("parallel", "parallel"),
        name="in_proj",
    )(x, mod, g_pre, *ws)


def _sgu(pc, lng, lnb, ws, bsm):
    tt = pc.shape[0]
    u = _gelu(pc[:, :C_W])
    v = _gelu(pc[:, C_W:])
    mu = jnp.mean(v, axis=-1, keepdims=True)
    vc = v - mu
    var = jnp.mean(vc * vc, axis=-1, keepdims=True)
    vb = (vc * lax.rsqrt(var + EPS) * lng + lnb).astype(BF16)
    grp = lax.broadcasted_iota(jnp.int32, (C_CHUNK, C_W), 1) // C_GD
    outs = []
    for n in range(tt // C_CHUNK):
        r = jnp.dot(ws, vb[n * C_CHUNK:(n + 1) * C_CHUNK, :], preferred_element_type=F32)
        mixed = r[3 * C_CHUNK:]
        for g in (2, 1, 0):
            mixed = jnp.where(grp == g, r[g * C_CHUNK:(g + 1) * C_CHUNK], mixed)
        outs.append(mixed + bsm)
    return u * jnp.concatenate(outs, axis=0)


def _mix_grid_kernel(pa_ref, prev_ref, next_ref, pc_ref, cw_ref, lng_ref, lnb_ref, ws_ref, bsm_ref,
                     ya_ref, yc_ref, *, nj):
    j = pl.program_id(1)
    pa = pa_ref[...].astype(F32)
    tt = pa.shape[0]
    cw = cw_ref[...]
    gate = pa[:, :A_W]
    m = pa[:, A_W:2 * A_W] * pa[:, 2 * A_W:]
    col = lax.broadcasted_iota(jnp.int32, (tt, A_HORIZ), 0) % GRID_W
    mh = m[:, :A_HORIZ]
    left = jnp.where(col == 0, 0.0, pltpu.roll(mh, 1, 0))
    right = jnp.where(col == GRID_W - 1, 0.0, pltpu.roll(mh, tt - 1, 0))
    yh = cw[0:1, :A_HORIZ] * left + cw[1:2, :A_HORIZ] * mh + cw[2:3, :A_HORIZ] * right
    mv = m[:, A_HORIZ:]
    pv = prev_ref[...].astype(F32)
    nv = next_ref[...].astype(F32)
    has_prev = jnp.where(j > 0, 1.0, 0.0)
    has_next = jnp.where(j < nj - 1, 1.0, 0.0)
    pm = pv[:, A_W + A_HORIZ:2 * A_W] * pv[:, 2 * A_W + A_HORIZ:] * has_prev
    nm = nv[:, A_W + A_HORIZ:2 * A_W] * nv[:, 2 * A_W + A_HORIZ:] * has_next
    up = jnp.concatenate([pm, mv[:tt - GRID_W]], axis=0)
    down = jnp.concatenate([mv[GRID_W:], nm], axis=0)
    yv = cw[0:1, A_HORIZ:] * up + cw[1:2, A_HORIZ:] * mv + cw[2:3, A_HORIZ:] * down
    ya_ref[...] = (gate * jnp.concatenate([yh, yv], axis=-1)).astype(BF16)
    yc_ref[...] = _sgu(pc_ref[...].astype(F32), lng_ref[...], lnb_ref[...], ws_ref[...],
                       bsm_ref[...]).astype(BF16)


def _mix_seq_kernel(pa_ref, pc_ref, cw_ref, lng_ref, lnb_ref, ws_ref, bsm_ref, ya_ref, yc_ref):
    pa = pa_ref[...].astype(F32)
    tt = pa.shape[0]
    cw = cw_ref[...]
    m = pa[:, A_W:2 * A_W] * pa[:, 2 * A_W:]
    row = lax.broadcasted_iota(jnp.int32, (tt, A_W), 0)
    left = jnp.where(row == 0, 0.0, pltpu.roll(m, 1, 0))
    right = jnp.where(row == tt - 1, 0.0, pltpu.roll(m, tt - 1, 0))
    y = cw[0:1] * left + cw[1:2] * m + cw[2:3] * right
    ya_ref[...] = (pa[:, :A_W] * y).astype(BF16)
    yc_ref[...] = _sgu(pc_ref[...].astype(F32), lng_ref[...], lnb_ref[...], ws_ref[...],
                       bsm_ref[...]).astype(BF16)


def _mixers_ac(pa, pc, conv_a, ln_g, ln_b, ws, bsm, *, grid_conv):
    b, t, _ = pa.shape
    const = lambda bi, j: (0, 0)
    tok = lambda bi, j: (bi, j, 0)
    small = [pl.BlockSpec(conv_a.shape, const), pl.BlockSpec(ln_g.shape, const),
             pl.BlockSpec(ln_b.shape, const), pl.BlockSpec(ws.shape, const), pl.BlockSpec(bsm.shape, const)]
    outs = dict(out_shape=[jax.ShapeDtypeStruct((b, t, A_W), BF16), jax.ShapeDtypeStruct((b, t, C_W), BF16)],
                compiler_params=_cparams("parallel", "parallel"))
    if not grid_conv:
        return pl.pallas_call(
            _mix_seq_kernel, grid=(b, 1),
            in_specs=[pl.BlockSpec((None, t, 3 * A_W), tok), pl.BlockSpec((None, t, 2 * C_W), tok)] + small,
            out_specs=[pl.BlockSpec((None, t, A_W), tok), pl.BlockSpec((None, t, C_W), tok)],
            name="mixers_ac_seq", **outs,
        )(pa, pc, conv_a, ln_g, ln_b, ws, bsm)
    tt = _row_tile(t)
    nj = t // tt
    rows = tt // GRID_W
    last = t // GRID_W - 1
    return pl.pallas_call(
        functools.partial(_mix_grid_kernel, nj=nj), grid=(b, nj),
        in_specs=[pl.BlockSpec((None, tt, 3 * A_W), tok),
                  pl.BlockSpec((None, GRID_W, 3 * A_W), lambda bi, j: (bi, jnp.maximum(j * rows - 1, 0), 0)),
                  pl.BlockSpec((None, GRID_W, 3 * A_W), lambda bi, j: (bi, jnp.minimum((j + 1) * rows, last), 0)),
                  pl.BlockSpec((None, tt, 2 * C_W), tok)] + small,
        out_specs=[pl.BlockSpec((None, tt, A_W), tok), pl.BlockSpec((None, tt, C_W), tok)],
        name="mixers_ac_grid", **outs,
    )(pa, pa, pa, pc, conv_a, ln_g, ln_b, ws, bsm)


HALO = 16
PAIRS = N_HEADS // 2
PAIR_W = 2 * HEAD_W


def _block_diag2(y):
    lane = lax.broadcasted_iota(jnp.int32, y.shape, 1)
    return jnp.concatenate([jnp.where(lane < GDN_CHUNK, y, 0.0), jnp.where(lane >= GDN_CHUNK, y, 0.0)],
                           axis=0).astype(BF16)


def _pdot(a, b):
    return jnp.dot(a.astype(BF16), _block_diag2(b), preferred_element_type=F32)


def _unit_tri_inverses(lds, ii, jl):
    xs = [jnp.where(ii == jl, 1.0, 0.0) - jnp.where((ii >> 1) == (jl >> 1), ld, 0.0) for ld in lds]
    for k in range(1, 6):
        mk = ((ii >> (k + 1)) == (jl >> (k + 1))) & ((ii >> k) != (jl >> k))
        ts = [_pdot(jnp.where(mk, ld, 0.0), x) for ld, x in zip(lds, xs)]
        xs = [x - _pdot(x, t) for x, t in zip(xs, ts)]
    return xs


def _gdn_prep_kernel(pq_ref, pqp_ref, pqn_ref, ab_ref, cw_ref, alog_t, dtb_t,
                     l1_ref, l2_ref, u_ref, dec_ref, q_s, k_s, v_s, *, cpt, nt):
    j = pl.program_id(1)
    tt = cpt * GDN_CHUNK
    x = pq_ref[...].astype(F32)
    w3 = 3 * GDN_W
    has_prev = jnp.where(j > 0, 1.0, 0.0)
    has_next = jnp.where(j < nt - 1, 1.0, 0.0)
    prev_row = pqp_ref[HALO - 1:HALO, :].astype(F32) * has_prev
    next_row = pqn_ref[0:1, :].astype(F32) * has_next
    row8 = lax.broadcasted_iota(jnp.int32, (8, w3), 0)
    xm1 = pltpu.roll(x, 1, 0)
    xm1 = jnp.concatenate([jnp.where(row8 == 0, prev_row, xm1[:8]), xm1[8:]], axis=0)
    xp1 = pltpu.roll(x, tt - 1, 0)
    xp1 = jnp.concatenate([xp1[:tt - 8], jnp.where(row8 == 7, next_row, xp1[tt - 8:])], axis=0)
    cw = cw_ref[...]
    s = _silu(cw[0:1] * xm1 + cw[1:2] * x + cw[2:3] * xp1)
    for h in range(N_HEADS):
        lo, hi = h * HEAD_W, (h + 1) * HEAD_W
        qh = s[:, lo:hi]
        q_s[:, lo:hi] = qh * (lax.rsqrt(jnp.sum(qh * qh, axis=-1, keepdims=True) + EPS) * HEAD_W ** -0.5)
        kh = s[:, GDN_W + lo:GDN_W + hi]
        k_s[:, lo:hi] = kh * lax.rsqrt(jnp.sum(kh * kh, axis=-1, keepdims=True) + EPS)
    v_s[...] = s[:, 2 * GDN_W:]

    cc = GDN_CHUNK
    ii = lax.broadcasted_iota(jnp.int32, (cc, LANES), 0)
    lane = lax.broadcasted_iota(jnp.int32, (cc, LANES), 1)
    jl = lane & (cc - 1)
    first_head = lane < cc
    lane_row = lax.broadcasted_iota(jnp.int32, (1, LANES), 1)
    zero_k = jnp.zeros((cc, HEAD_W), BF16)

    gts = []
    for c in range(cpt):
        abn = ab_ref[c * cc:(c + 1) * cc, :]
        gts.append(jnp.concatenate([abn, pltpu.roll(abn, LANES - 8, 1)], axis=0).T[0:8, :])
    gt = jnp.concatenate(gts, axis=0)
    beta_all = jax.nn.sigmoid(gt)
    g_all = -jnp.exp(alog_t[...]) * _softplus(gt + dtb_t[...])
    tok = lax.broadcasted_iota(jnp.int32, gt.shape, 1) & (cc - 1)
    pre = suf = g_all
    sh = 1
    while sh < cc:
        pre = pre + jnp.where(tok >= sh, pltpu.roll(pre, sh, 1), 0.0)
        suf = suf + jnp.where(tok < cc - sh, pltpu.roll(suf, LANES - sh, 1), 0.0)
        sh *= 2
    gam_all = jnp.where((lax.broadcasted_iota(jnp.int32, gt.shape, 0) & 7) < 6, pre, suf)
    gcols = jnp.concatenate([gam_all, jnp.zeros((LANES - 8 * cpt, LANES), F32)], axis=0).T

    def setup(c):
        kc = k_s[c * cc:(c + 1) * cc, :]
        qc = q_s[c * cc:(c + 1) * cc, :]
        vb = v_s[c * cc:(c + 1) * cc, :].astype(BF16)
        kb = kc.astype(BF16)
        pairs = []
        for p in range(PAIRS):
            lo = p * PAIR_W
            k0, k1 = kb[:, lo:lo + HEAD_W], kb[:, lo + HEAD_W:lo + PAIR_W]
            kbd = jnp.concatenate([jnp.concatenate([k0, zero_k], axis=1),
                                   jnp.concatenate([zero_k, k1], axis=1)], axis=0)
            vbd = jnp.concatenate([jnp.concatenate([vb[:, lo:lo + HEAD_W], zero_k], axis=1),
                                   jnp.concatenate([zero_k, vb[:, lo + HEAD_W:lo + PAIR_W]], axis=1)], axis=0)
            kq = lax.dot_general(jnp.concatenate([kb[:, lo:lo + PAIR_W], qc[:, lo:lo + PAIR_W].astype(BF16)], axis=0),
                                 kbd, (((1,), (1,)), ((), ())), preferred_element_type=F32)
            kt = jnp.concatenate([kc[:, lo:lo + HEAD_W], kc[:, lo + HEAD_W:lo + PAIR_W]], axis=0).T
            pairs.append((kbd, vbd, kq[:cc], kq[cc:], kt, qc[:, lo:lo + HEAD_W], qc[:, lo + HEAD_W:lo + PAIR_W]))
        probs = []
        for d in range(2):
            incl = (ii >= jl) if d == 0 else (ii <= jl)
            strict = (ii > jl) if d == 0 else (ii < jl)
            for p in range(PAIRS):
                kbd, vbd, kkt, qk, kt, q0, q1 = pairs[p]
                rb = 8 * c + 2 * d + p
                rg = rb + 4
                grow = gam_all[rg:rg + 1, :]
                brow = beta_all[rb:rb + 1, :]
                gc0 = jnp.broadcast_to(gcols[:cc, rg:rg + 1], (cc, LANES))
                gc1 = jnp.broadcast_to(gcols[cc:, rg:rg + 1], (cc, LANES))
                dm = jnp.exp(jnp.where(incl, jnp.where(first_head, gc0, gc1) - grow, -1e30))
                ld = jnp.where(strict, kkt * dm, 0.0) * brow
                top = jnp.where(incl, qk * dm, 0.0) * brow
                probs.append(dict(c=c, d=d, p=p, ld=ld, top=top, grow=grow, brow=brow, gc0=gc0, gc1=gc1,
                                  kbd=kbd, vbd=vbd, kt=kt, q0=q0, q1=q1))
        return probs

    def emit(pr, x):
        c, d, p, grow, brow = pr["c"], pr["d"], pr["p"], pr["grow"], pr["brow"]
        wk = jnp.dot((x * jnp.exp(grow)).astype(BF16), pr["kbd"], preferred_element_type=F32)
        u_ref[d, c, :, p * PAIR_W:(p + 1) * PAIR_W] = jnp.dot(x.astype(BF16), pr["vbd"], preferred_element_type=F32)
        l1_ref[d, c, 2 * p] = jnp.concatenate([wk[:, :HEAD_W], pr["q0"] * jnp.exp(pr["gc0"])], axis=0).astype(BF16)
        l1_ref[d, c, 2 * p + 1] = jnp.concatenate([wk[:, HEAD_W:], pr["q1"] * jnp.exp(pr["gc1"])], axis=0).astype(BF16)
        last = cc - 1 if d == 0 else 0
        gl0 = grow[:, last:last + 1]
        gl1 = grow[:, cc + last:cc + last + 1]
        tail = jnp.exp(jnp.where(lane_row < cc, gl0, gl1) - grow) * brow
        l2_ref[d, c, p] = jnp.concatenate([pr["top"], pr["kt"] * tail], axis=0).astype(BF16)
        dec_ref[d, c, :, p * PAIR_W:(p + 1) * PAIR_W] = jnp.concatenate(
            [jnp.broadcast_to(jnp.exp(gl0), (1, HEAD_W)), jnp.broadcast_to(jnp.exp(gl1), (1, HEAD_W))], axis=1)

    probs = []
    for c in range(cpt):
        probs += setup(c)
    xs = _unit_tri_inverses([pr["ld"] for pr in probs], ii, jl)
    for pr, xi in zip(probs, xs):
        emit(pr, xi)


def _gdn_tile_chunks(t):
    return min(t // GDN_CHUNK, 8)


def _gdn_prep(pq, pab, conv_qkv, a_log, dt_bias):
    b, t, w3 = pq.shape
    nc = t // GDN_CHUNK
    cpt = _gdn_tile_chunks(t)
    tt = cpt * GDN_CHUNK
    nt = t // tt
    hb = tt // HALO
    def gate_rows(a):
        rows = jnp.repeat(a.reshape(2 * PAIRS, 2), GDN_CHUNK, axis=1)
        return jnp.tile(jnp.concatenate([jnp.zeros_like(rows), rows], axis=0), (cpt, 1))

    alog_t, dtb_t = gate_rows(a_log), gate_rows(dt_bias)
    const = lambda bi, j: (0, 0)
    tok = lambda bi, j: (bi, j, 0)
    return pl.pallas_call(
        functools.partial(_gdn_prep_kernel, cpt=cpt, nt=nt),
        grid=(b, nt),
        in_specs=[pl.BlockSpec((None, tt, w3), tok),
                  pl.BlockSpec((None, HALO, w3), lambda bi, j: (bi, jnp.maximum(j * hb - 1, 0), 0)),
                  pl.BlockSpec((None, HALO, w3), lambda bi, j: (bi, jnp.minimum((j + 1) * hb, t // HALO - 1), 0)),
                  pl.BlockSpec((None, tt, LANES), tok),
                  pl.BlockSpec((3, w3), const),
                  pl.BlockSpec((8 * cpt, LANES), const), pl.BlockSpec((8 * cpt, LANES), const)],
        out_specs=[pl.BlockSpec((None, 2, cpt, N_HEADS, 2 * GDN_CHUNK, HEAD_W), lambda bi, j: (bi, 0, j, 0, 0, 0)),
                   pl.BlockSpec((None, 2, cpt, PAIRS, 3 * GDN_CHUNK, HEAD_W), lambda bi, j: (bi, 0, j, 0, 0, 0)),
                   pl.BlockSpec((None, 2, cpt, GDN_CHUNK, GDN_W), lambda bi, j: (bi, 0, j, 0, 0)),
                   pl.BlockSpec((None, 2, cpt, 1, GDN_W), lambda bi, j: (bi, 0, j, 0, 0))],
        out_shape=[jax.ShapeDtypeStruct((b, 2, nc, N_HEADS, 2 * GDN_CHUNK, HEAD_W), BF16),
                   jax.ShapeDtypeStruct((b, 2, nc, PAIRS, 3 * GDN_CHUNK, HEAD_W), BF16),
                   jax.ShapeDtypeStruct((b, 2, nc, GDN_CHUNK, GDN_W), F32),
                   jax.ShapeDtypeStruct((b, 2, nc, 1, GDN_W), F32)],
        scratch_shapes=[pltpu.VMEM((tt, GDN_W), F32)] * 3,
        compiler_params=_cparams("parallel", "parallel"),
        name="gdn_prep",
    )(pq, pq, pq, pab, conv_qkv, alog_t, dtb_t)


def _gdn_scan_kernel(l1f, l1b, l2f, l2b, uf, ub, decf, decb, s0_ref, of_ref, ob_ref, sfin_ref, s_ref,
                     *, cpg, nj):
    j = pl.program_id(1)

    @pl.when(j == 0)
    def _():
        s_ref[...] = s0_ref[...]

    zero = jnp.zeros((GDN_CHUNK, HEAD_W), BF16)

    def step(i, carry):
        dirs = ((l1f, l2f, uf, decf, of_ref, i), (l1b, l2b, ub, decb, ob_ref, cpg - 1 - i))
        for d, (l1, l2, u, dec, o_ref, c) in enumerate(dirs):
            vts, qss = [], []
            for h in range(N_HEADS):
                sh = s_ref[d, :, h * HEAD_W:(h + 1) * HEAD_W].astype(BF16)
                r1 = jnp.dot(l1[c, h], sh, preferred_element_type=F32)
                vts.append((u[c, :, h * HEAD_W:(h + 1) * HEAD_W] - r1[:GDN_CHUNK]).astype(BF16))
                qss.append(r1[GDN_CHUNK:])
            intra = []
            for p in range(N_HEADS // 2):
                rhs = jnp.concatenate([jnp.concatenate([vts[2 * p], zero], axis=-1),
                                       jnp.concatenate([zero, vts[2 * p + 1]], axis=-1)], axis=0)
                r2 = jnp.dot(l2[c, p], rhs, preferred_element_type=F32)
                intra.append(r2[:GDN_CHUNK])
                lo, hi = 2 * p * HEAD_W, 2 * (p + 1) * HEAD_W
                s_ref[d, :, lo:hi] = s_ref[d, :, lo:hi] * dec[c][:, lo:hi] + r2[GDN_CHUNK:]
            o = jnp.concatenate(qss, axis=-1) + jnp.concatenate(intra, axis=-1)
            o_ref[pl.ds(pl.multiple_of(c * GDN_CHUNK, GDN_CHUNK), GDN_CHUNK), :] = o.astype(o_ref.dtype)
        return carry

    lax.fori_loop(0, cpg, step, 0)

    @pl.when(j == nj - 1)
    def _():
        sfin_ref[...] = s_ref[...]


def _gdn_scan(l1, l2, u, dec, s0):
    b, _, nc = l1.shape[:3]
    cpg = min(nc, 8)
    nj = nc // cpg
    t = nc * GDN_CHUNK
    fwd5 = lambda bi, j: (bi, 0, j, 0, 0, 0)
    bwd5 = lambda bi, j: (bi, 1, nj - 1 - j, 0, 0, 0)
    fwd4 = lambda bi, j: (bi, 0, j, 0, 0)
    bwd4 = lambda bi, j: (bi, 1, nj - 1 - j, 0, 0)
    b1 = (None, None, cpg, N_HEADS, 2 * GDN_CHUNK, HEAD_W)
    b2 = (None, None, cpg, N_HEADS // 2, 3 * GDN_CHUNK, HEAD_W)
    bu = (None, None, cpg, GDN_CHUNK, GDN_W)
    bd = (None, None, cpg, 1, GDN_W)
    st = pl.BlockSpec((None, 2, HEAD_W, GDN_W), lambda bi, j: (bi, 0, 0, 0))
    return pl.pallas_call(
        functools.partial(_gdn_scan_kernel, cpg=cpg, nj=nj),
        grid=(b, nj),
        in_specs=[pl.BlockSpec(b1, fwd5), pl.BlockSpec(b1, bwd5), pl.BlockSpec(b2, fwd5), pl.BlockSpec(b2, bwd5),
                  pl.BlockSpec(bu, fwd4), pl.BlockSpec(bu, bwd4), pl.BlockSpec(bd, fwd4), pl.BlockSpec(bd, bwd4), st],
        out_specs=[pl.BlockSpec((None, cpg * GDN_CHUNK, GDN_W), lambda bi, j: (bi, j, 0)),
                   pl.BlockSpec((None, cpg * GDN_CHUNK, GDN_W), lambda bi, j: (bi, nj - 1 - j, 0)), st],
        out_shape=[jax.ShapeDtypeStruct((b, t, GDN_W), BF16), jax.ShapeDtypeStruct((b, t, GDN_W), BF16),
                   jax.ShapeDtypeStruct((b, 2, HEAD_W, GDN_W), F32)],
        scratch_shapes=[pltpu.VMEM((2, HEAD_W, GDN_W), F32)],
        compiler_params=_cparams("parallel", "arbitrary"),
        name="gdn_scan",
    )(l1, l1, l2, l2, u, u, dec, dec, s0)


def _out_kernel(x_ref, of_ref, ob_ref, z_ref, ya_ref, yc_ref, mod_ref, gon_ref, wo_ref, gpost_ref, o_ref):
    o = of_ref[...].astype(F32) + ob_ref[...].astype(F32)
    z = z_ref[...].astype(F32)
    ybs = []
    for h in range(N_HEADS):
        oh = o[:, h * HEAD_W:(h + 1) * HEAD_W]
        oh = oh * lax.rsqrt(jnp.mean(oh * oh, axis=-1, keepdims=True) + EPS) * gon_ref[...]
        ybs.append((oh * _silu(z[:, h * HEAD_W:(h + 1) * HEAD_W])).astype(BF16))
    y = jnp.concatenate([ya_ref[...]] + ybs + [yc_ref[...]], axis=-1)
    r = jnp.dot(y, wo_ref[...], preferred_element_type=F32)
    o_ref[...] = x_ref[...] + mod_ref[2:3, :] * _rms(r, gpost_ref[...])


def _out_proj(x, of, ob, z, ya, yc, mod, g_onorm, w_o, g_post):
    b, t, d = x.shape
    tm = _row_tile(t)
    const = lambda bi, j: (0, 0)
    tok = lambda bi, j: (bi, j, 0)
    return pl.pallas_call(
        _out_kernel,
        grid=(b, t // tm),
        in_specs=[pl.BlockSpec((None, tm, d), tok),
                  pl.BlockSpec((None, tm, GDN_W), tok), pl.BlockSpec((None, tm, GDN_W), tok),
                  pl.BlockSpec((None, tm, GDN_W), tok),
                  pl.BlockSpec((None, tm, A_W), tok), pl.BlockSpec((None, tm, C_W), tok),
                  pl.BlockSpec((None, 8, d), lambda bi, j: (bi, 0, 0)),
                  pl.BlockSpec((1, HEAD_W), const), pl.BlockSpec(w_o.shape, const), pl.BlockSpec((1, d), const)],
        out_specs=pl.BlockSpec((None, tm, d), tok),
        out_shape=jax.ShapeDtypeStruct((b, t, d), F32),
        compiler_params=_cparams("parallel", "parallel"),
        name="out_proj",
    )(x, of, ob, z, ya, yc, mod, g_onorm, w_o, g_post)


FFN_TILE = 1408


def _ffn_kernel(x_ref, mod_ref, gpre_ref, w1_ref, w2_ref, gpost_ref, o_ref, *, hidden):
    x = x_ref[...]
    hb = (_rms(x, gpre_ref[...]) * (1.0 + mod_ref[4:5, :]) + mod_ref[3:4, :]).astype(BF16)
    acc = None
    for c0 in range(0, hidden, FFN_TILE):
        g = jnp.dot(hb, w1_ref[:, c0:c0 + FFN_TILE], preferred_element_type=F32)
        u = jnp.dot(hb, w1_ref[:, hidden + c0:hidden + c0 + FFN_TILE], preferred_element_type=F32)
        part = jnp.dot((_silu(g) * u).astype(BF16), w2_ref[c0:c0 + FFN_TILE, :], preferred_element_type=F32)
        acc = part if acc is None else acc + part
    o_ref[...] = x + mod_ref[5:6, :] * _rms(acc, gpost_ref[...])


def _ffn(x, mod, g_pre, w1, w2, g_post):
    b, t, d = x.shape
    hidden = w2.shape[0]
    tm = _row_tile(t)
    const = lambda bi, j: (0, 0)
    tok = lambda bi, j: (bi, j, 0)
    return pl.pallas_call(
        functools.partial(_ffn_kernel, hidden=hidden),
        grid=(b, t // tm),
        in_specs=[pl.BlockSpec((None, tm, d), tok),
                  pl.BlockSpec((None, 8, d), lambda bi, j: (bi, 0, 0)),
                  pl.BlockSpec((1, d), const),
                  pl.BlockSpec(w1.shape, const, pipeline_mode=pl.Buffered(1)),
                  pl.BlockSpec(w2.shape, const, pipeline_mode=pl.Buffered(1)),
                  pl.BlockSpec((1, d), const)],
        out_specs=pl.BlockSpec((None, tm, d), tok),
        out_shape=jax.ShapeDtypeStruct((b, t, d), F32),
        compiler_params=_cparams("parallel", "parallel"),
        name="ffn",
    )(x, mod, g_pre, w1, w2, g_post)


OFF_A, OFF_Q, OFF_AB, OFF_Z, OFF_C = 0, 3 * A_W, 3 * A_W + 3 * GDN_W, 3 * A_W + 3 * GDN_W + 16, 3 * A_W + 4 * GDN_W + 16
GATE_PERM = [4 * ((n % 8) // 2) + 2 * (n % 2) + n // 8 for n in range(16)]


def _mod_rows(rows, b, d):
    m = jnp.broadcast_to(rows.reshape(-1, 6, d), (b, 6, d))
    return jnp.pad(m, ((0, 0), (0, 2), (0, 0)))


def kernel(x, c, ctx, c_ctx, w_mod, b_mod, g_pre_mix, g_post_mix, g_pre_ffn, g_post_ffn, w_in, conv_a, conv_qkv,
           a_log, dt_bias, g_onorm, ln_c_g, ln_c_b, w_s, b_s, w_o, w_ffn_in, w_ffn_out):
    b, t, d = x.shape
    depth = w_mod.shape[0]
    cc = jnp.concatenate([c, c_ctx[None, :], jnp.zeros((16 - b - 1, d), F32)], axis=0)
    mod = _modulation(cc, w_mod, b_mod)
    s_zero = jnp.zeros((b, 2, HEAD_W, GDN_W), F32)
    for l in range(depth):
        last = l == depth - 1
        modx = _mod_rows(mod[l, :b], b, d)
        modc = _mod_rows(mod[l, b:b + 1], b, d)
        wl = w_in[l].astype(BF16)
        w_gate = wl[:, OFF_AB:OFF_Z][:, jnp.array(GATE_PERM)]
        ws_in = [wl[:, OFF_Q:OFF_AB], wl[:, OFF_Z:OFF_C], wl[:, OFF_A:OFF_Q], wl[:, OFF_C:],
                 jnp.pad(w_gate, ((0, 0), (0, LANES - 16)))]
        gpm, gqm = g_pre_mix[l][None, :], g_post_mix[l][None, :]
        gpf, gqf = g_pre_ffn[l][None, :], g_post_ffn[l][None, :]
        sgu_w = w_s[l].reshape(C_GROUPS * C_CHUNK, C_CHUNK).astype(BF16)
        sgu_b = jnp.repeat(b_s[l].T, C_GD, axis=1)
        lng, lnb = ln_c_g[l][None, :], ln_c_b[l][None, :]
        wo = w_o[l].astype(BF16)
        w1, w2 = w_ffn_in[l].astype(BF16), w_ffn_out[l].astype(BF16)
        gon = g_onorm[l][None, :]

        def mixer_inputs(stream, m):
            pq, pz, pa, pc, pab = _in_proj(stream, m, gpm, ws_in)
            return pq, pz, pa, pc, _gdn_prep(pq, pab, conv_qkv[l], a_log[l], dt_bias[l])

        def finish(stream, m, of, ob, pz, pa, pc, grid_conv):
            ya, yc = _mixers_ac(pa, pc, conv_a[l], lng, lnb, sgu_w, sgu_b, grid_conv=grid_conv)
            stream = _out_proj(stream, of, ob, pz, ya, yc, m, gon, wo, gqm)
            return _ffn(stream, m, gpf, w1, w2, gqf)

        pq, pz, pa, pc, ops = mixer_inputs(ctx, modc)
        of, ob, s_ctx = _gdn_scan(*ops, s_zero)
        if not last:
            ctx = finish(ctx, modc, of, ob, pz, pa, pc, False)
        pq, pz, pa, pc, ops = mixer_inputs(x, modx)
        of, ob, _ = _gdn_scan(*ops, s_ctx)
        x = finish(x, modx, of, ob, pz, pa, pc, True)
    return x
```

```python
import functools

import jax
import jax.numpy as jnp
from jax import lax
from jax.experimental import pallas as pl
from jax.experimental.pallas import tpu as pltpu

F32 = jnp.float32
BF16 = jnp.bfloat16
HIGHEST = lax.Precision.HIGHEST

EPS = 1e-6
GRID_W = 64
N_HEADS = 4
HEAD_W = 128
GDN_W = N_HEADS * HEAD_W
GDN_CHUNK = 64
A_W = 256
A_HORIZ = 128
C_W = 256
C_GROUPS = 4
C_GD = C_W // C_GROUPS
C_CHUNK = 128
LANES = 128
VMEM_LIMIT = 56 * 1024 * 1024


def _cparams(*sem):
    return pltpu.CompilerParams(dimension_semantics=sem, vmem_limit_bytes=VMEM_LIMIT)


def _hdot(a, b):
    return jnp.dot(a, b, precision=HIGHEST, preferred_element_type=F32)


def _rms(x, g):
    return x * lax.rsqrt(jnp.mean(x * x, axis=-1, keepdims=True) + EPS) * g


def _silu(x):
    return x * jax.nn.sigmoid(x)


def _gelu(x):
    return 0.5 * x * (1.0 + lax.erf(x * 0.7071067811865476))


def _softplus(x):
    return jnp.maximum(x, 0.0) + jnp.log1p(jnp.exp(-jnp.abs(x)))


def _mod_kernel(c_ref, w_ref, b_ref, o_ref):
    o_ref[...] = _hdot(_silu(c_ref[...]), w_ref[...]) + b_ref[...]


def _modulation(cc, w_mod, b_mod):
    depth, d, six_d = w_mod.shape
    r = cc.shape[0]
    return pl.pallas_call(
        _mod_kernel,
        grid=(depth, six_d // d),
        in_specs=[pl.BlockSpec((r, d), lambda l, j: (0, 0)),
                  pl.BlockSpec((None, d, d), lambda l, j: (l, 0, j)),
                  pl.BlockSpec((None, 1, d), lambda l, j: (l, 0, j))],
        out_specs=pl.BlockSpec((None, r, d), lambda l, j: (l, 0, j)),
        out_shape=jax.ShapeDtypeStruct((depth, r, six_d), F32),
        compiler_params=_cparams("parallel", "parallel"),
        name="modulation",
    )(cc, w_mod, b_mod.reshape(depth, 1, six_d))


def _in_kernel(x_ref, mod_ref, g_ref, wq, wz, wa, wc, wab, oq, oz, oa, oc, oab):
    h = _rms(x_ref[...], g_ref[...]) * (1.0 + mod_ref[1:2, :]) + mod_ref[0:1, :]
    hb = h.astype(BF16)
    for w, o in ((wq, oq), (wz, oz), (wa, oa), (wc, oc)):
        o[...] = jnp.dot(hb, w[...], preferred_element_type=F32).astype(BF16)
    oab[...] = jnp.dot(hb, wab[...], preferred_element_type=F32)


def _row_tile(t, rows=512):
    return min(t, rows)


def _in_proj(x, mod, g_pre, ws):
    b, t, d = x.shape
    tm = _row_tile(t, 1024)
    widths = [w.shape[1] for w in ws]
    const = lambda bi, j: (0, 0)
    tok = lambda bi, j: (bi, j, 0)
    return pl.pallas_call(
        _in_kernel,
        grid=(b, t // tm),
        in_specs=[pl.BlockSpec((None, tm, d), tok),
                  pl.BlockSpec((None, 8, d), lambda bi, j: (bi, 0, 0)),
                  pl.BlockSpec((1, d), const)]
                 + [pl.BlockSpec((d, n), const) for n in widths],
        out_specs=[pl.BlockSpec((None, tm, n), tok) for n in widths],
        out_shape=[jax.ShapeDtypeStruct((b, t, n), BF16) for n in widths[:-1]]
                  + [jax.ShapeDtypeStruct((b, t, widths[-1]), F32)],
        compiler_params=_cparams("parallel", "parallel"),
        name="in_proj",
    )(x, mod, g_pre, *ws)


def _sgu(pc, lng, lnb, ws, bsm):
    tt = pc.shape[0]
    u = _gelu(pc[:, :C_W])
    v = _gelu(pc[:, C_W:])
    mu = jnp.mean(v, axis=-1, keepdims=True)
    vc = v - mu
    var = jnp.mean(vc * vc, axis=-1, keepdims=True)
    vb = (vc * lax.rsqrt(var + EPS) * lng + lnb).astype(BF16)
    grp = lax.broadcasted_iota(jnp.int32, (C_CHUNK, C_W), 1) // C_GD
    outs = []
    for n in range(tt // C_CHUNK):
        r = jnp.dot(ws, vb[n * C_CHUNK:(n + 1) * C_CHUNK, :], preferred_element_type=F32)
        mixed = r[3 * C_CHUNK:]
        for g in (2, 1, 0):
            mixed = jnp.where(grp == g, r[g * C_CHUNK:(g + 1) * C_CHUNK], mixed)
        outs.append(mixed + bsm)
    return u * jnp.concatenate(outs, axis=0)


def _mix_grid_kernel(pa_ref, prev_ref, next_ref, pc_ref, cw_ref, lng_ref, lnb_ref, ws_ref, bsm_ref,
                     ya_ref, yc_ref, *, nj):
    j = pl.program_id(1)
    pa = pa_ref[...].astype(F32)
    tt = pa.shape[0]
    cw = cw_ref[...]
    gate = pa[:, :A_W]
    m = pa[:, A_W:2 * A_W] * pa[:, 2 * A_W:]
    col = lax.broadcasted_iota(jnp.int32, (tt, A_HORIZ), 0) % GRID_W
    mh = m[:, :A_HORIZ]
    left = jnp.where(col == 0, 0.0, pltpu.roll(mh, 1, 0))
    right = jnp.where(col == GRID_W - 1, 0.0, pltpu.roll(mh, tt - 1, 0))
    yh = cw[0:1, :A_HORIZ] * left + cw[1:2, :A_HORIZ] * mh + cw[2:3, :A_HORIZ] * right
    mv = m[:, A_HORIZ:]
    pv = prev_ref[...].astype(F32)
    nv = next_ref[...].astype(F32)
    has_prev = jnp.where(j > 0, 1.0, 0.0)
    has_next = jnp.where(j < nj - 1, 1.0, 0.0)
    pm = pv[:, A_W + A_HORIZ:2 * A_W] * pv[:, 2 * A_W + A_HORIZ:] * has_prev
    nm = nv[:, A_W + A_HORIZ:2 * A_W] * nv[:, 2 * A_W + A_HORIZ:] * has_next
    up = jnp.concatenate([pm, mv[:tt - GRID_W]], axis=0)
    down = jnp.concatenate([mv[GRID_W:], nm], axis=0)
    yv = cw[0:1, A_HORIZ:] * up + cw[1:2, A_HORIZ:] * mv + cw[2:3, A_HORIZ:] * down
    ya_ref[...] = (gate * jnp.concatenate([yh, yv], axis=-1)).astype(BF16)
    yc_ref[...] = _sgu(pc_ref[...].astype(F32), lng_ref[...], lnb_ref[...], ws_ref[...],
                       bsm_ref[...]).astype(BF16)


def _mix_seq_kernel(pa_ref, pc_ref, cw_ref, lng_ref, lnb_ref, ws_ref, bsm_ref, ya_ref, yc_ref):
    pa = pa_ref[...].astype(F32)
    tt = pa.shape[0]
    cw = cw_ref[...]
    m = pa[:, A_W:2 * A_W] * pa[:, 2 * A_W:]
    row = lax.broadcasted_iota(jnp.int32, (tt, A_W), 0)
    left = jnp.where(row == 0, 0.0, pltpu.roll(m, 1, 0))
    right = jnp.where(row == tt - 1, 0.0, pltpu.roll(m, tt - 1, 0))
    y = cw[0:1] * left + cw[1:2] * m + cw[2:3] * right
    ya_ref[...] = (pa[:, :A_W] * y).astype(BF16)
    yc_ref[...] = _sgu(pc_ref[...].astype(F32), lng_ref[...], lnb_ref[...], ws_ref[...],
                       bsm_ref[...]).astype(BF16)


def _mixers_ac(pa, pc, conv_a, ln_g, ln_b, ws, bsm, *, grid_conv):
    b, t, _ = pa.shape
    const = lambda bi, j: (0, 0)
    tok = lambda bi, j: (bi, j, 0)
    small = [pl.BlockSpec(conv_a.shape, const), pl.BlockSpec(ln_g.shape, const),
             pl.BlockSpec(ln_b.shape, const), pl.BlockSpec(ws.shape, const), pl.BlockSpec(bsm.shape, const)]
    outs = dict(out_shape=[jax.ShapeDtypeStruct((b, t, A_W), BF16), jax.ShapeDtypeStruct((b, t, C_W), BF16)],
                compiler_params=_cparams("parallel", "parallel"))
    if not grid_conv:
        return pl.pallas_call(
            _mix_seq_kernel, grid=(b, 1),
            in_specs=[pl.BlockSpec((None, t, 3 * A_W), tok), pl.BlockSpec((None, t, 2 * C_W), tok)] + small,
            out_specs=[pl.BlockSpec((None, t, A_W), tok), pl.BlockSpec((None, t, C_W), tok)],
            name="mixers_ac_seq", **outs,
        )(pa, pc, conv_a, ln_g, ln_b, ws, bsm)
    tt = _row_tile(t)
    nj = t // tt
    rows = tt // GRID_W
    last = t // GRID_W - 1
    return pl.pallas_call(
        functools.partial(_mix_grid_kernel, nj=nj), grid=(b, nj),
        in_specs=[pl.BlockSpec((None, tt, 3 * A_W), tok),
                  pl.BlockSpec((None, GRID_W, 3 * A_W), lambda bi, j: (bi, jnp.maximum(j * rows - 1, 0), 0)),
                  pl.BlockSpec((None, GRID_W, 3 * A_W), lambda bi, j: (bi, jnp.minimum((j + 1) * rows, last), 0)),
                  pl.BlockSpec((None, tt, 2 * C_W), tok)] + small,
        out_specs=[pl.BlockSpec((None, tt, A_W), tok), pl.BlockSpec((None, tt, C_W), tok)],
        name="mixers_ac_grid", **outs,
    )(pa, pa, pa, pc, conv_a, ln_g, ln_b, ws, bsm)


HALO = 16
PAIRS = N_HEADS // 2
PAIR_W = 2 * HEAD_W


def _block_diag2(y):
    lane = lax.broadcasted_iota(jnp.int32, y.shape, 1)
    return jnp.concatenate([jnp.where(lane < GDN_CHUNK, y, 0.0), jnp.where(lane >= GDN_CHUNK, y, 0.0)],
                           axis=0).astype(BF16)


def _pdot(a, b):
    return jnp.dot(a.astype(BF16), _block_diag2(b), preferred_element_type=F32)


def _unit_tri_inverses(lds, ii, jl):
    xs = [jnp.where(ii == jl, 1.0, 0.0) - jnp.where((ii >> 1) == (jl >> 1), ld, 0.0) for ld in lds]
    for k in range(1, 6):
        mk = ((ii >> (k + 1)) == (jl >> (k + 1))) & ((ii >> k) != (jl >> k))
        ts = [_pdot(jnp.where(mk, ld, 0.0), x) for ld, x in zip(lds, xs)]
        xs = [x - _pdot(x, t) for x, t in zip(xs, ts)]
    return xs


def _gdn_prep_kernel(pq_ref, pqp_ref, pqn_ref, ab_ref, cw_ref, alog_t, dtb_t,
                     l1_ref, l2_ref, u_ref, dec_ref, q_s, k_s, v_s, *, cpt, nt):
    j = pl.program_id(1)
    tt = cpt * GDN_CHUNK
    x = pq_ref[...].astype(F32)
    w3 = 3 * GDN_W
    has_prev = jnp.where(j > 0, 1.0, 0.0)
    has_next = jnp.where(j < nt - 1, 1.0, 0.0)
    prev_row = pqp_ref[HALO - 1:HALO, :].astype(F32) * has_prev
    next_row = pqn_ref[0:1, :].astype(F32) * has_next
    row8 = lax.broadcasted_iota(jnp.int32, (8, w3), 0)
    xm1 = pltpu.roll(x, 1, 0)
    xm1 = jnp.concatenate([jnp.where(row8 == 0, prev_row, xm1[:8]), xm1[8:]], axis=0)
    xp1 = pltpu.roll(x, tt - 1, 0)
    xp1 = jnp.concatenate([xp1[:tt - 8], jnp.where(row8 == 7, next_row, xp1[tt - 8:])], axis=0)
    cw = cw_ref[...]
    s = _silu(cw[0:1] * xm1 + cw[1:2] * x + cw[2:3] * xp1)
    for h in range(N_HEADS):
        lo, hi = h * HEAD_W, (h + 1) * HEAD_W
        qh = s[:, lo:hi]
        q_s[:, lo:hi] = qh * (lax.rsqrt(jnp.sum(qh * qh, axis=-1, keepdims=True) + EPS) * HEAD_W ** -0.5)
        kh = s[:, GDN_W + lo:GDN_W + hi]
        k_s[:, lo:hi] = kh * lax.rsqrt(jnp.sum(kh * kh, axis=-1, keepdims=True) + EPS)
    v_s[...] = s[:, 2 * GDN_W:]

    cc = GDN_CHUNK
    ii = lax.broadcasted_iota(jnp.int32, (cc, LANES), 0)
    lane = lax.broadcasted_iota(jnp.int32, (cc, LANES), 1)
    jl = lane & (cc - 1)
    first_head = lane < cc
    lane_row = lax.broadcasted_iota(jnp.int32, (1, LANES), 1)
    zero_k = jnp.zeros((cc, HEAD_W), BF16)

    gts = []
    for c in range(cpt):
        abn = ab_ref[c * cc:(c + 1) * cc, :]
        gts.append(jnp.concatenate([abn, pltpu.roll(abn, LANES - 8, 1)], axis=0).T[0:8, :])
    gt = jnp.concatenate(gts, axis=0)
    beta_all = jax.nn.sigmoid(gt)
    g_all = -jnp.exp(alog_t[...]) * _softplus(gt + dtb_t[...])
    tok = lax.broadcasted_iota(jnp.int32, gt.shape, 1) & (cc - 1)
    pre = suf = g_all
    sh = 1
    while sh < cc:
        pre = pre + jnp.where(tok >= sh, pltpu.roll(pre, sh, 1), 0.0)
        suf = suf + jnp.where(tok < cc - sh, pltpu.roll(suf, LANES - sh, 1), 0.0)
        sh *= 2
    gam_all = jnp.where((lax.broadcasted_iota(jnp.int32, gt.shape, 0) & 7) < 6, pre, suf)
    gcols = jnp.concatenate([gam_all, jnp.zeros((LANES - 8 * cpt, LANES), F32)], axis=0).T

    def setup(c):
        kc = k_s[c * cc:(c + 1) * cc, :]
        qc = q_s[c * cc:(c + 1) * cc, :]
        vb = v_s[c * cc:(c + 1) * cc, :].astype(BF16)
        kb = kc.astype(BF16)
        pairs = []
        for p in range(PAIRS):
            lo = p * PAIR_W
            k0, k1 = kb[:, lo:lo + HEAD_W], kb[:, lo + HEAD_W:lo + PAIR_W]
            kbd = jnp.concatenate([jnp.concatenate([k0, zero_k], axis=1),
                                   jnp.concatenate([zero_k, k1], axis=1)], axis=0)
            vbd = jnp.concatenate([jnp.concatenate([vb[:, lo:lo + HEAD_W], zero_k], axis=1),
                                   jnp.concatenate([zero_k, vb[:, lo + HEAD_W:lo + PAIR_W]], axis=1)], axis=0)
            kq = lax.dot_general(jnp.concatenate([kb[:, lo:lo + PAIR_W], qc[:, lo:lo + PAIR_W].astype(BF16)], axis=0),
                                 kbd, (((1,), (1,)), ((), ())), preferred_element_type=F32)
            kt = jnp.concatenate([kc[:, lo:lo + HEAD_W], kc[:, lo + HEAD_W:lo + PAIR_W]], axis=0).T
            pairs.append((kbd, vbd, kq[:cc], kq[cc:], kt, qc[:, lo:lo + HEAD_W], qc[:, lo + HEAD_W:lo + PAIR_W]))
        probs = []
        for d in range(2):
            incl = (ii >= jl) if d == 0 else (ii <= jl)
            strict = (ii > jl) if d == 0 else (ii < jl)
            for p in range(PAIRS):
                kbd, vbd, kkt, qk, kt, q0, q1 = pairs[p]
                rb = 8 * c + 2 * d + p
                rg = rb + 4
                grow = gam_all[rg:rg + 1, :]
                brow = beta_all[rb:rb + 1, :]
                gc0 = jnp.broadcast_to(gcols[:cc, rg:rg + 1], (cc, LANES))
                gc1 = jnp.broadcast_to(gcols[cc:, rg:rg + 1], (cc, LANES))
                dm = jnp.exp(jnp.where(incl, jnp.where(first_head, gc0, gc1) - grow, -1e30))
                ld = jnp.where(strict, kkt * dm, 0.0) * brow
                top = jnp.where(incl, qk * dm, 0.0) * brow
                probs.append(dict(c=c, d=d, p=p, ld=ld, top=top, grow=grow, brow=brow, gc0=gc0, gc1=gc1,
                                  kbd=kbd, vbd=vbd, kt=kt, q0=q0, q1=q1))
        return probs

    def emit(pr, x):
        c, d, p, grow, brow = pr["c"], pr["d"], pr["p"], pr["grow"], pr["brow"]
        wk = jnp.dot((x * jnp.exp(grow)).astype(BF16), pr["kbd"], preferred_element_type=F32)
        u_ref[d, c, :, p * PAIR_W:(p + 1) * PAIR_W] = jnp.dot(x.astype(BF16), pr["vbd"], preferred_element_type=F32)
        l1_ref[d, c, 2 * p] = jnp.concatenate([wk[:, :HEAD_W], pr["q0"] * jnp.exp(pr["gc0"])], axis=0).astype(BF16)
        l1_ref[d, c, 2 * p + 1] = jnp.concatenate([wk[:, HEAD_W:], pr["q1"] * jnp.exp(pr["gc1"])], axis=0).astype(BF16)
        last = cc - 1 if d == 0 else 0
        gl0 = grow[:, last:last + 1]
        gl1 = grow[:, cc + last:cc + last + 1]
        tail = jnp.exp(jnp.where(lane_row < cc, gl0, gl1) - grow) * brow
        l2_ref[d, c, p] = jnp.concatenate([pr["top"], pr["kt"] * tail], axis=0).astype(BF16)
        dec_ref[d, c, :, p * PAIR_W:(p + 1) * PAIR_W] = jnp.concatenate(
            [jnp.broadcast_to(jnp.exp(gl0), (1, HEAD_W)), jnp.broadcast_to(jnp.exp(gl1), (1, HEAD_W))], axis=1)

    probs = []
    for c in range(cpt):
        probs += setup(c)
    xs = _unit_tri_inverses([pr["ld"] for pr in probs], ii, jl)
    for pr, xi in zip(probs, xs):
        emit(pr, xi)


def _gdn_tile_chunks(t):
    return min(t // GDN_CHUNK, 8)


def _gdn_prep(pq, pab, conv_qkv, a_log, dt_bias):
    b, t, w3 = pq.shape
    nc = t // GDN_CHUNK
    cpt = _gdn_tile_chunks(t)
    tt = cpt * GDN_CHUNK
    nt = t // tt
    hb = tt // HALO
    def gate_rows(a):
        rows = jnp.repeat(a.reshape(2 * PAIRS, 2), GDN_CHUNK, axis=1)
        return jnp.tile(jnp.concatenate([jnp.zeros_like(rows), rows], axis=0), (cpt, 1))

    alog_t, dtb_t = gate_rows(a_log), gate_rows(dt_bias)
    const = lambda bi, j: (0, 0)
    tok = lambda bi, j: (bi, j, 0)
    return pl.pallas_call(
        functools.partial(_gdn_prep_kernel, cpt=cpt, nt=nt),
        grid=(b, nt),
        in_specs=[pl.BlockSpec((None, tt, w3), tok),
                  pl.BlockSpec((None, HALO, w3), lambda bi, j: (bi, jnp.maximum(j * hb - 1, 0), 0)),
                  pl.BlockSpec((None, HALO, w3), lambda bi, j: (bi, jnp.minimum((j + 1) * hb, t // HALO - 1), 0)),
                  pl.BlockSpec((None, tt, LANES), tok),
                  pl.BlockSpec((3, w3), const),
                  pl.BlockSpec((8 * cpt, LANES), const), pl.BlockSpec((8 * cpt, LANES), const)],
        out_specs=[pl.BlockSpec((None, 2, cpt, N_HEADS, 2 * GDN_CHUNK, HEAD_W), lambda bi, j: (bi, 0, j, 0, 0, 0)),
                   pl.BlockSpec((None, 2, cpt, PAIRS, 3 * GDN_CHUNK, HEAD_W), lambda bi, j: (bi, 0, j, 0, 0, 0)),
                   pl.BlockSpec((None, 2, cpt, GDN_CHUNK, GDN_W), lambda bi, j: (bi, 0, j, 0, 0)),
                   pl.BlockSpec((None, 2, cpt, 1, GDN_W), lambda bi, j: (bi, 0, j, 0, 0))],
        out_shape=[jax.ShapeDtypeStruct((b, 2, nc, N_HEADS, 2 * GDN_CHUNK, HEAD_W), BF16),
                   jax.ShapeDtypeStruct((b, 2, nc, PAIRS, 3 * GDN_CHUNK, HEAD_W), BF16),
                   jax.ShapeDtypeStruct((b, 2, nc, GDN_CHUNK, GDN_W), F32),
                   jax.ShapeDtypeStruct((b, 2, nc, 1, GDN_W), F32)],
        scratch_shapes=[pltpu.VMEM((tt, GDN_W), F32)] * 3,
        compiler_params=_cparams("parallel", "parallel"),
        name="gdn_prep",
    )(pq, pq, pq, pab, conv_qkv, alog_t, dtb_t)


SCAN_BATCH = 4
SCAN_CHUNKS = 4


def _gdn_scan_kernel(l1f, l1b, l2f, l2b, uf, ub, decf, decb, s0_ref, of_ref, ob_ref, sfin_ref, s_ref,
                     *, cpg, nj, nb):
    j = pl.program_id(1)

    @pl.when(j == 0)
    def _():
        s_ref[...] = s0_ref[...]

    zero = jnp.zeros((GDN_CHUNK, HEAD_W), BF16)

    def step(i, carry):
        chains = []
        for n in range(nb):
            chains.append((n, 0, l1f, l2f, uf, decf, of_ref, i))
            chains.append((n, 1, l1b, l2b, ub, decb, ob_ref, cpg - 1 - i))
        first = []
        for n, d, l1, l2, u, dec, o_ref, c in chains:
            vts, qss = [], []
            for h in range(N_HEADS):
                sh = s_ref[n, d, :, h * HEAD_W:(h + 1) * HEAD_W].astype(BF16)
                r1 = jnp.dot(l1[n, c, h], sh, preferred_element_type=F32)
                vts.append((u[n, c, :, h * HEAD_W:(h + 1) * HEAD_W] - r1[:GDN_CHUNK]).astype(BF16))
                qss.append(r1[GDN_CHUNK:])
            first.append((vts, qss))
        for (n, d, l1, l2, u, dec, o_ref, c), (vts, qss) in zip(chains, first):
            intra = []
            for p in range(N_HEADS // 2):
                rhs = jnp.concatenate([jnp.concatenate([vts[2 * p], zero], axis=-1),
                                       jnp.concatenate([zero, vts[2 * p + 1]], axis=-1)], axis=0)
                r2 = jnp.dot(l2[n, c, p], rhs, preferred_element_type=F32)
                intra.append(r2[:GDN_CHUNK])
                lo, hi = 2 * p * HEAD_W, 2 * (p + 1) * HEAD_W
                s_ref[n, d, :, lo:hi] = s_ref[n, d, :, lo:hi] * dec[n, c][:, lo:hi] + r2[GDN_CHUNK:]
            o = jnp.concatenate(qss, axis=-1) + jnp.concatenate(intra, axis=-1)
            o_ref[n, pl.ds(pl.multiple_of(c * GDN_CHUNK, GDN_CHUNK), GDN_CHUNK), :] = o.astype(o_ref.dtype)
        return carry

    lax.fori_loop(0, cpg, step, 0)

    @pl.when(j == nj - 1)
    def _():
        sfin_ref[...] = s_ref[...]


def _gdn_scan(l1, l2, u, dec, s0):
    b, _, nc = l1.shape[:3]
    cpg = min(nc, SCAN_CHUNKS)
    nj = nc // cpg
    t = nc * GDN_CHUNK
    fwd5 = lambda bi, j: (bi, 0, j, 0, 0, 0)
    bwd5 = lambda bi, j: (bi, 1, nj - 1 - j, 0, 0, 0)
    fwd4 = lambda bi, j: (bi, 0, j, 0, 0)
    bwd4 = lambda bi, j: (bi, 1, nj - 1 - j, 0, 0)
    nb = SCAN_BATCH if b % SCAN_BATCH == 0 else 1
    b1 = (nb, None, cpg, N_HEADS, 2 * GDN_CHUNK, HEAD_W)
    b2 = (nb, None, cpg, N_HEADS // 2, 3 * GDN_CHUNK, HEAD_W)
    bu = (nb, None, cpg, GDN_CHUNK, GDN_W)
    bd = (nb, None, cpg, 1, GDN_W)
    st = pl.BlockSpec((nb, 2, HEAD_W, GDN_W), lambda bi, j: (bi, 0, 0, 0))
    return pl.pallas_call(
        functools.partial(_gdn_scan_kernel, cpg=cpg, nj=nj, nb=nb),
        grid=(b // nb, nj),
        in_specs=[pl.BlockSpec(b1, fwd5), pl.BlockSpec(b1, bwd5), pl.BlockSpec(b2, fwd5), pl.BlockSpec(b2, bwd5),
                  pl.BlockSpec(bu, fwd4), pl.BlockSpec(bu, bwd4), pl.BlockSpec(bd, fwd4), pl.BlockSpec(bd, bwd4), st],
        out_specs=[pl.BlockSpec((nb, cpg * GDN_CHUNK, GDN_W), lambda bi, j: (bi, j, 0)),
                   pl.BlockSpec((nb, cpg * GDN_CHUNK, GDN_W), lambda bi, j: (bi, nj - 1 - j, 0)), st],
        out_shape=[jax.ShapeDtypeStruct((b, t, GDN_W), BF16), jax.ShapeDtypeStruct((b, t, GDN_W), BF16),
                   jax.ShapeDtypeStruct((b, 2, HEAD_W, GDN_W), F32)],
        scratch_shapes=[pltpu.VMEM((nb, 2, HEAD_W, GDN_W), F32)],
        compiler_params=_cparams("parallel", "arbitrary"),
        name="gdn_scan",
    )(l1, l1, l2, l2, u, u, dec, dec, s0)


def _out_kernel(x_ref, of_ref, ob_ref, z_ref, ya_ref, yc_ref, mod_ref, gon_ref, wo_ref, gpost_ref, o_ref):
    o = of_ref[...].astype(F32) + ob_ref[...].astype(F32)
    z = z_ref[...].astype(F32)
    ybs = []
    for h in range(N_HEADS):
        oh = o[:, h * HEAD_W:(h + 1) * HEAD_W]
        oh = oh * lax.rsqrt(jnp.mean(oh * oh, axis=-1, keepdims=True) + EPS) * gon_ref[...]
        ybs.append((oh * _silu(z[:, h * HEAD_W:(h + 1) * HEAD_W])).astype(BF16))
    y = jnp.concatenate([ya_ref[...]] + ybs + [yc_ref[...]], axis=-1)
    r = jnp.dot(y, wo_ref[...], preferred_element_type=F32)
    o_ref[...] = x_ref[...] + mod_ref[2:3, :] * _rms(r, gpost_ref[...])


def _out_proj(x, of, ob, z, ya, yc, mod, g_onorm, w_o, g_post):
    b, t, d = x.shape
    tm = _row_tile(t)
    const = lambda bi, j: (0, 0)
    tok = lambda bi, j: (bi, j, 0)
    return pl.pallas_call(
        _out_kernel,
        grid=(b, t // tm),
        in_specs=[pl.BlockSpec((None, tm, d), tok),
                  pl.BlockSpec((None, tm, GDN_W), tok), pl.BlockSpec((None, tm, GDN_W), tok),
                  pl.BlockSpec((None, tm, GDN_W), tok),
                  pl.BlockSpec((None, tm, A_W), tok), pl.BlockSpec((None, tm, C_W), tok),
                  pl.BlockSpec((None, 8, d), lambda bi, j: (bi, 0, 0)),
                  pl.BlockSpec((1, HEAD_W), const), pl.BlockSpec(w_o.shape, const), pl.BlockSpec((1, d), const)],
        out_specs=pl.BlockSpec((None, tm, d), tok),
        out_shape=jax.ShapeDtypeStruct((b, t, d), F32),
        compiler_params=_cparams("parallel", "parallel"),
        name="out_proj",
    )(x, of, ob, z, ya, yc, mod, g_onorm, w_o, g_post)


FFN_TILE = 256


def _ffn_kernel(x_ref, mod_ref, gpre_ref, w1_ref, w2_ref, gpost_ref, o_ref, *, hidden):
    x = x_ref[...]
    hb = (_rms(x, gpre_ref[...]) * (1.0 + mod_ref[4:5, :]) + mod_ref[3:4, :]).astype(BF16)
    acc = None
    for c0 in range(0, hidden, FFN_TILE):
        g = jnp.dot(hb, w1_ref[:, c0:c0 + FFN_TILE], preferred_element_type=F32)
        u = jnp.dot(hb, w1_ref[:, hidden + c0:hidden + c0 + FFN_TILE], preferred_element_type=F32)
        part = jnp.dot((_silu(g) * u).astype(BF16), w2_ref[c0:c0 + FFN_TILE, :], preferred_element_type=F32)
        acc = part if acc is None else acc + part
    o_ref[...] = x + mod_ref[5:6, :] * _rms(acc, gpost_ref[...])


def _ffn(x, mod, g_pre, w1, w2, g_post):
    b, t, d = x.shape
    hidden = w2.shape[0]
    tm = _row_tile(t)
    const = lambda bi, j: (0, 0)
    tok = lambda bi, j: (bi, j, 0)
    return pl.pallas_call(
        functools.partial(_ffn_kernel, hidden=hidden),
        grid=(b, t // tm),
        in_specs=[pl.BlockSpec((None, tm, d), tok),
                  pl.BlockSpec((None, 8, d), lambda bi, j: (bi, 0, 0)),
                  pl.BlockSpec((1, d), const),
                  pl.BlockSpec(w1.shape, const, pipeline_mode=pl.Buffered(1)),
                  pl.BlockSpec(w2.shape, const, pipeline_mode=pl.Buffered(1)),
                  pl.BlockSpec((1, d), const)],
        out_specs=pl.BlockSpec((None, tm, d), tok),
        out_shape=jax.ShapeDtypeStruct((b, t, d), F32),
        compiler_params=_cparams("parallel", "parallel"),
        name="ffn",
    )(x, mod, g_pre, w1, w2, g_post)


OFF_A, OFF_Q, OFF_AB, OFF_Z, OFF_C = 0, 3 * A_W, 3 * A_W + 3 * GDN_W, 3 * A_W + 3 * GDN_W + 16, 3 * A_W + 4 * GDN_W + 16
GATE_PERM = [4 * ((n % 8) // 2) + 2 * (n % 2) + n // 8 for n in range(16)]


def _mod_rows(rows, b, d):
    m = jnp.broadcast_to(rows.reshape(-1, 6, d), (b, 6, d))
    return jnp.pad(m, ((0, 0), (0, 2), (0, 0)))


def kernel(x, c, ctx, c_ctx, w_mod, b_mod, g_pre_mix, g_post_mix, g_pre_ffn, g_post_ffn, w_in, conv_a, conv_qkv,
           a_log, dt_bias, g_onorm, ln_c_g, ln_c_b, w_s, b_s, w_o, w_ffn_in, w_ffn_out):
    b, t, d = x.shape
    depth = w_mod.shape[0]
    cc = jnp.concatenate([c, c_ctx[None, :], jnp.zeros((16 - b - 1, d), F32)], axis=0)
    mod = _modulation(cc, w_mod, b_mod)
    s_zero = jnp.zeros((b, 2, HEAD_W, GDN_W), F32)
    for l in range(depth):
        last = l == depth - 1
        modx = _mod_rows(mod[l, :b], b, d)
        modc = _mod_rows(mod[l, b:b + 1], b, d)
        wl = w_in[l].astype(BF16)
        w_gate = wl[:, OFF_AB:OFF_Z][:, jnp.array(GATE_PERM)]
        ws_in = [wl[:, OFF_Q:OFF_AB], wl[:, OFF_Z:OFF_C], wl[:, OFF_A:OFF_Q], wl[:, OFF_C:],
                 jnp.pad(w_gate, ((0, 0), (0, LANES - 16)))]
        gpm, gqm = g_pre_mix[l][None, :], g_post_mix[l][None, :]
        gpf, gqf = g_pre_ffn[l][None, :], g_post_ffn[l][None, :]
        sgu_w = w_s[l].reshape(C_GROUPS * C_CHUNK, C_CHUNK).astype(BF16)
        sgu_b = jnp.repeat(b_s[l].T, C_GD, axis=1)
        lng, lnb = ln_c_g[l][None, :], ln_c_b[l][None, :]
        wo = w_o[l].astype(BF16)
        w1, w2 = w_ffn_in[l].astype(BF16), w_ffn_out[l].astype(BF16)
        gon = g_onorm[l][None, :]

        def mixer_inputs(stream, m):
            pq, pz, pa, pc, pab = _in_proj(stream, m, gpm, ws_in)
            return pq, pz, pa, pc, _gdn_prep(pq, pab, conv_qkv[l], a_log[l], dt_bias[l])

        def finish(stream, m, of, ob, pz, pa, pc, grid_conv):
            ya, yc = _mixers_ac(pa, pc, conv_a[l], lng, lnb, sgu_w, sgu_b, grid_conv=grid_conv)
            stream = _out_proj(stream, of, ob, pz, ya, yc, m, gon, wo, gqm)
            return _ffn(stream, m, gpf, w1, w2, gqf)

        pq, pz, pa, pc, ops = mixer_inputs(ctx, modc)
        of, ob, s_ctx = _gdn_scan(*ops, s_zero)
        if not last:
            ctx = finish(ctx, modc, of, ob, pz, pa, pc, False)
        pq, pz, pa, pc, ops = mixer_inputs(x, modx)
        of, ob, _ = _gdn_scan(*ops, s_ctx)
        x = finish(x, modx, of, ob, pz, pa, pc, True)
    return x
```

```python
import functools

import jax
import jax.numpy as jnp
from jax import lax
from jax.experimental import pallas as pl
from jax.experimental.pallas import tpu as pltpu

F32 = jnp.float32
BF16 = jnp.bfloat16
HIGHEST = lax.Precision.HIGHEST

EPS = 1e-6
GRID_W = 64
N_HEADS = 4
HEAD_W = 128
GDN_W = N_HEADS * HEAD_W
GDN_CHUNK = 64
A_W = 256
A_HORIZ = 128
C_W = 256
C_GROUPS = 4
C_GD = C_W // C_GROUPS
C_CHUNK = 128
LANES = 128
VMEM_LIMIT = 56 * 1024 * 1024


def _cparams(*sem):
    return pltpu.CompilerParams(dimension_semantics=sem, vmem_limit_bytes=VMEM_LIMIT)


def _hdot(a, b):
    return jnp.dot(a, b, precision=HIGHEST, preferred_element_type=F32)


def _rms(x, g):
    return x * lax.rsqrt(jnp.mean(x * x, axis=-1, keepdims=True) + EPS) * g


def _silu(x):
    return x * jax.nn.sigmoid(x)


def _gelu(x):
    return 0.5 * x * (1.0 + lax.erf(x * 0.7071067811865476))


def _softplus(x):
    return jnp.maximum(x, 0.0) + jnp.log1p(jnp.exp(-jnp.abs(x)))


def _mod_kernel(c_ref, w_ref, b_ref, o_ref):
    o_ref[...] = _hdot(_silu(c_ref[...]), w_ref[...]) + b_ref[...]


def _modulation(cc, w_mod, b_mod):
    depth, d, six_d = w_mod.shape
    r = cc.shape[0]
    return pl.pallas_call(
        _mod_kernel,
        grid=(depth, six_d // d),
        in_specs=[pl.BlockSpec((r, d), lambda l, j: (0, 0)),
                  pl.BlockSpec((None, d, d), lambda l, j: (l, 0, j)),
                  pl.BlockSpec((None, 1, d), lambda l, j: (l, 0, j))],
        out_specs=pl.BlockSpec((None, r, d), lambda l, j: (l, 0, j)),
        out_shape=jax.ShapeDtypeStruct((depth, r, six_d), F32),
        compiler_params=_cparams("parallel", "parallel"),
        name="modulation",
    )(cc, w_mod, b_mod.reshape(depth, 1, six_d))


def _in_kernel(x_ref, mod_ref, g_ref, wq, wz, wa, wc, wab, oq, oz, oa, oc, oab):
    h = _rms(x_ref[...], g_ref[...]) * (1.0 + mod_ref[1:2, :]) + mod_ref[0:1, :]
    hb = h.astype(BF16)
    for w, o in ((wq, oq), (wz, oz), (wa, oa), (wc, oc)):
        o[...] = jnp.dot(hb, w[...], preferred_element_type=F32).astype(BF16)
    oab[...] = jnp.dot(hb, wab[...], preferred_element_type=F32)


def _row_tile(t, rows=512):
    return min(t, rows)


def _in_proj(x, mod, g_pre, ws):
    b, t, d = x.shape
    tm = _row_tile(t, 1024)
    widths = [w.shape[1] for w in ws]
    const = lambda bi, j: (0, 0)
    tok = lambda bi, j: (bi, j, 0)
    return pl.pallas_call(
        _in_kernel,
        grid=(b, t // tm),
        in_specs=[pl.BlockSpec((None, tm, d), tok),
                  pl.BlockSpec((None, 8, d), lambda bi, j: (bi, 0, 0)),
                  pl.BlockSpec((1, d), const)]
                 + [pl.BlockSpec((d, n), const) for n in widths],
        out_specs=[pl.BlockSpec((None, tm, n), tok) for n in widths],
        out_shape=[jax.ShapeDtypeStruct((b, t, n), BF16) for n in widths[:-1]]
                  + [jax.ShapeDtypeStruct((b, t, widths[-1]), F32)],
        compiler_params=_cparams("parallel", "parallel"),
        name="in_proj",
    )(x, mod, g_pre, *ws)


def _sgu(pc, lng, lnb, ws, bsm):
    tt = pc.shape[0]
    u = _gelu(pc[:, :C_W])
    v = _gelu(pc[:, C_W:])
    mu = jnp.mean(v, axis=-1, keepdims=True)
    vc = v - mu
    var = jnp.mean(vc * vc, axis=-1, keepdims=True)
    vb = (vc * lax.rsqrt(var + EPS) * lng + lnb).astype(BF16)
    grp = lax.broadcasted_iota(jnp.int32, (C_CHUNK, C_W), 1) // C_GD
    outs = []
    for n in range(tt // C_CHUNK):
        r = jnp.dot(ws, vb[n * C_CHUNK:(n + 1) * C_CHUNK, :], preferred_element_type=F32)
        mixed = r[3 * C_CHUNK:]
        for g in (2, 1, 0):
            mixed = jnp.where(grp == g, r[g * C_CHUNK:(g + 1) * C_CHUNK], mixed)
        outs.append(mixed + bsm)
    return u * jnp.concatenate(outs, axis=0)


def _conv_grid(pa, pv, nv, cw, has_prev, has_next):
    tt = pa.shape[0]
    gate = pa[:, :A_W]
    m = pa[:, A_W:2 * A_W] * pa[:, 2 * A_W:]
    col = lax.broadcasted_iota(jnp.int32, (tt, A_HORIZ), 0) % GRID_W
    mh = m[:, :A_HORIZ]
    left = jnp.where(col == 0, 0.0, pltpu.roll(mh, 1, 0))
    right = jnp.where(col == GRID_W - 1, 0.0, pltpu.roll(mh, tt - 1, 0))
    yh = cw[0:1, :A_HORIZ] * left + cw[1:2, :A_HORIZ] * mh + cw[2:3, :A_HORIZ] * right
    mv = m[:, A_HORIZ:]
    pm = pv[:, A_W + A_HORIZ:2 * A_W] * pv[:, 2 * A_W + A_HORIZ:] * has_prev
    nm = nv[:, A_W + A_HORIZ:2 * A_W] * nv[:, 2 * A_W + A_HORIZ:] * has_next
    up = jnp.concatenate([pm, mv[:tt - GRID_W]], axis=0)
    down = jnp.concatenate([mv[GRID_W:], nm], axis=0)
    yv = cw[0:1, A_HORIZ:] * up + cw[1:2, A_HORIZ:] * mv + cw[2:3, A_HORIZ:] * down
    return gate * jnp.concatenate([yh, yv], axis=-1)


def _conv_seq(pa, cw):
    tt = pa.shape[0]
    m = pa[:, A_W:2 * A_W] * pa[:, 2 * A_W:]
    row = lax.broadcasted_iota(jnp.int32, (tt, A_W), 0)
    left = jnp.where(row == 0, 0.0, pltpu.roll(m, 1, 0))
    right = jnp.where(row == tt - 1, 0.0, pltpu.roll(m, tt - 1, 0))
    return pa[:, :A_W] * (cw[0:1] * left + cw[1:2] * m + cw[2:3] * right)


HALO = 16
PAIRS = N_HEADS // 2
PAIR_W = 2 * HEAD_W


def _block_diag2(y):
    lane = lax.broadcasted_iota(jnp.int32, y.shape, 1)
    return jnp.concatenate([jnp.where(lane < GDN_CHUNK, y, 0.0), jnp.where(lane >= GDN_CHUNK, y, 0.0)],
                           axis=0).astype(BF16)


def _pdot(a, b):
    return jnp.dot(a.astype(BF16), _block_diag2(b), preferred_element_type=F32)


def _unit_tri_inverses(lds, ii, jl):
    xs = [jnp.where(ii == jl, 1.0, 0.0) - jnp.where((ii >> 1) == (jl >> 1), ld, 0.0) for ld in lds]
    for k in range(1, 6):
        mk = ((ii >> (k + 1)) == (jl >> (k + 1))) & ((ii >> k) != (jl >> k))
        ts = [_pdot(jnp.where(mk, ld, 0.0), x) for ld, x in zip(lds, xs)]
        xs = [x - _pdot(x, t) for x, t in zip(xs, ts)]
    return xs


def _gdn_prep_kernel(pq_ref, pqp_ref, pqn_ref, ab_ref, cw_ref, alog_t, dtb_t,
                     l1_ref, l2_ref, u_ref, dec_ref, q_s, k_s, v_s, *, cpt, nt):
    j = pl.program_id(1)
    tt = cpt * GDN_CHUNK
    x = pq_ref[...].astype(F32)
    w3 = 3 * GDN_W
    has_prev = jnp.where(j > 0, 1.0, 0.0)
    has_next = jnp.where(j < nt - 1, 1.0, 0.0)
    prev_row = pqp_ref[HALO - 1:HALO, :].astype(F32) * has_prev
    next_row = pqn_ref[0:1, :].astype(F32) * has_next
    row8 = lax.broadcasted_iota(jnp.int32, (8, w3), 0)
    xm1 = pltpu.roll(x, 1, 0)
    xm1 = jnp.concatenate([jnp.where(row8 == 0, prev_row, xm1[:8]), xm1[8:]], axis=0)
    xp1 = pltpu.roll(x, tt - 1, 0)
    xp1 = jnp.concatenate([xp1[:tt - 8], jnp.where(row8 == 7, next_row, xp1[tt - 8:])], axis=0)
    cw = cw_ref[...]
    s = _silu(cw[0:1] * xm1 + cw[1:2] * x + cw[2:3] * xp1)
    for h in range(N_HEADS):
        lo, hi = h * HEAD_W, (h + 1) * HEAD_W
        qh = s[:, lo:hi]
        q_s[:, lo:hi] = qh * (lax.rsqrt(jnp.sum(qh * qh, axis=-1, keepdims=True) + EPS) * HEAD_W ** -0.5)
        kh = s[:, GDN_W + lo:GDN_W + hi]
        k_s[:, lo:hi] = kh * lax.rsqrt(jnp.sum(kh * kh, axis=-1, keepdims=True) + EPS)
    v_s[...] = s[:, 2 * GDN_W:]

    cc = GDN_CHUNK
    ii = lax.broadcasted_iota(jnp.int32, (cc, LANES), 0)
    lane = lax.broadcasted_iota(jnp.int32, (cc, LANES), 1)
    jl = lane & (cc - 1)
    first_head = lane < cc
    lane_row = lax.broadcasted_iota(jnp.int32, (1, LANES), 1)
    zero_k = jnp.zeros((cc, HEAD_W), BF16)

    gts = []
    for c in range(cpt):
        abn = ab_ref[c * cc:(c + 1) * cc, :]
        gts.append(jnp.concatenate([abn, pltpu.roll(abn, LANES - 8, 1)], axis=0).T[0:8, :])
    gt = jnp.concatenate(gts, axis=0)
    beta_all = jax.nn.sigmoid(gt)
    g_all = -jnp.exp(alog_t[...]) * _softplus(gt + dtb_t[...])
    tok = lax.broadcasted_iota(jnp.int32, gt.shape, 1) & (cc - 1)
    pre = suf = g_all
    sh = 1
    while sh < cc:
        pre = pre + jnp.where(tok >= sh, pltpu.roll(pre, sh, 1), 0.0)
        suf = suf + jnp.where(tok < cc - sh, pltpu.roll(suf, LANES - sh, 1), 0.0)
        sh *= 2
    gam_all = jnp.where((lax.broadcasted_iota(jnp.int32, gt.shape, 0) & 7) < 6, pre, suf)
    gcols = jnp.concatenate([gam_all, jnp.zeros((LANES - 8 * cpt, LANES), F32)], axis=0).T

    def setup(c):
        kc = k_s[c * cc:(c + 1) * cc, :]
        qc = q_s[c * cc:(c + 1) * cc, :]
        vb = v_s[c * cc:(c + 1) * cc, :].astype(BF16)
        kb = kc.astype(BF16)
        pairs = []
        for p in range(PAIRS):
            lo = p * PAIR_W
            k0, k1 = kb[:, lo:lo + HEAD_W], kb[:, lo + HEAD_W:lo + PAIR_W]
            kbd = jnp.concatenate([jnp.concatenate([k0, zero_k], axis=1),
                                   jnp.concatenate([zero_k, k1], axis=1)], axis=0)
            vbd = jnp.concatenate([jnp.concatenate([vb[:, lo:lo + HEAD_W], zero_k], axis=1),
                                   jnp.concatenate([zero_k, vb[:, lo + HEAD_W:lo + PAIR_W]], axis=1)], axis=0)
            kq = lax.dot_general(jnp.concatenate([kb[:, lo:lo + PAIR_W], qc[:, lo:lo + PAIR_W].astype(BF16)], axis=0),
                                 kbd, (((1,), (1,)), ((), ())), preferred_element_type=F32)
            kt = jnp.concatenate([kc[:, lo:lo + HEAD_W], kc[:, lo + HEAD_W:lo + PAIR_W]], axis=0).T
            pairs.append((kbd, vbd, kq[:cc], kq[cc:], kt, qc[:, lo:lo + HEAD_W], qc[:, lo + HEAD_W:lo + PAIR_W]))
        probs = []
        for d in range(2):
            incl = (ii >= jl) if d == 0 else (ii <= jl)
            strict = (ii > jl) if d == 0 else (ii < jl)
            for p in range(PAIRS):
                kbd, vbd, kkt, qk, kt, q0, q1 = pairs[p]
                rb = 8 * c + 2 * d + p
                rg = rb + 4
                grow = gam_all[rg:rg + 1, :]
                brow = beta_all[rb:rb + 1, :]
                gc0 = jnp.broadcast_to(gcols[:cc, rg:rg + 1], (cc, LANES))
                gc1 = jnp.broadcast_to(gcols[cc:, rg:rg + 1], (cc, LANES))
                dm = jnp.exp(jnp.where(incl, jnp.where(first_head, gc0, gc1) - grow, -1e30))
                ld = jnp.where(strict, kkt * dm, 0.0) * brow
                top = jnp.where(incl, qk * dm, 0.0) * brow
                probs.append(dict(c=c, d=d, p=p, ld=ld, top=top, grow=grow, brow=brow, gc0=gc0, gc1=gc1,
                                  kbd=kbd, vbd=vbd, kt=kt, q0=q0, q1=q1))
        return probs

    def emit(pr, x):
        c, d, p, grow, brow = pr["c"], pr["d"], pr["p"], pr["grow"], pr["brow"]
        wk = jnp.dot((x * jnp.exp(grow)).astype(BF16), pr["kbd"], preferred_element_type=F32)
        u_ref[d, c, :, p * PAIR_W:(p + 1) * PAIR_W] = jnp.dot(x.astype(BF16), pr["vbd"], preferred_element_type=F32)
        l1_ref[d, c, 2 * p] = jnp.concatenate([wk[:, :HEAD_W], pr["q0"] * jnp.exp(pr["gc0"])], axis=0).astype(BF16)
        l1_ref[d, c, 2 * p + 1] = jnp.concatenate([wk[:, HEAD_W:], pr["q1"] * jnp.exp(pr["gc1"])], axis=0).astype(BF16)
        last = cc - 1 if d == 0 else 0
        gl0 = grow[:, last:last + 1]
        gl1 = grow[:, cc + last:cc + last + 1]
        tail = jnp.exp(jnp.where(lane_row < cc, gl0, gl1) - grow) * brow
        l2_ref[d, c, p] = jnp.concatenate([pr["top"], pr["kt"] * tail], axis=0).astype(BF16)
        dec_ref[d, c, :, p * PAIR_W:(p + 1) * PAIR_W] = jnp.concatenate(
            [jnp.broadcast_to(jnp.exp(gl0), (1, HEAD_W)), jnp.broadcast_to(jnp.exp(gl1), (1, HEAD_W))], axis=1)

    probs = []
    for c in range(cpt):
        probs += setup(c)
    xs = _unit_tri_inverses([pr["ld"] for pr in probs], ii, jl)
    for pr, xi in zip(probs, xs):
        emit(pr, xi)


def _gdn_tile_chunks(t):
    return min(t // GDN_CHUNK, 8)


def _gdn_prep(pq, pab, conv_qkv, a_log, dt_bias):
    b, t, w3 = pq.shape
    nc = t // GDN_CHUNK
    cpt = _gdn_tile_chunks(t)
    tt = cpt * GDN_CHUNK
    nt = t // tt
    hb = tt // HALO
    def gate_rows(a):
        rows = jnp.repeat(a.reshape(2 * PAIRS, 2), GDN_CHUNK, axis=1)
        return jnp.tile(jnp.concatenate([jnp.zeros_like(rows), rows], axis=0), (cpt, 1))

    alog_t, dtb_t = gate_rows(a_log), gate_rows(dt_bias)
    const = lambda bi, j: (0, 0)
    tok = lambda bi, j: (bi, j, 0)
    return pl.pallas_call(
        functools.partial(_gdn_prep_kernel, cpt=cpt, nt=nt),
        grid=(b, nt),
        in_specs=[pl.BlockSpec((None, tt, w3), tok),
                  pl.BlockSpec((None, HALO, w3), lambda bi, j: (bi, jnp.maximum(j * hb - 1, 0), 0)),
                  pl.BlockSpec((None, HALO, w3), lambda bi, j: (bi, jnp.minimum((j + 1) * hb, t // HALO - 1), 0)),
                  pl.BlockSpec((None, tt, LANES), tok),
                  pl.BlockSpec((3, w3), const),
                  pl.BlockSpec((8 * cpt, LANES), const), pl.BlockSpec((8 * cpt, LANES), const)],
        out_specs=[pl.BlockSpec((None, 2, cpt, N_HEADS, 2 * GDN_CHUNK, HEAD_W), lambda bi, j: (bi, 0, j, 0, 0, 0)),
                   pl.BlockSpec((None, 2, cpt, PAIRS, 3 * GDN_CHUNK, HEAD_W), lambda bi, j: (bi, 0, j, 0, 0, 0)),
                   pl.BlockSpec((None, 2, cpt, GDN_CHUNK, GDN_W), lambda bi, j: (bi, 0, j, 0, 0)),
                   pl.BlockSpec((None, 2, cpt, 1, GDN_W), lambda bi, j: (bi, 0, j, 0, 0))],
        out_shape=[jax.ShapeDtypeStruct((b, 2, nc, N_HEADS, 2 * GDN_CHUNK, HEAD_W), BF16),
                   jax.ShapeDtypeStruct((b, 2, nc, PAIRS, 3 * GDN_CHUNK, HEAD_W), BF16),
                   jax.ShapeDtypeStruct((b, 2, nc, GDN_CHUNK, GDN_W), F32),
                   jax.ShapeDtypeStruct((b, 2, nc, 1, GDN_W), F32)],
        scratch_shapes=[pltpu.VMEM((tt, GDN_W), F32)] * 3,
        compiler_params=_cparams("parallel", "parallel"),
        name="gdn_prep",
    )(pq, pq, pq, pab, conv_qkv, alog_t, dtb_t)


SCAN_BATCH = 4
SCAN_CHUNKS = 4


def _gdn_scan_kernel(l1f, l1b, l2f, l2b, uf, ub, decf, decb, s0_ref, of_ref, ob_ref, sfin_ref, s_ref,
                     *, cpg, nj, nb):
    j = pl.program_id(1)

    @pl.when(j == 0)
    def _():
        s_ref[...] = s0_ref[...]

    zero = jnp.zeros((GDN_CHUNK, HEAD_W), BF16)

    def step(i, carry):
        chains = []
        for n in range(nb):
            chains.append((n, 0, l1f, l2f, uf, decf, of_ref, i))
            chains.append((n, 1, l1b, l2b, ub, decb, ob_ref, cpg - 1 - i))
        first = []
        for n, d, l1, l2, u, dec, o_ref, c in chains:
            vts, qss = [], []
            for h in range(N_HEADS):
                sh = s_ref[n, d, :, h * HEAD_W:(h + 1) * HEAD_W].astype(BF16)
                r1 = jnp.dot(l1[n, c, h], sh, preferred_element_type=F32)
                vts.append((u[n, c, :, h * HEAD_W:(h + 1) * HEAD_W] - r1[:GDN_CHUNK]).astype(BF16))
                qss.append(r1[GDN_CHUNK:])
            first.append((vts, qss))
        for (n, d, l1, l2, u, dec, o_ref, c), (vts, qss) in zip(chains, first):
            intra = []
            for p in range(N_HEADS // 2):
                rhs = jnp.concatenate([jnp.concatenate([vts[2 * p], zero], axis=-1),
                                       jnp.concatenate([zero, vts[2 * p + 1]], axis=-1)], axis=0)
                r2 = jnp.dot(l2[n, c, p], rhs, preferred_element_type=F32)
                intra.append(r2[:GDN_CHUNK])
                lo, hi = 2 * p * HEAD_W, 2 * (p + 1) * HEAD_W
                s_ref[n, d, :, lo:hi] = s_ref[n, d, :, lo:hi] * dec[n, c][:, lo:hi] + r2[GDN_CHUNK:]
            o = jnp.concatenate(qss, axis=-1) + jnp.concatenate(intra, axis=-1)
            o_ref[n, pl.ds(pl.multiple_of(c * GDN_CHUNK, GDN_CHUNK), GDN_CHUNK), :] = o.astype(o_ref.dtype)
        return carry

    lax.fori_loop(0, cpg, step, 0)

    @pl.when(j == nj - 1)
    def _():
        sfin_ref[...] = s_ref[...]


def _gdn_scan(l1, l2, u, dec, s0):
    b, _, nc = l1.shape[:3]
    cpg = min(nc, SCAN_CHUNKS)
    nj = nc // cpg
    t = nc * GDN_CHUNK
    fwd5 = lambda bi, j: (bi, 0, j, 0, 0, 0)
    bwd5 = lambda bi, j: (bi, 1, nj - 1 - j, 0, 0, 0)
    fwd4 = lambda bi, j: (bi, 0, j, 0, 0)
    bwd4 = lambda bi, j: (bi, 1, nj - 1 - j, 0, 0)
    nb = SCAN_BATCH if b % SCAN_BATCH == 0 else 1
    b1 = (nb, None, cpg, N_HEADS, 2 * GDN_CHUNK, HEAD_W)
    b2 = (nb, None, cpg, N_HEADS // 2, 3 * GDN_CHUNK, HEAD_W)
    bu = (nb, None, cpg, GDN_CHUNK, GDN_W)
    bd = (nb, None, cpg, 1, GDN_W)
    st = pl.BlockSpec((nb, 2, HEAD_W, GDN_W), lambda bi, j: (bi, 0, 0, 0))
    return pl.pallas_call(
        functools.partial(_gdn_scan_kernel, cpg=cpg, nj=nj, nb=nb),
        grid=(b // nb, nj),
        in_specs=[pl.BlockSpec(b1, fwd5), pl.BlockSpec(b1, bwd5), pl.BlockSpec(b2, fwd5), pl.BlockSpec(b2, bwd5),
                  pl.BlockSpec(bu, fwd4), pl.BlockSpec(bu, bwd4), pl.BlockSpec(bd, fwd4), pl.BlockSpec(bd, bwd4), st],
        out_specs=[pl.BlockSpec((nb, cpg * GDN_CHUNK, GDN_W), lambda bi, j: (bi, j, 0)),
                   pl.BlockSpec((nb, cpg * GDN_CHUNK, GDN_W), lambda bi, j: (bi, nj - 1 - j, 0)), st],
        out_shape=[jax.ShapeDtypeStruct((b, t, GDN_W), BF16), jax.ShapeDtypeStruct((b, t, GDN_W), BF16),
                   jax.ShapeDtypeStruct((b, 2, HEAD_W, GDN_W), F32)],
        scratch_shapes=[pltpu.VMEM((nb, 2, HEAD_W, GDN_W), F32)],
        compiler_params=_cparams("parallel", "arbitrary"),
        name="gdn_scan",
    )(l1, l1, l2, l2, u, u, dec, dec, s0)


FFN_TILE = 256


def _post_kernel(x_ref, of_ref, ob_ref, z_ref, pa_ref, prev_ref, next_ref, pc_ref, mod_ref, cw_ref, lng_ref, lnb_ref,
                 ws_ref, bsm_ref, gon_ref, wo_ref, gqm_ref, gpf_ref, w1_ref, w2_ref, gqf_ref, o_ref,
                 *, nj, grid_conv, hidden):
    j = pl.program_id(1)
    pa = pa_ref[...].astype(F32)
    if grid_conv:
        ya = _conv_grid(pa, prev_ref[...].astype(F32), next_ref[...].astype(F32), cw_ref[...],
                        jnp.where(j > 0, 1.0, 0.0), jnp.where(j < nj - 1, 1.0, 0.0))
    else:
        ya = _conv_seq(pa, cw_ref[...])
    yc = _sgu(pc_ref[...].astype(F32), lng_ref[...], lnb_ref[...], ws_ref[...], bsm_ref[...])
    o = of_ref[...].astype(F32) + ob_ref[...].astype(F32)
    z = z_ref[...].astype(F32)
    ybs = []
    for h in range(N_HEADS):
        oh = o[:, h * HEAD_W:(h + 1) * HEAD_W]
        oh = oh * lax.rsqrt(jnp.mean(oh * oh, axis=-1, keepdims=True) + EPS) * gon_ref[...]
        ybs.append((oh * _silu(z[:, h * HEAD_W:(h + 1) * HEAD_W])).astype(BF16))
    y = jnp.concatenate([ya.astype(BF16)] + ybs + [yc.astype(BF16)], axis=-1)
    r = jnp.dot(y, wo_ref[...], preferred_element_type=F32)
    x = x_ref[...] + mod_ref[2:3, :] * _rms(r, gqm_ref[...])
    hb = (_rms(x, gpf_ref[...]) * (1.0 + mod_ref[4:5, :]) + mod_ref[3:4, :]).astype(BF16)
    acc = None
    for c0 in range(0, hidden, FFN_TILE):
        g = jnp.dot(hb, w1_ref[:, c0:c0 + FFN_TILE], preferred_element_type=F32)
        u = jnp.dot(hb, w1_ref[:, hidden + c0:hidden + c0 + FFN_TILE], preferred_element_type=F32)
        part = jnp.dot((_silu(g) * u).astype(BF16), w2_ref[c0:c0 + FFN_TILE, :], preferred_element_type=F32)
        acc = part if acc is None else acc + part
    o_ref[...] = x + mod_ref[5:6, :] * _rms(acc, gqf_ref[...])


def _post(x, of, ob, z, pa, pc, mod, conv_a, ln_g, ln_b, ws, bsm, g_onorm, w_o, gqm, gpf, w1, w2, gqf, *, grid_conv):
    b, t, d = x.shape
    hidden = w2.shape[0]
    tm = _row_tile(t)
    nj = t // tm
    assert grid_conv or nj == 1
    rows = tm // GRID_W
    last = t // GRID_W - 1
    const = lambda bi, j: (0, 0)
    tok = lambda bi, j: (bi, j, 0)
    full = lambda a: pl.BlockSpec(a.shape, const)
    resident = lambda a: pl.BlockSpec(a.shape, const, pipeline_mode=pl.Buffered(1))
    return pl.pallas_call(
        functools.partial(_post_kernel, nj=nj, grid_conv=grid_conv, hidden=hidden),
        grid=(b, nj),
        in_specs=[pl.BlockSpec((None, tm, d), tok),
                  pl.BlockSpec((None, tm, GDN_W), tok), pl.BlockSpec((None, tm, GDN_W), tok),
                  pl.BlockSpec((None, tm, GDN_W), tok),
                  pl.BlockSpec((None, tm, 3 * A_W), tok),
                  pl.BlockSpec((None, GRID_W, 3 * A_W), lambda bi, j: (bi, jnp.maximum(j * rows - 1, 0), 0)),
                  pl.BlockSpec((None, GRID_W, 3 * A_W), lambda bi, j: (bi, jnp.minimum((j + 1) * rows, last), 0)),
                  pl.BlockSpec((None, tm, 2 * C_W), tok),
                  pl.BlockSpec((None, 8, d), lambda bi, j: (bi, 0, 0)),
                  full(conv_a), full(ln_g), full(ln_b), full(ws), full(bsm), full(g_onorm), resident(w_o),
                  full(gqm), full(gpf), resident(w1), resident(w2), full(gqf)],
        out_specs=pl.BlockSpec((None, tm, d), tok),
        out_shape=jax.ShapeDtypeStruct((b, t, d), F32),
        compiler_params=_cparams("parallel", "parallel"),
        name="post_grid" if grid_conv else "post_seq",
    )(x, of, ob, z, pa, pa, pa, pc, mod, conv_a, ln_g, ln_b, ws, bsm, g_onorm, w_o, gqm, gpf, w1, w2, gqf)


OFF_A, OFF_Q, OFF_AB, OFF_Z, OFF_C = 0, 3 * A_W, 3 * A_W + 3 * GDN_W, 3 * A_W + 3 * GDN_W + 16, 3 * A_W + 4 * GDN_W + 16
GATE_PERM = [4 * ((n % 8) // 2) + 2 * (n % 2) + n // 8 for n in range(16)]


def _mod_rows(rows, b, d):
    m = jnp.broadcast_to(rows.reshape(-1, 6, d), (b, 6, d))
    return jnp.pad(m, ((0, 0), (0, 2), (0, 0)))


def kernel(x, c, ctx, c_ctx, w_mod, b_mod, g_pre_mix, g_post_mix, g_pre_ffn, g_post_ffn, w_in, conv_a, conv_qkv,
           a_log, dt_bias, g_onorm, ln_c_g, ln_c_b, w_s, b_s, w_o, w_ffn_in, w_ffn_out):
    b, t, d = x.shape
    depth = w_mod.shape[0]
    cc = jnp.concatenate([c, c_ctx[None, :], jnp.zeros((16 - b - 1, d), F32)], axis=0)
    mod = _modulation(cc, w_mod, b_mod)
    s_zero = jnp.zeros((b, 2, HEAD_W, GDN_W), F32)
    for l in range(depth):
        last = l == depth - 1
        modx = _mod_rows(mod[l, :b], b, d)
        modc = _mod_rows(mod[l, b:b + 1], b, d)
        wl = w_in[l].astype(BF16)
        w_gate = wl[:, OFF_AB:OFF_Z][:, jnp.array(GATE_PERM)]
        ws_in = [wl[:, OFF_Q:OFF_AB], wl[:, OFF_Z:OFF_C], wl[:, OFF_A:OFF_Q], wl[:, OFF_C:],
                 jnp.pad(w_gate, ((0, 0), (0, LANES - 16)))]
        gpm, gqm = g_pre_mix[l][None, :], g_post_mix[l][None, :]
        gpf, gqf = g_pre_ffn[l][None, :], g_post_ffn[l][None, :]
        sgu_w = w_s[l].reshape(C_GROUPS * C_CHUNK, C_CHUNK).astype(BF16)
        sgu_b = jnp.repeat(b_s[l].T, C_GD, axis=1)
        lng, lnb = ln_c_g[l][None, :], ln_c_b[l][None, :]
        wo = w_o[l].astype(BF16)
        w1, w2 = w_ffn_in[l].astype(BF16), w_ffn_out[l].astype(BF16)
        gon = g_onorm[l][None, :]

        def mixer_inputs(stream, m):
            pq, pz, pa, pc, pab = _in_proj(stream, m, gpm, ws_in)
            return pq, pz, pa, pc, _gdn_prep(pq, pab, conv_qkv[l], a_log[l], dt_bias[l])

        def finish(stream, m, of, ob, pz, pa, pc, grid_conv):
            return _post(stream, of, ob, pz, pa, pc, m, conv_a[l], lng, lnb, sgu_w, sgu_b, gon, wo, gqm, gpf, w1, w2,
                         gqf, grid_conv=grid_conv)

        pq, pz, pa, pc, ops = mixer_inputs(ctx, modc)
        of, ob, s_ctx = _gdn_scan(*ops, s_zero)
        if not last:
            ctx = finish(ctx, modc, of, ob, pz, pa, pc, False)
        pq, pz, pa, pc, ops = mixer_inputs(x, modx)
        of, ob, _ = _gdn_scan(*ops, s_ctx)
        x = finish(x, modx, of, ob, pz, pa, pc, True)
    return x
```

```python
import functools

import jax
import jax.numpy as jnp
from jax import lax
from jax.experimental import pallas as pl
from jax.experimental.pallas import tpu as pltpu

F32 = jnp.float32
BF16 = jnp.bfloat16
HIGHEST = lax.Precision.HIGHEST

EPS = 1e-6
GRID_W = 64
N_HEADS = 4
HEAD_W = 128
GDN_W = N_HEADS * HEAD_W
GDN_CHUNK = 64
A_W = 256
A_HORIZ = 128
C_W = 256
C_GROUPS = 4
C_GD = C_W // C_GROUPS
C_CHUNK = 128
LANES = 128
VMEM_LIMIT = 56 * 1024 * 1024


def _cparams(*sem):
    return pltpu.CompilerParams(dimension_semantics=sem, vmem_limit_bytes=VMEM_LIMIT)


def _hdot(a, b):
    return jnp.dot(a, b, precision=HIGHEST, preferred_element_type=F32)


def _rms(x, g):
    return x * lax.rsqrt(jnp.mean(x * x, axis=-1, keepdims=True) + EPS) * g


def _silu(x):
    return x * jax.nn.sigmoid(x)


def _gelu(x):
    return 0.5 * x * (1.0 + lax.erf(x * 0.7071067811865476))


def _softplus(x):
    return jnp.maximum(x, 0.0) + jnp.log1p(jnp.exp(-jnp.abs(x)))


def _mod_kernel(c_ref, w_ref, b_ref, o_ref):
    o_ref[...] = _hdot(_silu(c_ref[...]), w_ref[...]) + b_ref[...]


def _modulation(cc, w_mod, b_mod):
    depth, d, six_d = w_mod.shape
    r = cc.shape[0]
    return pl.pallas_call(
        _mod_kernel,
        grid=(depth, six_d // d),
        in_specs=[pl.BlockSpec((r, d), lambda l, j: (0, 0)),
                  pl.BlockSpec((None, d, d), lambda l, j: (l, 0, j)),
                  pl.BlockSpec((None, 1, d), lambda l, j: (l, 0, j))],
        out_specs=pl.BlockSpec((None, r, d), lambda l, j: (l, 0, j)),
        out_shape=jax.ShapeDtypeStruct((depth, r, six_d), F32),
        compiler_params=_cparams("parallel", "parallel"),
        name="modulation",
    )(cc, w_mod, b_mod.reshape(depth, 1, six_d))


HALO = 16


def _in_kernel(x_ref, xp_ref, xn_ref, mod_ref, g_ref, wq, wz, wa, wc, wab, cw_ref, oq, oz, oa, oc, oab, *, nj):
    j = pl.program_id(1)
    tm = x_ref.shape[0]

    def normed(v):
        return _rms(v, g_ref[...]) * (1.0 + mod_ref[1:2, :]) + mod_ref[0:1, :]

    hb = jnp.concatenate([normed(xp_ref[...]), normed(x_ref[...]), normed(xn_ref[...])], axis=0).astype(BF16)
    p = jnp.dot(hb, wq[...], preferred_element_type=F32)
    p = jnp.concatenate([p[:HALO] * jnp.where(j > 0, 1.0, 0.0), p[HALO:HALO + tm],
                         p[HALO + tm:] * jnp.where(j < nj - 1, 1.0, 0.0)], axis=0)
    cw = cw_ref[...]
    rows = tm + 2 * HALO
    s = _silu(cw[0:1] * pltpu.roll(p, 1, 0)[HALO:HALO + tm] + cw[1:2] * p[HALO:HALO + tm]
              + cw[2:3] * pltpu.roll(p, rows - 1, 0)[HALO:HALO + tm])
    for h in range(N_HEADS):
        lo, hi = h * HEAD_W, (h + 1) * HEAD_W
        qh = s[:, lo:hi]
        oq[:, lo:hi] = (qh * (lax.rsqrt(jnp.sum(qh * qh, axis=-1, keepdims=True) + EPS) * HEAD_W ** -0.5)).astype(BF16)
        kh = s[:, GDN_W + lo:GDN_W + hi]
        oq[:, GDN_W + lo:GDN_W + hi] = (kh * lax.rsqrt(jnp.sum(kh * kh, axis=-1, keepdims=True) + EPS)).astype(BF16)
    oq[:, 2 * GDN_W:] = s[:, 2 * GDN_W:].astype(BF16)
    hm = hb[HALO:HALO + tm]
    for w, o in ((wz, oz), (wa, oa), (wc, oc)):
        o[...] = jnp.dot(hm, w[...], preferred_element_type=F32).astype(BF16)
    oab[...] = jnp.dot(hm, wab[...], preferred_element_type=F32)


def _row_tile(t, rows=512):
    return min(t, rows)


def _in_proj(x, mod, g_pre, ws, conv_qkv):
    b, t, d = x.shape
    tm = _row_tile(t)
    nj = t // tm
    hb = tm // HALO
    widths = [w.shape[1] for w in ws]
    const = lambda bi, j: (0, 0)
    tok = lambda bi, j: (bi, j, 0)
    return pl.pallas_call(
        functools.partial(_in_kernel, nj=nj),
        grid=(b, nj),
        in_specs=[pl.BlockSpec((None, tm, d), tok),
                  pl.BlockSpec((None, HALO, d), lambda bi, j: (bi, jnp.maximum(j * hb - 1, 0), 0)),
                  pl.BlockSpec((None, HALO, d), lambda bi, j: (bi, jnp.minimum((j + 1) * hb, t // HALO - 1), 0)),
                  pl.BlockSpec((None, 8, d), lambda bi, j: (bi, 0, 0)),
                  pl.BlockSpec((1, d), const)]
                 + [pl.BlockSpec((d, n), const, pipeline_mode=pl.Buffered(1)) for n in widths]
                 + [pl.BlockSpec(conv_qkv.shape, const)],
        out_specs=[pl.BlockSpec((None, tm, n), tok) for n in widths],
        out_shape=[jax.ShapeDtypeStruct((b, t, n), BF16) for n in widths[:-1]]
                  + [jax.ShapeDtypeStruct((b, t, widths[-1]), F32)],
        compiler_params=_cparams("parallel", "parallel"),
        name="in_proj",
    )(x, x, x, mod, g_pre, *ws, conv_qkv)


def _sgu(pc, lng, lnb, ws, bsm):
    tt = pc.shape[0]
    u = _gelu(pc[:, :C_W])
    v = _gelu(pc[:, C_W:])
    mu = jnp.mean(v, axis=-1, keepdims=True)
    vc = v - mu
    var = jnp.mean(vc * vc, axis=-1, keepdims=True)
    vb = (vc * lax.rsqrt(var + EPS) * lng + lnb).astype(BF16)
    grp = lax.broadcasted_iota(jnp.int32, (C_CHUNK, C_W), 1) // C_GD
    outs = []
    for n in range(tt // C_CHUNK):
        r = jnp.dot(ws, vb[n * C_CHUNK:(n + 1) * C_CHUNK, :], preferred_element_type=F32)
        mixed = r[3 * C_CHUNK:]
        for g in (2, 1, 0):
            mixed = jnp.where(grp == g, r[g * C_CHUNK:(g + 1) * C_CHUNK], mixed)
        outs.append(mixed + bsm)
    return u * jnp.concatenate(outs, axis=0)


def _conv_grid(pa, pv, nv, cw, has_prev, has_next):
    tt = pa.shape[0]
    gate = pa[:, :A_W]
    m = pa[:, A_W:2 * A_W] * pa[:, 2 * A_W:]
    col = lax.broadcasted_iota(jnp.int32, (tt, A_HORIZ), 0) % GRID_W
    mh = m[:, :A_HORIZ]
    left = jnp.where(col == 0, 0.0, pltpu.roll(mh, 1, 0))
    right = jnp.where(col == GRID_W - 1, 0.0, pltpu.roll(mh, tt - 1, 0))
    yh = cw[0:1, :A_HORIZ] * left + cw[1:2, :A_HORIZ] * mh + cw[2:3, :A_HORIZ] * right
    mv = m[:, A_HORIZ:]
    pm = pv[:, A_W + A_HORIZ:2 * A_W] * pv[:, 2 * A_W + A_HORIZ:] * has_prev
    nm = nv[:, A_W + A_HORIZ:2 * A_W] * nv[:, 2 * A_W + A_HORIZ:] * has_next
    up = jnp.concatenate([pm, mv[:tt - GRID_W]], axis=0)
    down = jnp.concatenate([mv[GRID_W:], nm], axis=0)
    yv = cw[0:1, A_HORIZ:] * up + cw[1:2, A_HORIZ:] * mv + cw[2:3, A_HORIZ:] * down
    return gate * jnp.concatenate([yh, yv], axis=-1)


def _conv_seq(pa, cw):
    tt = pa.shape[0]
    m = pa[:, A_W:2 * A_W] * pa[:, 2 * A_W:]
    row = lax.broadcasted_iota(jnp.int32, (tt, A_W), 0)
    left = jnp.where(row == 0, 0.0, pltpu.roll(m, 1, 0))
    right = jnp.where(row == tt - 1, 0.0, pltpu.roll(m, tt - 1, 0))
    return pa[:, :A_W] * (cw[0:1] * left + cw[1:2] * m + cw[2:3] * right)


PAIRS = N_HEADS // 2
PAIR_W = 2 * HEAD_W
PREP_GROUP = 4


def _block_diag2(y):
    lane = lax.broadcasted_iota(jnp.int32, y.shape, 1)
    return jnp.concatenate([jnp.where(lane < GDN_CHUNK, y, 0.0), jnp.where(lane >= GDN_CHUNK, y, 0.0)],
                           axis=0).astype(BF16)


def _pdot(a, b):
    return jnp.dot(a.astype(BF16), _block_diag2(b), preferred_element_type=F32)


def _unit_tri_inverses(lds, ii, jl):
    xs = [jnp.where(ii == jl, 1.0, 0.0) - jnp.where((ii >> 1) == (jl >> 1), ld, 0.0) for ld in lds]
    for k in range(1, 6):
        mk = ((ii >> (k + 1)) == (jl >> (k + 1))) & ((ii >> k) != (jl >> k))
        ts = [_pdot(jnp.where(mk, ld, 0.0), x) for ld, x in zip(lds, xs)]
        xs = [x - _pdot(x, t) for x, t in zip(xs, ts)]
    return xs


def _gdn_prep_kernel(qkv_ref, ab_ref, alog_t, dtb_t, l1_ref, l2_ref, u_ref, dec_ref, *, cpt):

    cc = GDN_CHUNK
    ii = lax.broadcasted_iota(jnp.int32, (cc, LANES), 0)
    lane = lax.broadcasted_iota(jnp.int32, (cc, LANES), 1)
    jl = lane & (cc - 1)
    first_head = lane < cc
    lane_row = lax.broadcasted_iota(jnp.int32, (1, LANES), 1)
    zero_k = jnp.zeros((cc, HEAD_W), BF16)

    gts = []
    for c in range(cpt):
        abn = ab_ref[c * cc:(c + 1) * cc, :]
        gts.append(jnp.concatenate([abn, pltpu.roll(abn, LANES - 8, 1)], axis=0).T[0:8, :])
    gt = jnp.concatenate(gts, axis=0)
    beta_all = jax.nn.sigmoid(gt)
    g_all = -jnp.exp(alog_t[...]) * _softplus(gt + dtb_t[...])
    tok = lax.broadcasted_iota(jnp.int32, gt.shape, 1) & (cc - 1)
    pre = suf = g_all
    sh = 1
    while sh < cc:
        pre = pre + jnp.where(tok >= sh, pltpu.roll(pre, sh, 1), 0.0)
        suf = suf + jnp.where(tok < cc - sh, pltpu.roll(suf, LANES - sh, 1), 0.0)
        sh *= 2
    gam_all = jnp.where((lax.broadcasted_iota(jnp.int32, gt.shape, 0) & 7) < 6, pre, suf)
    gcols = jnp.concatenate([gam_all, jnp.zeros((LANES - 8 * cpt, LANES), F32)], axis=0).T

    def setup(c):
        qb = qkv_ref[c * cc:(c + 1) * cc, 0:GDN_W]
        kb = qkv_ref[c * cc:(c + 1) * cc, GDN_W:2 * GDN_W]
        vb = qkv_ref[c * cc:(c + 1) * cc, 2 * GDN_W:]
        qc = qb.astype(F32)
        kc = kb.astype(F32)
        pairs = []
        for p in range(PAIRS):
            lo = p * PAIR_W
            k0, k1 = kb[:, lo:lo + HEAD_W], kb[:, lo + HEAD_W:lo + PAIR_W]
            kbd = jnp.concatenate([jnp.concatenate([k0, zero_k], axis=1),
                                   jnp.concatenate([zero_k, k1], axis=1)], axis=0)
            vbd = jnp.concatenate([jnp.concatenate([vb[:, lo:lo + HEAD_W], zero_k], axis=1),
                                   jnp.concatenate([zero_k, vb[:, lo + HEAD_W:lo + PAIR_W]], axis=1)], axis=0)
            kq = lax.dot_general(jnp.concatenate([kb[:, lo:lo + PAIR_W], qb[:, lo:lo + PAIR_W]], axis=0),
                                 kbd, (((1,), (1,)), ((), ())), preferred_element_type=F32)
            kt = jnp.concatenate([kc[:, lo:lo + HEAD_W], kc[:, lo + HEAD_W:lo + PAIR_W]], axis=0).T
            pairs.append((kbd, vbd, kq[:cc], kq[cc:], kt, qc[:, lo:lo + HEAD_W], qc[:, lo + HEAD_W:lo + PAIR_W]))
        probs = []
        for d in range(2):
            incl = (ii >= jl) if d == 0 else (ii <= jl)
            strict = (ii > jl) if d == 0 else (ii < jl)
            for p in range(PAIRS):
                kbd, vbd, kkt, qk, kt, q0, q1 = pairs[p]
                rb = 8 * c + 2 * d + p
                rg = rb + 4
                grow = gam_all[rg:rg + 1, :]
                brow = beta_all[rb:rb + 1, :]
                gc0 = jnp.broadcast_to(gcols[:cc, rg:rg + 1], (cc, LANES))
                gc1 = jnp.broadcast_to(gcols[cc:, rg:rg + 1], (cc, LANES))
                dm = jnp.exp(jnp.where(incl, jnp.where(first_head, gc0, gc1) - grow, -1e30))
                ld = jnp.where(strict, kkt * dm, 0.0) * brow
                top = jnp.where(incl, qk * dm, 0.0) * brow
                probs.append(dict(c=c, d=d, p=p, ld=ld, top=top, grow=grow, brow=brow, gc0=gc0, gc1=gc1,
                                  kbd=kbd, vbd=vbd, kt=kt, q0=q0, q1=q1))
        return probs

    def emit(pr, x):
        c, d, p, grow, brow = pr["c"], pr["d"], pr["p"], pr["grow"], pr["brow"]
        wk = jnp.dot((x * jnp.exp(grow)).astype(BF16), pr["kbd"], preferred_element_type=F32)
        u_ref[d, c, :, p * PAIR_W:(p + 1) * PAIR_W] = jnp.dot(x.astype(BF16), pr["vbd"], preferred_element_type=F32)
        l1_ref[d, c, 2 * p] = jnp.concatenate([wk[:, :HEAD_W], pr["q0"] * jnp.exp(pr["gc0"])], axis=0).astype(BF16)
        l1_ref[d, c, 2 * p + 1] = jnp.concatenate([wk[:, HEAD_W:], pr["q1"] * jnp.exp(pr["gc1"])], axis=0).astype(BF16)
        last = cc - 1 if d == 0 else 0
        gl0 = grow[:, last:last + 1]
        gl1 = grow[:, cc + last:cc + last + 1]
        tail = jnp.exp(jnp.where(lane_row < cc, gl0, gl1) - grow) * brow
        l2_ref[d, c, p] = jnp.concatenate([pr["top"], pr["kt"] * tail], axis=0).astype(BF16)
        dec_ref[d, c, :, p * PAIR_W:(p + 1) * PAIR_W] = jnp.concatenate(
            [jnp.broadcast_to(jnp.exp(gl0), (1, HEAD_W)), jnp.broadcast_to(jnp.exp(gl1), (1, HEAD_W))], axis=1)

    group = min(cpt, PREP_GROUP)
    for c0 in range(0, cpt, group):
        probs = []
        for c in range(c0, c0 + group):
            probs += setup(c)
        xs = _unit_tri_inverses([pr["ld"] for pr in probs], ii, jl)
        for pr, xi in zip(probs, xs):
            emit(pr, xi)


def _gdn_tile_chunks(t):
    return min(t // GDN_CHUNK, 8)


def _gdn_prep(qkv, pab, a_log, dt_bias):
    b, t, w3 = qkv.shape
    nc = t // GDN_CHUNK
    cpt = _gdn_tile_chunks(t)
    tt = cpt * GDN_CHUNK
    nt = t // tt
    def gate_rows(a):
        rows = jnp.repeat(a.reshape(2 * PAIRS, 2), GDN_CHUNK, axis=1)
        return jnp.tile(jnp.concatenate([jnp.zeros_like(rows), rows], axis=0), (cpt, 1))

    alog_t, dtb_t = gate_rows(a_log), gate_rows(dt_bias)
    const = lambda bi, j: (0, 0)
    tok = lambda bi, j: (bi, j, 0)
    return pl.pallas_call(
        functools.partial(_gdn_prep_kernel, cpt=cpt),
        grid=(b, nt),
        in_specs=[pl.BlockSpec((None, tt, w3), tok),
                  pl.BlockSpec((None, tt, LANES), tok),
                  pl.BlockSpec((8 * cpt, LANES), const), pl.BlockSpec((8 * cpt, LANES), const)],
        out_specs=[pl.BlockSpec((None, 2, cpt, N_HEADS, 2 * GDN_CHUNK, HEAD_W), lambda bi, j: (bi, 0, j, 0, 0, 0)),
                   pl.BlockSpec((None, 2, cpt, PAIRS, 3 * GDN_CHUNK, HEAD_W), lambda bi, j: (bi, 0, j, 0, 0, 0)),
                   pl.BlockSpec((None, 2, cpt, GDN_CHUNK, GDN_W), lambda bi, j: (bi, 0, j, 0, 0)),
                   pl.BlockSpec((None, 2, cpt, 1, GDN_W), lambda bi, j: (bi, 0, j, 0, 0))],
        out_shape=[jax.ShapeDtypeStruct((b, 2, nc, N_HEADS, 2 * GDN_CHUNK, HEAD_W), BF16),
                   jax.ShapeDtypeStruct((b, 2, nc, PAIRS, 3 * GDN_CHUNK, HEAD_W), BF16),
                   jax.ShapeDtypeStruct((b, 2, nc, GDN_CHUNK, GDN_W), F32),
                   jax.ShapeDtypeStruct((b, 2, nc, 1, GDN_W), F32)],
        compiler_params=_cparams("parallel", "parallel"),
        name="gdn_prep",
    )(qkv, pab, alog_t, dtb_t)


SCAN_BATCH = 4
SCAN_CHUNKS = 4


def _gdn_scan_kernel(l1f, l1b, l2f, l2b, uf, ub, decf, decb, s0_ref, of_ref, ob_ref, sfin_ref, s_ref,
                     *, cpg, nj, nb):
    j = pl.program_id(1)

    @pl.when(j == 0)
    def _():
        s_ref[...] = s0_ref[...]

    zero = jnp.zeros((GDN_CHUNK, HEAD_W), BF16)

    def step(i, carry):
        chains = []
        for n in range(nb):
            chains.append((n, 0, l1f, l2f, uf, decf, of_ref, i))
            chains.append((n, 1, l1b, l2b, ub, decb, ob_ref, cpg - 1 - i))
        first = []
        for n, d, l1, l2, u, dec, o_ref, c in chains:
            vts, qss = [], []
            for h in range(N_HEADS):
                sh = s_ref[n, d, :, h * HEAD_W:(h + 1) * HEAD_W].astype(BF16)
                r1 = jnp.dot(l1[n, c, h], sh, preferred_element_type=F32)
                vts.append((u[n, c, :, h * HEAD_W:(h + 1) * HEAD_W] - r1[:GDN_CHUNK]).astype(BF16))
                qss.append(r1[GDN_CHUNK:])
            first.append((vts, qss))
        for (n, d, l1, l2, u, dec, o_ref, c), (vts, qss) in zip(chains, first):
            intra = []
            for p in range(N_HEADS // 2):
                rhs = jnp.concatenate([jnp.concatenate([vts[2 * p], zero], axis=-1),
                                       jnp.concatenate([zero, vts[2 * p + 1]], axis=-1)], axis=0)
                r2 = jnp.dot(l2[n, c, p], rhs, preferred_element_type=F32)
                intra.append(r2[:GDN_CHUNK])
                lo, hi = 2 * p * HEAD_W, 2 * (p + 1) * HEAD_W
                s_ref[n, d, :, lo:hi] = s_ref[n, d, :, lo:hi] * dec[n, c][:, lo:hi] + r2[GDN_CHUNK:]
            o = jnp.concatenate(qss, axis=-1) + jnp.concatenate(intra, axis=-1)
            o_ref[n, pl.ds(pl.multiple_of(c * GDN_CHUNK, GDN_CHUNK), GDN_CHUNK), :] = o.astype(o_ref.dtype)
        return carry

    lax.fori_loop(0, cpg, step, 0)

    @pl.when(j == nj - 1)
    def _():
        sfin_ref[...] = s_ref[...]


def _gdn_scan(l1, l2, u, dec, s0):
    b, _, nc = l1.shape[:3]
    cpg = min(nc, SCAN_CHUNKS)
    nj = nc // cpg
    t = nc * GDN_CHUNK
    fwd5 = lambda bi, j: (bi, 0, j, 0, 0, 0)
    bwd5 = lambda bi, j: (bi, 1, nj - 1 - j, 0, 0, 0)
    fwd4 = lambda bi, j: (bi, 0, j, 0, 0)
    bwd4 = lambda bi, j: (bi, 1, nj - 1 - j, 0, 0)
    nb = SCAN_BATCH if b % SCAN_BATCH == 0 else 1
    b1 = (nb, None, cpg, N_HEADS, 2 * GDN_CHUNK, HEAD_W)
    b2 = (nb, None, cpg, N_HEADS // 2, 3 * GDN_CHUNK, HEAD_W)
    bu = (nb, None, cpg, GDN_CHUNK, GDN_W)
    bd = (nb, None, cpg, 1, GDN_W)
    st = pl.BlockSpec((nb, 2, HEAD_W, GDN_W), lambda bi, j: (bi, 0, 0, 0))
    return pl.pallas_call(
        functools.partial(_gdn_scan_kernel, cpg=cpg, nj=nj, nb=nb),
        grid=(b // nb, nj),
        in_specs=[pl.BlockSpec(b1, fwd5), pl.BlockSpec(b1, bwd5), pl.BlockSpec(b2, fwd5), pl.BlockSpec(b2, bwd5),
                  pl.BlockSpec(bu, fwd4), pl.BlockSpec(bu, bwd4), pl.BlockSpec(bd, fwd4), pl.BlockSpec(bd, bwd4), st],
        out_specs=[pl.BlockSpec((nb, cpg * GDN_CHUNK, GDN_W), lambda bi, j: (bi, j, 0)),
                   pl.BlockSpec((nb, cpg * GDN_CHUNK, GDN_W), lambda bi, j: (bi, nj - 1 - j, 0)), st],
        out_shape=[jax.ShapeDtypeStruct((b, t, GDN_W), BF16), jax.ShapeDtypeStruct((b, t, GDN_W), BF16),
                   jax.ShapeDtypeStruct((b, 2, HEAD_W, GDN_W), F32)],
        scratch_shapes=[pltpu.VMEM((nb, 2, HEAD_W, GDN_W), F32)],
        compiler_params=_cparams("parallel", "arbitrary"),
        name="gdn_scan",
    )(l1, l1, l2, l2, u, u, dec, dec, s0)


FFN_TILE = 256


def _post_kernel(x_ref, of_ref, ob_ref, z_ref, pa_ref, prev_ref, next_ref, pc_ref, mod_ref, cw_ref, lng_ref, lnb_ref,
                 ws_ref, bsm_ref, gon_ref, wo_ref, gqm_ref, gpf_ref, w1_ref, w2_ref, gqf_ref, o_ref,
                 *, nj, grid_conv, hidden):
    j = pl.program_id(1)
    pa = pa_ref[...].astype(F32)
    if grid_conv:
        ya = _conv_grid(pa, prev_ref[...].astype(F32), next_ref[...].astype(F32), cw_ref[...],
                        jnp.where(j > 0, 1.0, 0.0), jnp.where(j < nj - 1, 1.0, 0.0))
    else:
        ya = _conv_seq(pa, cw_ref[...])
    yc = _sgu(pc_ref[...].astype(F32), lng_ref[...], lnb_ref[...], ws_ref[...], bsm_ref[...])
    o = of_ref[...].astype(F32) + ob_ref[...].astype(F32)
    z = z_ref[...].astype(F32)
    ybs = []
    for h in range(N_HEADS):
        oh = o[:, h * HEAD_W:(h + 1) * HEAD_W]
        oh = oh * lax.rsqrt(jnp.mean(oh * oh, axis=-1, keepdims=True) + EPS) * gon_ref[...]
        ybs.append((oh * _silu(z[:, h * HEAD_W:(h + 1) * HEAD_W])).astype(BF16))
    y = jnp.concatenate([ya.astype(BF16)] + ybs + [yc.astype(BF16)], axis=-1)
    r = jnp.dot(y, wo_ref[...], preferred_element_type=F32)
    x = x_ref[...] + mod_ref[2:3, :] * _rms(r, gqm_ref[...])
    hb = (_rms(x, gpf_ref[...]) * (1.0 + mod_ref[4:5, :]) + mod_ref[3:4, :]).astype(BF16)
    acc = None
    for c0 in range(0, hidden, FFN_TILE):
        g = jnp.dot(hb, w1_ref[:, c0:c0 + FFN_TILE], preferred_element_type=F32)
        u = jnp.dot(hb, w1_ref[:, hidden + c0:hidden + c0 + FFN_TILE], preferred_element_type=F32)
        part = jnp.dot((_silu(g) * u).astype(BF16), w2_ref[c0:c0 + FFN_TILE, :], preferred_element_type=F32)
        acc = part if acc is None else acc + part
    o_ref[...] = x + mod_ref[5:6, :] * _rms(acc, gqf_ref[...])


def _post(x, of, ob, z, pa, pc, mod, conv_a, ln_g, ln_b, ws, bsm, g_onorm, w_o, gqm, gpf, w1, w2, gqf, *, grid_conv):
    b, t, d = x.shape
    hidden = w2.shape[0]
    tm = _row_tile(t)
    nj = t // tm
    assert grid_conv or nj == 1
    rows = tm // GRID_W
    last = t // GRID_W - 1
    const = lambda bi, j: (0, 0)
    tok = lambda bi, j: (bi, j, 0)
    full = lambda a: pl.BlockSpec(a.shape, const)
    resident = lambda a: pl.BlockSpec(a.shape, const, pipeline_mode=pl.Buffered(1))
    return pl.pallas_call(
        functools.partial(_post_kernel, nj=nj, grid_conv=grid_conv, hidden=hidden),
        grid=(b, nj),
        in_specs=[pl.BlockSpec((None, tm, d), tok),
                  pl.BlockSpec((None, tm, GDN_W), tok), pl.BlockSpec((None, tm, GDN_W), tok),
                  pl.BlockSpec((None, tm, GDN_W), tok),
                  pl.BlockSpec((None, tm, 3 * A_W), tok),
                  pl.BlockSpec((None, GRID_W, 3 * A_W), lambda bi, j: (bi, jnp.maximum(j * rows - 1, 0), 0)),
                  pl.BlockSpec((None, GRID_W, 3 * A_W), lambda bi, j: (bi, jnp.minimum((j + 1) * rows, last), 0)),
                  pl.BlockSpec((None, tm, 2 * C_W), tok),
                  pl.BlockSpec((None, 8, d), lambda bi, j: (bi, 0, 0)),
                  full(conv_a), full(ln_g), full(ln_b), full(ws), full(bsm), full(g_onorm), resident(w_o),
                  full(gqm), full(gpf), resident(w1), resident(w2), full(gqf)],
        out_specs=pl.BlockSpec((None, tm, d), tok),
        out_shape=jax.ShapeDtypeStruct((b, t, d), F32),
        compiler_params=_cparams("parallel", "parallel"),
        name="post_grid" if grid_conv else "post_seq",
    )(x, of, ob, z, pa, pa, pa, pc, mod, conv_a, ln_g, ln_b, ws, bsm, g_onorm, w_o, gqm, gpf, w1, w2, gqf)


OFF_A, OFF_Q, OFF_AB, OFF_Z, OFF_C = 0, 3 * A_W, 3 * A_W + 3 * GDN_W, 3 * A_W + 3 * GDN_W + 16, 3 * A_W + 4 * GDN_W + 16
GATE_PERM = [4 * ((n % 8) // 2) + 2 * (n % 2) + n // 8 for n in range(16)]


def _mod_rows(rows, b, d):
    m = jnp.broadcast_to(rows.reshape(-1, 6, d), (b, 6, d))
    return jnp.pad(m, ((0, 0), (0, 2), (0, 0)))


def kernel(x, c, ctx, c_ctx, w_mod, b_mod, g_pre_mix, g_post_mix, g_pre_ffn, g_post_ffn, w_in, conv_a, conv_qkv,
           a_log, dt_bias, g_onorm, ln_c_g, ln_c_b, w_s, b_s, w_o, w_ffn_in, w_ffn_out):
    b, t, d = x.shape
    depth = w_mod.shape[0]
    cc = jnp.concatenate([c, c_ctx[None, :], jnp.zeros((16 - b - 1, d), F32)], axis=0)
    mod = _modulation(cc, w_mod, b_mod)
    s_zero = jnp.zeros((b, 2, HEAD_W, GDN_W), F32)
    for l in range(depth):
        last = l == depth - 1
        modx = _mod_rows(mod[l, :b], b, d)
        modc = _mod_rows(mod[l, b:b + 1], b, d)
        wl = w_in[l].astype(BF16)
        w_gate = wl[:, OFF_AB:OFF_Z][:, jnp.array(GATE_PERM)]
        ws_in = [wl[:, OFF_Q:OFF_AB], wl[:, OFF_Z:OFF_C], wl[:, OFF_A:OFF_Q], wl[:, OFF_C:],
                 jnp.pad(w_gate, ((0, 0), (0, LANES - 16)))]
        gpm, gqm = g_pre_mix[l][None, :], g_post_mix[l][None, :]
        gpf, gqf = g_pre_ffn[l][None, :], g_post_ffn[l][None, :]
        sgu_w = w_s[l].reshape(C_GROUPS * C_CHUNK, C_CHUNK).astype(BF16)
        sgu_b = jnp.repeat(b_s[l].T, C_GD, axis=1)
        lng, lnb = ln_c_g[l][None, :], ln_c_b[l][None, :]
        wo = w_o[l].astype(BF16)
        w1, w2 = w_ffn_in[l].astype(BF16), w_ffn_out[l].astype(BF16)
        gon = g_onorm[l][None, :]

        def mixer_inputs(stream, m):
            qkv, pz, pa, pc, pab = _in_proj(stream, m, gpm, ws_in, conv_qkv[l])
            return qkv, pz, pa, pc, _gdn_prep(qkv, pab, a_log[l], dt_bias[l])

        def finish(stream, m, of, ob, pz, pa, pc, grid_conv):
            return _post(stream, of, ob, pz, pa, pc, m, conv_a[l], lng, lnb, sgu_w, sgu_b, gon, wo, gqm, gpf, w1, w2,
                         gqf, grid_conv=grid_conv)

        pq, pz, pa, pc, ops = mixer_inputs(ctx, modc)
        of, ob, s_ctx = _gdn_scan(*ops, s_zero)
        if not last:
            ctx = finish(ctx, modc, of, ob, pz, pa, pc, False)
        pq, pz, pa, pc, ops = mixer_inputs(x, modx)
        of, ob, _ = _gdn_scan(*ops, s_ctx)
        x = finish(x, modx, of, ob, pz, pa, pc, True)
    return x
```

```python
import functools

import jax
import jax.numpy as jnp
from jax import lax
from jax.experimental import pallas as pl
from jax.experimental.pallas import tpu as pltpu

F32 = jnp.float32
BF16 = jnp.bfloat16
HIGHEST = lax.Precision.HIGHEST

EPS = 1e-6
GRID_W = 64
N_HEADS = 4
HEAD_W = 128
GDN_W = N_HEADS * HEAD_W
GDN_CHUNK = 64
A_W = 256
A_HORIZ = 128
C_W = 256
C_GROUPS = 4
C_GD = C_W // C_GROUPS
C_CHUNK = 128
LANES = 128
VMEM_LIMIT = 56 * 1024 * 1024


def _cparams(*sem):
    return pltpu.CompilerParams(dimension_semantics=sem, vmem_limit_bytes=VMEM_LIMIT)


def _hdot(a, b):
    return jnp.dot(a, b, precision=HIGHEST, preferred_element_type=F32)


def _rms(x, g):
    return x * lax.rsqrt(jnp.mean(x * x, axis=-1, keepdims=True) + EPS) * g


def _silu(x):
    return x * jax.nn.sigmoid(x)


def _gelu(x):
    return 0.5 * x * (1.0 + lax.erf(x * 0.7071067811865476))


def _softplus(x):
    return jnp.maximum(x, 0.0) + jnp.log1p(jnp.exp(-jnp.abs(x)))


def _mod_kernel(c_ref, w_ref, b_ref, o_ref):
    o_ref[...] = _hdot(_silu(c_ref[...]), w_ref[...]) + b_ref[...]


def _modulation(cc, w_mod, b_mod):
    depth, d, six_d = w_mod.shape
    r = cc.shape[0]
    return pl.pallas_call(
        _mod_kernel,
        grid=(depth, six_d // d),
        in_specs=[pl.BlockSpec((r, d), lambda l, j: (0, 0)),
                  pl.BlockSpec((None, d, d), lambda l, j: (l, 0, j)),
                  pl.BlockSpec((None, 1, d), lambda l, j: (l, 0, j))],
        out_specs=pl.BlockSpec((None, r, d), lambda l, j: (l, 0, j)),
        out_shape=jax.ShapeDtypeStruct((depth, r, six_d), F32),
        compiler_params=_cparams("parallel", "parallel"),
        name="modulation",
    )(cc, w_mod, b_mod.reshape(depth, 1, six_d))


def _in_kernel(x_ref, mod_ref, g_ref, wq, wz, wa, wc, wab, oq, oz, oa, oc, oab):
    h = _rms(x_ref[...], g_ref[...] * (1.0 + mod_ref[1:2, :])) + mod_ref[0:1, :]
    hb = h.astype(BF16)
    for w, o in ((wq, oq), (wz, oz), (wa, oa), (wc, oc)):
        o[...] = jnp.dot(hb, w[...], preferred_element_type=F32).astype(BF16)
    oab[...] = jnp.dot(hb, wab[...], preferred_element_type=F32)


def _row_tile(t, rows=512):
    return min(t, rows)


def _in_proj(x, mod, g_pre, ws):
    b, t, d = x.shape
    tm = _row_tile(t, 1024)
    widths = [w.shape[1] for w in ws]
    const = lambda bi, j: (0, 0)
    tok = lambda bi, j: (bi, j, 0)
    return pl.pallas_call(
        _in_kernel,
        grid=(b, t // tm),
        in_specs=[pl.BlockSpec((None, tm, d), tok),
                  pl.BlockSpec((None, 8, d), lambda bi, j: (bi, 0, 0)),
                  pl.BlockSpec((1, d), const)]
                 + [pl.BlockSpec((d, n), const) for n in widths],
        out_specs=[pl.BlockSpec((None, tm, n), tok) for n in widths],
        out_shape=[jax.ShapeDtypeStruct((b, t, n), BF16) for n in widths[:-1]]
                  + [jax.ShapeDtypeStruct((b, t, widths[-1]), F32)],
        compiler_params=_cparams("parallel", "parallel"),
        name="in_proj",
    )(x, mod, g_pre, *ws)


def _sgu(pc, lng, lnb, ws, bsm):
    tt = pc.shape[0]
    u = _gelu(pc[:, :C_W])
    v = _gelu(pc[:, C_W:])
    mu = jnp.mean(v, axis=-1, keepdims=True)
    vc = v - mu
    var = jnp.mean(vc * vc, axis=-1, keepdims=True)
    vb = (vc * lax.rsqrt(var + EPS) * lng + lnb).astype(BF16)
    grp = lax.broadcasted_iota(jnp.int32, (C_CHUNK, C_W), 1) // C_GD
    outs = []
    for n in range(tt // C_CHUNK):
        r = jnp.dot(ws, vb[n * C_CHUNK:(n + 1) * C_CHUNK, :], preferred_element_type=F32)
        mixed = r[3 * C_CHUNK:]
        for g in (2, 1, 0):
            mixed = jnp.where(grp == g, r[g * C_CHUNK:(g + 1) * C_CHUNK], mixed)
        outs.append(mixed + bsm)
    return u * jnp.concatenate(outs, axis=0)


def _conv_grid(pa, pv, nv, cw, has_prev, has_next):
    tt = pa.shape[0]
    gate = pa[:, :A_W]
    m = pa[:, A_W:2 * A_W] * pa[:, 2 * A_W:]
    col = lax.broadcasted_iota(jnp.int32, (tt, A_HORIZ), 0) % GRID_W
    mh = m[:, :A_HORIZ]
    left = jnp.where(col == 0, 0.0, pltpu.roll(mh, 1, 0))
    right = jnp.where(col == GRID_W - 1, 0.0, pltpu.roll(mh, tt - 1, 0))
    yh = cw[0:1, :A_HORIZ] * left + cw[1:2, :A_HORIZ] * mh + cw[2:3, :A_HORIZ] * right
    mv = m[:, A_HORIZ:]
    pm = pv[:, A_W + A_HORIZ:2 * A_W] * pv[:, 2 * A_W + A_HORIZ:] * has_prev
    nm = nv[:, A_W + A_HORIZ:2 * A_W] * nv[:, 2 * A_W + A_HORIZ:] * has_next
    up = jnp.concatenate([pm, mv[:tt - GRID_W]], axis=0)
    down = jnp.concatenate([mv[GRID_W:], nm], axis=0)
    yv = cw[0:1, A_HORIZ:] * up + cw[1:2, A_HORIZ:] * mv + cw[2:3, A_HORIZ:] * down
    return gate * jnp.concatenate([yh, yv], axis=-1)


def _conv_seq(pa, cw):
    tt = pa.shape[0]
    m = pa[:, A_W:2 * A_W] * pa[:, 2 * A_W:]
    row = lax.broadcasted_iota(jnp.int32, (tt, A_W), 0)
    left = jnp.where(row == 0, 0.0, pltpu.roll(m, 1, 0))
    right = jnp.where(row == tt - 1, 0.0, pltpu.roll(m, tt - 1, 0))
    return pa[:, :A_W] * (cw[0:1] * left + cw[1:2] * m + cw[2:3] * right)


HALO = 16
PAIRS = N_HEADS // 2
PAIR_W = 2 * HEAD_W
PREP_GROUP = 4


def _block_diag2(y):
    lane = lax.broadcasted_iota(jnp.int32, y.shape, 1)
    return jnp.concatenate([jnp.where(lane < GDN_CHUNK, y, 0.0), jnp.where(lane >= GDN_CHUNK, y, 0.0)],
                           axis=0).astype(BF16)


def _pdot(a, b):
    return jnp.dot(a.astype(BF16), _block_diag2(b), preferred_element_type=F32)


def _unit_tri_inverses(lds, ii, jl):
    xs = [jnp.where(ii == jl, 1.0, 0.0) - jnp.where((ii >> 1) == (jl >> 1), ld, 0.0) for ld in lds]
    for k in range(1, 6):
        mk = ((ii >> (k + 1)) == (jl >> (k + 1))) & ((ii >> k) != (jl >> k))
        ts = [_pdot(jnp.where(mk, ld, 0.0), x) for ld, x in zip(lds, xs)]
        xs = [x - _pdot(x, t) for x, t in zip(xs, ts)]
    return xs


def _gdn_prep_kernel(pq_ref, pqp_ref, pqn_ref, ab_ref, cw_ref, alog_t, dtb_t,
                     l1_ref, l2_ref, u_ref, dec_ref, q_s, k_s, v_s, *, cpt, nt):
    j = pl.program_id(1)
    tt = cpt * GDN_CHUNK
    x = pq_ref[...].astype(F32)
    w3 = 3 * GDN_W
    has_prev = jnp.where(j > 0, 1.0, 0.0)
    has_next = jnp.where(j < nt - 1, 1.0, 0.0)
    prev_row = pqp_ref[HALO - 1:HALO, :].astype(F32) * has_prev
    next_row = pqn_ref[0:1, :].astype(F32) * has_next
    row8 = lax.broadcasted_iota(jnp.int32, (8, w3), 0)
    xm1 = pltpu.roll(x, 1, 0)
    xm1 = jnp.concatenate([jnp.where(row8 == 0, prev_row, xm1[:8]), xm1[8:]], axis=0)
    xp1 = pltpu.roll(x, tt - 1, 0)
    xp1 = jnp.concatenate([xp1[:tt - 8], jnp.where(row8 == 7, next_row, xp1[tt - 8:])], axis=0)
    cw = cw_ref[...]
    s = _silu(cw[0:1] * xm1 + cw[1:2] * x + cw[2:3] * xp1)
    for h in range(N_HEADS):
        lo, hi = h * HEAD_W, (h + 1) * HEAD_W
        qh = s[:, lo:hi]
        q_s[:, lo:hi] = qh * (lax.rsqrt(jnp.sum(qh * qh, axis=-1, keepdims=True) + EPS) * HEAD_W ** -0.5)
        kh = s[:, GDN_W + lo:GDN_W + hi]
        k_s[:, lo:hi] = kh * lax.rsqrt(jnp.sum(kh * kh, axis=-1, keepdims=True) + EPS)
    v_s[...] = s[:, 2 * GDN_W:]

    cc = GDN_CHUNK
    ii = lax.broadcasted_iota(jnp.int32, (cc, LANES), 0)
    lane = lax.broadcasted_iota(jnp.int32, (cc, LANES), 1)
    jl = lane & (cc - 1)
    first_head = lane < cc
    lane_row = lax.broadcasted_iota(jnp.int32, (1, LANES), 1)
    zero_k = jnp.zeros((cc, HEAD_W), BF16)

    gts = []
    for c in range(cpt):
        abn = ab_ref[c * cc:(c + 1) * cc, :]
        gts.append(jnp.concatenate([abn, pltpu.roll(abn, LANES - 8, 1)], axis=0).T[0:8, :])
    gt = jnp.concatenate(gts, axis=0)
    beta_all = jax.nn.sigmoid(gt)
    g_all = -jnp.exp(alog_t[...]) * _softplus(gt + dtb_t[...])
    tok = lax.broadcasted_iota(jnp.int32, gt.shape, 1) & (cc - 1)
    pre = suf = g_all
    sh = 1
    while sh < cc:
        pre = pre + jnp.where(tok >= sh, pltpu.roll(pre, sh, 1), 0.0)
        suf = suf + jnp.where(tok < cc - sh, pltpu.roll(suf, LANES - sh, 1), 0.0)
        sh *= 2
    gam_all = jnp.where((lax.broadcasted_iota(jnp.int32, gt.shape, 0) & 7) < 6, pre, suf)
    gcols = jnp.concatenate([gam_all, jnp.zeros((LANES - 8 * cpt, LANES), F32)], axis=0).T

    def setup(c):
        kc = k_s[c * cc:(c + 1) * cc, :]
        qc = q_s[c * cc:(c + 1) * cc, :]
        vb = v_s[c * cc:(c + 1) * cc, :].astype(BF16)
        kb = kc.astype(BF16)
        pairs = []
        for p in range(PAIRS):
            lo = p * PAIR_W
            k0, k1 = kb[:, lo:lo + HEAD_W], kb[:, lo + HEAD_W:lo + PAIR_W]
            kbd = jnp.concatenate([jnp.concatenate([k0, zero_k], axis=1),
                                   jnp.concatenate([zero_k, k1], axis=1)], axis=0)
            vbd = jnp.concatenate([jnp.concatenate([vb[:, lo:lo + HEAD_W], zero_k], axis=1),
                                   jnp.concatenate([zero_k, vb[:, lo + HEAD_W:lo + PAIR_W]], axis=1)], axis=0)
            kq = lax.dot_general(jnp.concatenate([kb[:, lo:lo + PAIR_W], qc[:, lo:lo + PAIR_W].astype(BF16)], axis=0),
                                 kbd, (((1,), (1,)), ((), ())), preferred_element_type=F32)
            kt = jnp.concatenate([kc[:, lo:lo + HEAD_W], kc[:, lo + HEAD_W:lo + PAIR_W]], axis=0).T
            pairs.append((kbd, vbd, kq[:cc], kq[cc:], kt, qc[:, lo:lo + HEAD_W], qc[:, lo + HEAD_W:lo + PAIR_W]))
        probs = []
        for d in range(2):
            incl = (ii >= jl) if d == 0 else (ii <= jl)
            strict = (ii > jl) if d == 0 else (ii < jl)
            for p in range(PAIRS):
                kbd, vbd, kkt, qk, kt, q0, q1 = pairs[p]
                rb = 8 * c + 2 * d + p
                rg = rb + 4
                grow = gam_all[rg:rg + 1, :]
                brow = beta_all[rb:rb + 1, :]
                gc0 = jnp.broadcast_to(gcols[:cc, rg:rg + 1], (cc, LANES))
                gc1 = jnp.broadcast_to(gcols[cc:, rg:rg + 1], (cc, LANES))
                dm = jnp.exp(jnp.where(incl, jnp.where(first_head, gc0, gc1) - grow, -1e30))
                ld = jnp.where(strict, kkt * dm, 0.0) * brow
                top = jnp.where(incl, qk * dm, 0.0) * brow
                probs.append(dict(c=c, d=d, p=p, ld=ld, top=top, grow=grow, brow=brow, gc0=gc0, gc1=gc1,
                                  kbd=kbd, vbd=vbd, kt=kt, q0=q0, q1=q1))
        return probs

    def emit(pr, x):
        c, d, p, grow, brow = pr["c"], pr["d"], pr["p"], pr["grow"], pr["brow"]
        wk = jnp.dot((x * jnp.exp(grow)).astype(BF16), pr["kbd"], preferred_element_type=F32)
        u_ref[d, c, :, p * PAIR_W:(p + 1) * PAIR_W] = jnp.dot(x.astype(BF16), pr["vbd"],
                                                              preferred_element_type=F32).astype(BF16)
        l1_ref[d, c, 2 * p] = jnp.concatenate([wk[:, :HEAD_W], pr["q0"] * jnp.exp(pr["gc0"])], axis=0).astype(BF16)
        l1_ref[d, c, 2 * p + 1] = jnp.concatenate([wk[:, HEAD_W:], pr["q1"] * jnp.exp(pr["gc1"])], axis=0).astype(BF16)
        last = cc - 1 if d == 0 else 0
        gl0 = grow[:, last:last + 1]
        gl1 = grow[:, cc + last:cc + last + 1]
        tail = jnp.exp(jnp.where(lane_row < cc, gl0, gl1) - grow) * brow
        l2_ref[d, c, p] = jnp.concatenate([pr["top"], pr["kt"] * tail], axis=0).astype(BF16)
        dec_ref[d, c, :, p * PAIR_W:(p + 1) * PAIR_W] = jnp.concatenate(
            [jnp.broadcast_to(jnp.exp(gl0), (1, HEAD_W)), jnp.broadcast_to(jnp.exp(gl1), (1, HEAD_W))], axis=1)

    group = min(cpt, PREP_GROUP)
    for c0 in range(0, cpt, group):
        probs = []
        for c in range(c0, c0 + group):
            probs += setup(c)
        xs = _unit_tri_inverses([pr["ld"] for pr in probs], ii, jl)
        for pr, xi in zip(probs, xs):
            emit(pr, xi)


def _gdn_tile_chunks(t):
    return min(t // GDN_CHUNK, 8)


def _gdn_prep(pq, pab, conv_qkv, a_log, dt_bias):
    b, t, w3 = pq.shape
    nc = t // GDN_CHUNK
    cpt = _gdn_tile_chunks(t)
    tt = cpt * GDN_CHUNK
    nt = t // tt
    hb = tt // HALO
    def gate_rows(a):
        rows = jnp.repeat(a.reshape(2 * PAIRS, 2), GDN_CHUNK, axis=1)
        return jnp.tile(jnp.concatenate([jnp.zeros_like(rows), rows], axis=0), (cpt, 1))

    alog_t, dtb_t = gate_rows(a_log), gate_rows(dt_bias)
    const = lambda bi, j: (0, 0)
    tok = lambda bi, j: (bi, j, 0)
    return pl.pallas_call(
        functools.partial(_gdn_prep_kernel, cpt=cpt, nt=nt),
        grid=(b, nt),
        in_specs=[pl.BlockSpec((None, tt, w3), tok),
                  pl.BlockSpec((None, HALO, w3), lambda bi, j: (bi, jnp.maximum(j * hb - 1, 0), 0)),
                  pl.BlockSpec((None, HALO, w3), lambda bi, j: (bi, jnp.minimum((j + 1) * hb, t // HALO - 1), 0)),
                  pl.BlockSpec((None, tt, LANES), tok),
                  pl.BlockSpec((3, w3), const),
                  pl.BlockSpec((8 * cpt, LANES), const), pl.BlockSpec((8 * cpt, LANES), const)],
        out_specs=[pl.BlockSpec((None, 2, cpt, N_HEADS, 2 * GDN_CHUNK, HEAD_W), lambda bi, j: (bi, 0, j, 0, 0, 0)),
                   pl.BlockSpec((None, 2, cpt, PAIRS, 3 * GDN_CHUNK, HEAD_W), lambda bi, j: (bi, 0, j, 0, 0, 0)),
                   pl.BlockSpec((None, 2, cpt, GDN_CHUNK, GDN_W), lambda bi, j: (bi, 0, j, 0, 0)),
                   pl.BlockSpec((None, 2, cpt, 1, GDN_W), lambda bi, j: (bi, 0, j, 0, 0))],
        out_shape=[jax.ShapeDtypeStruct((b, 2, nc, N_HEADS, 2 * GDN_CHUNK, HEAD_W), BF16),
                   jax.ShapeDtypeStruct((b, 2, nc, PAIRS, 3 * GDN_CHUNK, HEAD_W), BF16),
                   jax.ShapeDtypeStruct((b, 2, nc, GDN_CHUNK, GDN_W), BF16),
                   jax.ShapeDtypeStruct((b, 2, nc, 1, GDN_W), F32)],
        scratch_shapes=[pltpu.VMEM((tt, GDN_W), F32)] * 3,
        compiler_params=_cparams("parallel", "parallel"),
        name="gdn_prep",
    )(pq, pq, pq, pab, conv_qkv, alog_t, dtb_t)


SCAN_BATCH = 4
SCAN_CHUNKS = 4


def _gdn_scan_kernel(l1f, l1b, l2f, l2b, uf, ub, decf, decb, s0_ref, of_ref, ob_ref, sfin_ref, s_ref,
                     *, cpg, nj, nb):
    j = pl.program_id(1)

    @pl.when(j == 0)
    def _():
        s_ref[...] = s0_ref[...]

    zero = jnp.zeros((GDN_CHUNK, HEAD_W), BF16)

    def step(i, carry):
        chains = []
        for n in range(nb):
            chains.append((n, 0, l1f, l2f, uf, decf, of_ref, i))
            chains.append((n, 1, l1b, l2b, ub, decb, ob_ref, cpg - 1 - i))
        first = []
        for n, d, l1, l2, u, dec, o_ref, c in chains:
            vts, qss = [], []
            for h in range(N_HEADS):
                sh = s_ref[n, d, :, h * HEAD_W:(h + 1) * HEAD_W].astype(BF16)
                r1 = jnp.dot(l1[n, c, h], sh, preferred_element_type=F32)
                vts.append((u[n, c, :, h * HEAD_W:(h + 1) * HEAD_W].astype(F32) - r1[:GDN_CHUNK]).astype(BF16))
                qss.append(r1[GDN_CHUNK:])
            first.append((vts, qss))
        for (n, d, l1, l2, u, dec, o_ref, c), (vts, qss) in zip(chains, first):
            intra = []
            for p in range(N_HEADS // 2):
                rhs = jnp.concatenate([jnp.concatenate([vts[2 * p], zero], axis=-1),
                                       jnp.concatenate([zero, vts[2 * p + 1]], axis=-1)], axis=0)
                r2 = jnp.dot(l2[n, c, p], rhs, preferred_element_type=F32)
                intra.append(r2[:GDN_CHUNK])
                lo, hi = 2 * p * HEAD_W, 2 * (p + 1) * HEAD_W
                s_ref[n, d, :, lo:hi] = s_ref[n, d, :, lo:hi] * dec[n, c][:, lo:hi] + r2[GDN_CHUNK:]
            o = jnp.concatenate(qss, axis=-1) + jnp.concatenate(intra, axis=-1)
            o_ref[n, pl.ds(pl.multiple_of(c * GDN_CHUNK, GDN_CHUNK), GDN_CHUNK), :] = o.astype(o_ref.dtype)
        return carry

    lax.fori_loop(0, cpg, step, 0)

    @pl.when(j == nj - 1)
    def _():
        sfin_ref[...] = s_ref[...]


def _gdn_scan(l1, l2, u, dec, s0):
    b, _, nc = l1.shape[:3]
    cpg = min(nc, SCAN_CHUNKS)
    nj = nc // cpg
    t = nc * GDN_CHUNK
    fwd5 = lambda bi, j: (bi, 0, j, 0, 0, 0)
    bwd5 = lambda bi, j: (bi, 1, nj - 1 - j, 0, 0, 0)
    fwd4 = lambda bi, j: (bi, 0, j, 0, 0)
    bwd4 = lambda bi, j: (bi, 1, nj - 1 - j, 0, 0)
    nb = SCAN_BATCH if b % SCAN_BATCH == 0 else 1
    b1 = (nb, None, cpg, N_HEADS, 2 * GDN_CHUNK, HEAD_W)
    b2 = (nb, None, cpg, N_HEADS // 2, 3 * GDN_CHUNK, HEAD_W)
    bu = (nb, None, cpg, GDN_CHUNK, GDN_W)
    bd = (nb, None, cpg, 1, GDN_W)
    st = pl.BlockSpec((nb, 2, HEAD_W, GDN_W), lambda bi, j: (bi, 0, 0, 0))
    return pl.pallas_call(
        functools.partial(_gdn_scan_kernel, cpg=cpg, nj=nj, nb=nb),
        grid=(b // nb, nj),
        in_specs=[pl.BlockSpec(b1, fwd5), pl.BlockSpec(b1, bwd5), pl.BlockSpec(b2, fwd5), pl.BlockSpec(b2, bwd5),
                  pl.BlockSpec(bu, fwd4), pl.BlockSpec(bu, bwd4), pl.BlockSpec(bd, fwd4), pl.BlockSpec(bd, bwd4), st],
        out_specs=[pl.BlockSpec((nb, cpg * GDN_CHUNK, GDN_W), lambda bi, j: (bi, j, 0)),
                   pl.BlockSpec((nb, cpg * GDN_CHUNK, GDN_W), lambda bi, j: (bi, nj - 1 - j, 0)), st],
        out_shape=[jax.ShapeDtypeStruct((b, t, GDN_W), BF16), jax.ShapeDtypeStruct((b, t, GDN_W), BF16),
                   jax.ShapeDtypeStruct((b, 2, HEAD_W, GDN_W), F32)],
        scratch_shapes=[pltpu.VMEM((nb, 2, HEAD_W, GDN_W), F32)],
        compiler_params=_cparams("parallel", "arbitrary"),
        name="gdn_scan",
    )(l1, l1, l2, l2, u, u, dec, dec, s0)


FFN_TILE = 256


def _post_kernel(x_ref, of_ref, ob_ref, z_ref, pa_ref, prev_ref, next_ref, pc_ref, mod_ref, cw_ref, lng_ref, lnb_ref,
                 ws_ref, bsm_ref, gon_ref, wo_ref, gqm_ref, gpf_ref, w1_ref, w2_ref, gqf_ref, o_ref,
                 *, nj, grid_conv, hidden):
    j = pl.program_id(1)
    pa = pa_ref[...].astype(F32)
    if grid_conv:
        ya = _conv_grid(pa, prev_ref[...].astype(F32), next_ref[...].astype(F32), cw_ref[...],
                        jnp.where(j > 0, 1.0, 0.0), jnp.where(j < nj - 1, 1.0, 0.0))
    else:
        ya = _conv_seq(pa, cw_ref[...])
    yc = _sgu(pc_ref[...].astype(F32), lng_ref[...], lnb_ref[...], ws_ref[...], bsm_ref[...])
    o = of_ref[...].astype(F32) + ob_ref[...].astype(F32)
    z = z_ref[...].astype(F32)
    ybs = []
    for h in range(N_HEADS):
        oh = o[:, h * HEAD_W:(h + 1) * HEAD_W]
        oh = oh * lax.rsqrt(jnp.mean(oh * oh, axis=-1, keepdims=True) + EPS) * gon_ref[...]
        ybs.append((oh * _silu(z[:, h * HEAD_W:(h + 1) * HEAD_W])).astype(BF16))
    y = jnp.concatenate([ya.astype(BF16)] + ybs + [yc.astype(BF16)], axis=-1)
    r = jnp.dot(y, wo_ref[...], preferred_element_type=F32)
    x = x_ref[...] + _rms(r, gqm_ref[...] * mod_ref[2:3, :])
    hb = (_rms(x, gpf_ref[...] * (1.0 + mod_ref[4:5, :])) + mod_ref[3:4, :]).astype(BF16)
    acc = None
    for c0 in range(0, hidden, FFN_TILE):
        g = jnp.dot(hb, w1_ref[:, c0:c0 + FFN_TILE], preferred_element_type=F32)
        u = jnp.dot(hb, w1_ref[:, hidden + c0:hidden + c0 + FFN_TILE], preferred_element_type=F32)
        part = jnp.dot((_silu(g) * u).astype(BF16), w2_ref[c0:c0 + FFN_TILE, :], preferred_element_type=F32)
        acc = part if acc is None else acc + part
    o_ref[...] = x + _rms(acc, gqf_ref[...] * mod_ref[5:6, :])


def _post(x, of, ob, z, pa, pc, mod, conv_a, ln_g, ln_b, ws, bsm, g_onorm, w_o, gqm, gpf, w1, w2, gqf, *, grid_conv):
    b, t, d = x.shape
    hidden = w2.shape[0]
    tm = _row_tile(t)
    nj = t // tm
    assert grid_conv or nj == 1
    rows = tm // GRID_W
    last = t // GRID_W - 1
    const = lambda bi, j: (0, 0)
    tok = lambda bi, j: (bi, j, 0)
    full = lambda a: pl.BlockSpec(a.shape, const)
    resident = lambda a: pl.BlockSpec(a.shape, const, pipeline_mode=pl.Buffered(1))
    return pl.pallas_call(
        functools.partial(_post_kernel, nj=nj, grid_conv=grid_conv, hidden=hidden),
        grid=(b, nj),
        in_specs=[pl.BlockSpec((None, tm, d), tok),
                  pl.BlockSpec((None, tm, GDN_W), tok), pl.BlockSpec((None, tm, GDN_W), tok),
                  pl.BlockSpec((None, tm, GDN_W), tok),
                  pl.BlockSpec((None, tm, 3 * A_W), tok),
                  pl.BlockSpec((None, GRID_W, 3 * A_W), lambda bi, j: (bi, jnp.maximum(j * rows - 1, 0), 0)),
                  pl.BlockSpec((None, GRID_W, 3 * A_W), lambda bi, j: (bi, jnp.minimum((j + 1) * rows, last), 0)),
                  pl.BlockSpec((None, tm, 2 * C_W), tok),
                  pl.BlockSpec((None, 8, d), lambda bi, j: (bi, 0, 0)),
                  full(conv_a), full(ln_g), full(ln_b), full(ws), full(bsm), full(g_onorm), resident(w_o),
                  full(gqm), full(gpf), resident(w1), resident(w2), full(gqf)],
        out_specs=pl.BlockSpec((None, tm, d), tok),
        out_shape=jax.ShapeDtypeStruct((b, t, d), F32),
        compiler_params=_cparams("parallel", "parallel"),
        name="post_grid" if grid_conv else "post_seq",
    )(x, of, ob, z, pa, pa, pa, pc, mod, conv_a, ln_g, ln_b, ws, bsm, g_onorm, w_o, gqm, gpf, w1, w2, gqf)


OFF_A, OFF_Q, OFF_AB, OFF_Z, OFF_C = 0, 3 * A_W, 3 * A_W + 3 * GDN_W, 3 * A_W + 3 * GDN_W + 16, 3 * A_W + 4 * GDN_W + 16
GATE_PERM = [4 * ((n % 8) // 2) + 2 * (n % 2) + n // 8 for n in range(16)]


def _mod_rows(rows, b, d):
    m = jnp.broadcast_to(rows.reshape(-1, 6, d), (b, 6, d))
    return jnp.pad(m, ((0, 0), (0, 2), (0, 0)))


def kernel(x, c, ctx, c_ctx, w_mod, b_mod, g_pre_mix, g_post_mix, g_pre_ffn, g_post_ffn, w_in, conv_a, conv_qkv,
           a_log, dt_bias, g_onorm, ln_c_g, ln_c_b, w_s, b_s, w_o, w_ffn_in, w_ffn_out):
    b, t, d = x.shape
    depth = w_mod.shape[0]
    cc = jnp.concatenate([c, c_ctx[None, :], jnp.zeros((16 - b - 1, d), F32)], axis=0)
    mod = _modulation(cc, w_mod, b_mod)
    s_zero = jnp.zeros((b, 2, HEAD_W, GDN_W), F32)
    for l in range(depth):
        last = l == depth - 1
        modx = _mod_rows(mod[l, :b], b, d)
        modc = _mod_rows(mod[l, b:b + 1], b, d)
        wl = w_in[l].astype(BF16)
        w_gate = wl[:, OFF_AB:OFF_Z][:, jnp.array(GATE_PERM)]
        ws_in = [wl[:, OFF_Q:OFF_AB], wl[:, OFF_Z:OFF_C], wl[:, OFF_A:OFF_Q], wl[:, OFF_C:],
                 jnp.pad(w_gate, ((0, 0), (0, LANES - 16)))]
        gpm, gqm = g_pre_mix[l][None, :], g_post_mix[l][None, :]
        gpf, gqf = g_pre_ffn[l][None, :], g_post_ffn[l][None, :]
        sgu_w = w_s[l].reshape(C_GROUPS * C_CHUNK, C_CHUNK).astype(BF16)
        sgu_b = jnp.repeat(b_s[l].T, C_GD, axis=1)
        lng, lnb = ln_c_g[l][None, :], ln_c_b[l][None, :]
        wo = w_o[l].astype(BF16)
        w1, w2 = w_ffn_in[l].astype(BF16), w_ffn_out[l].astype(BF16)
        gon = g_onorm[l][None, :]

        def mixer_inputs(stream, m):
            pq, pz, pa, pc, pab = _in_proj(stream, m, gpm, ws_in)
            return pq, pz, pa, pc, _gdn_prep(pq, pab, conv_qkv[l], a_log[l], dt_bias[l])

        def finish(stream, m, of, ob, pz, pa, pc, grid_conv):
            return _post(stream, of, ob, pz, pa, pc, m, conv_a[l], lng, lnb, sgu_w, sgu_b, gon, wo, gqm, gpf, w1, w2,
                         gqf, grid_conv=grid_conv)

        pq, pz, pa, pc, ops = mixer_inputs(ctx, modc)
        of, ob, s_ctx = _gdn_scan(*ops, s_zero)
        if not last:
            ctx = finish(ctx, modc, of, ob, pz, pa, pc, False)
        pq, pz, pa, pc, ops = mixer_inputs(x, modx)
        of, ob, _ = _gdn_scan(*ops, s_ctx)
        x = finish(x, modx, of, ob, pz, pa, pc, True)
    return x
```

```python
import functools

import jax
import jax.numpy as jnp
from jax import lax
from jax.experimental import pallas as pl
from jax.experimental.pallas import tpu as pltpu

F32 = jnp.float32
BF16 = jnp.bfloat16

EPS = 1e-6
GRID_W = 64
N_HEADS = 4
HEAD_W = 128
GDN_W = N_HEADS * HEAD_W
GDN_CHUNK = 64
A_W = 256
A_HORIZ = 128
C_W = 256
C_GROUPS = 4
C_GD = C_W // C_GROUPS
C_CHUNK = 128
LANES = 128
VMEM_LIMIT = 56 * 1024 * 1024


def _cparams(*sem):
    return pltpu.CompilerParams(dimension_semantics=sem, vmem_limit_bytes=VMEM_LIMIT)


def _dot3(a, b):
    ah, bh = a.astype(BF16), b.astype(BF16)
    al, bl = (a - ah.astype(F32)).astype(BF16), (b - bh.astype(F32)).astype(BF16)
    d = functools.partial(jnp.dot, preferred_element_type=F32)
    return d(ah, bh) + (d(ah, bl) + d(al, bh))


def _rms(x, g):
    return x * lax.rsqrt(jnp.mean(x * x, axis=-1, keepdims=True) + EPS) * g


def _silu(x):
    return x * jax.nn.sigmoid(x)


def _gelu(x):
    return 0.5 * x * (1.0 + lax.erf(x * 0.7071067811865476))


def _softplus(x):
    return jnp.maximum(x, 0.0) + jnp.log1p(jnp.exp(-jnp.abs(x)))


def _mod_kernel(c_ref, w_ref, b_ref, o_ref):
    o_ref[...] = _dot3(_silu(c_ref[...]), w_ref[...]) + b_ref[...]


def _modulation(cc, w_mod, b_mod):
    depth, d, six_d = w_mod.shape
    r = cc.shape[0]
    return pl.pallas_call(
        _mod_kernel,
        grid=(depth, six_d // d),
        in_specs=[pl.BlockSpec((r, d), lambda l, j: (0, 0)),
                  pl.BlockSpec((None, d, d), lambda l, j: (l, 0, j)),
                  pl.BlockSpec((None, 1, d), lambda l, j: (l, 0, j))],
        out_specs=pl.BlockSpec((None, r, d), lambda l, j: (l, 0, j)),
        out_shape=jax.ShapeDtypeStruct((depth, r, six_d), F32),
        compiler_params=_cparams("parallel", "parallel"),
        name="modulation",
    )(cc, w_mod, b_mod.reshape(depth, 1, six_d))


def _in_kernel(x_ref, mod_ref, g_ref, wq, wz, wa, wc, wab, oq, oz, oa, oc, oab):
    h = _rms(x_ref[...], g_ref[...] * (1.0 + mod_ref[1:2, :])) + mod_ref[0:1, :]
    hb = h.astype(BF16)
    for w, o in ((wq, oq), (wz, oz), (wa, oa), (wc, oc)):
        o[...] = jnp.dot(hb, w[...], preferred_element_type=F32).astype(BF16)
    oab[...] = jnp.dot(hb, wab[...], preferred_element_type=F32)


def _row_tile(t, rows=512):
    return min(t, rows)


def _in_proj(x, mod, g_pre, ws):
    b, t, d = x.shape
    tm = _row_tile(t, 1024)
    widths = [w.shape[1] for w in ws]
    const = lambda bi, j: (0, 0)
    tok = lambda bi, j: (bi, j, 0)
    return pl.pallas_call(
        _in_kernel,
        grid=(b, t // tm),
        in_specs=[pl.BlockSpec((None, tm, d), tok),
                  pl.BlockSpec((None, 8, d), lambda bi, j: (bi, 0, 0)),
                  pl.BlockSpec((1, d), const)]
                 + [pl.BlockSpec((d, n), const) for n in widths],
        out_specs=[pl.BlockSpec((None, tm, n), tok) for n in widths],
        out_shape=[jax.ShapeDtypeStruct((b, t, n), BF16) for n in widths[:-1]]
                  + [jax.ShapeDtypeStruct((b, t, widths[-1]), F32)],
        compiler_params=_cparams("parallel", "parallel"),
        name="in_proj",
    )(x, mod, g_pre, *ws)


def _sgu(pc, lng, lnb, ws, bsm):
    tt = pc.shape[0]
    u = _gelu(pc[:, :C_W])
    v = _gelu(pc[:, C_W:])
    mu = jnp.mean(v, axis=-1, keepdims=True)
    vc = v - mu
    var = jnp.mean(vc * vc, axis=-1, keepdims=True)
    vb = (vc * lax.rsqrt(var + EPS) * lng + lnb).astype(BF16)
    grp = lax.broadcasted_iota(jnp.int32, (C_CHUNK, C_W), 1) // C_GD
    outs = []
    for n in range(tt // C_CHUNK):
        r = jnp.dot(ws, vb[n * C_CHUNK:(n + 1) * C_CHUNK, :], preferred_element_type=F32)
        mixed = r[3 * C_CHUNK:]
        for g in (2, 1, 0):
            mixed = jnp.where(grp == g, r[g * C_CHUNK:(g + 1) * C_CHUNK], mixed)
        outs.append(mixed + bsm)
    return u * jnp.concatenate(outs, axis=0)


def _conv_grid(pa, pv, nv, cw, has_prev, has_next):
    tt = pa.shape[0]
    gate = pa[:, :A_W]
    m = pa[:, A_W:2 * A_W] * pa[:, 2 * A_W:]
    col = lax.broadcasted_iota(jnp.int32, (tt, A_HORIZ), 0) % GRID_W
    mh = m[:, :A_HORIZ]
    left = jnp.where(col == 0, 0.0, pltpu.roll(mh, 1, 0))
    right = jnp.where(col == GRID_W - 1, 0.0, pltpu.roll(mh, tt - 1, 0))
    yh = cw[0:1, :A_HORIZ] * left + cw[1:2, :A_HORIZ] * mh + cw[2:3, :A_HORIZ] * right
    mv = m[:, A_HORIZ:]
    pm = pv[:, A_W + A_HORIZ:2 * A_W] * pv[:, 2 * A_W + A_HORIZ:] * has_prev
    nm = nv[:, A_W + A_HORIZ:2 * A_W] * nv[:, 2 * A_W + A_HORIZ:] * has_next
    up = jnp.concatenate([pm, mv[:tt - GRID_W]], axis=0)
    down = jnp.concatenate([mv[GRID_W:], nm], axis=0)
    yv = cw[0:1, A_HORIZ:] * up + cw[1:2, A_HORIZ:] * mv + cw[2:3, A_HORIZ:] * down
    return gate * jnp.concatenate([yh, yv], axis=-1)


def _conv_seq(pa, cw):
    tt = pa.shape[0]
    m = pa[:, A_W:2 * A_W] * pa[:, 2 * A_W:]
    row = lax.broadcasted_iota(jnp.int32, (tt, A_W), 0)
    left = jnp.where(row == 0, 0.0, pltpu.roll(m, 1, 0))
    right = jnp.where(row == tt - 1, 0.0, pltpu.roll(m, tt - 1, 0))
    return pa[:, :A_W] * (cw[0:1] * left + cw[1:2] * m + cw[2:3] * right)


HALO = 16
PAIRS = N_HEADS // 2
PAIR_W = 2 * HEAD_W
PREP_GROUP = 4


def _block_diag2(y):
    lane = lax.broadcasted_iota(jnp.int32, y.shape, 1)
    return jnp.concatenate([jnp.where(lane < GDN_CHUNK, y, 0.0), jnp.where(lane >= GDN_CHUNK, y, 0.0)],
                           axis=0).astype(BF16)


def _pdot(a, b):
    return jnp.dot(a.astype(BF16), _block_diag2(b), preferred_element_type=F32)


def _unit_tri_inverses(lds, ii, jl):
    xs = [jnp.where(ii == jl, 1.0, 0.0) - jnp.where((ii >> 1) == (jl >> 1), ld, 0.0) for ld in lds]
    for k in range(1, 6):
        mk = ((ii >> (k + 1)) == (jl >> (k + 1))) & ((ii >> k) != (jl >> k))
        ts = [_pdot(jnp.where(mk, ld, 0.0), x) for ld, x in zip(lds, xs)]
        xs = [x - _pdot(x, t) for x, t in zip(xs, ts)]
    return xs


def _gdn_prep_kernel(pq_ref, pqp_ref, pqn_ref, ab_ref, cw_ref, alog_t, dtb_t,
                     l1_ref, l2_ref, u_ref, dec_ref, q_s, k_s, v_s, *, cpt, nt):
    j = pl.program_id(1)
    tt = cpt * GDN_CHUNK
    x = pq_ref[...].astype(F32)
    w3 = 3 * GDN_W
    has_prev = jnp.where(j > 0, 1.0, 0.0)
    has_next = jnp.where(j < nt - 1, 1.0, 0.0)
    prev_row = pqp_ref[HALO - 1:HALO, :].astype(F32) * has_prev
    next_row = pqn_ref[0:1, :].astype(F32) * has_next
    row8 = lax.broadcasted_iota(jnp.int32, (8, w3), 0)
    xm1 = pltpu.roll(x, 1, 0)
    xm1 = jnp.concatenate([jnp.where(row8 == 0, prev_row, xm1[:8]), xm1[8:]], axis=0)
    xp1 = pltpu.roll(x, tt - 1, 0)
    xp1 = jnp.concatenate([xp1[:tt - 8], jnp.where(row8 == 7, next_row, xp1[tt - 8:])], axis=0)
    cw = cw_ref[...]
    s = _silu(cw[0:1] * xm1 + cw[1:2] * x + cw[2:3] * xp1)
    for h in range(N_HEADS):
        lo, hi = h * HEAD_W, (h + 1) * HEAD_W
        qh = s[:, lo:hi]
        q_s[:, lo:hi] = qh * (lax.rsqrt(jnp.sum(qh * qh, axis=-1, keepdims=True) + EPS) * HEAD_W ** -0.5)
        kh = s[:, GDN_W + lo:GDN_W + hi]
        k_s[:, lo:hi] = kh * lax.rsqrt(jnp.sum(kh * kh, axis=-1, keepdims=True) + EPS)
    v_s[...] = s[:, 2 * GDN_W:]

    cc = GDN_CHUNK
    ii = lax.broadcasted_iota(jnp.int32, (cc, LANES), 0)
    lane = lax.broadcasted_iota(jnp.int32, (cc, LANES), 1)
    jl = lane & (cc - 1)
    first_head = lane < cc
    lane_row = lax.broadcasted_iota(jnp.int32, (1, LANES), 1)
    zero_k = jnp.zeros((cc, HEAD_W), BF16)

    gts = []
    for c in range(cpt):
        abn = ab_ref[c * cc:(c + 1) * cc, :]
        gts.append(jnp.concatenate([abn, pltpu.roll(abn, LANES - 8, 1)], axis=0).T[0:8, :])
    gt = jnp.concatenate(gts, axis=0)
    beta_all = jax.nn.sigmoid(gt)
    g_all = -jnp.exp(alog_t[...]) * _softplus(gt + dtb_t[...])
    tok = lax.broadcasted_iota(jnp.int32, gt.shape, 1) & (cc - 1)
    pre = suf = g_all
    sh = 1
    while sh < cc:
        pre = pre + jnp.where(tok >= sh, pltpu.roll(pre, sh, 1), 0.0)
        suf = suf + jnp.where(tok < cc - sh, pltpu.roll(suf, LANES - sh, 1), 0.0)
        sh *= 2
    gam_all = jnp.where((lax.broadcasted_iota(jnp.int32, gt.shape, 0) & 7) < 6, pre, suf)
    gcols = jnp.concatenate([gam_all, jnp.zeros((LANES - 8 * cpt, LANES), F32)], axis=0).T

    def setup(c):
        kc = k_s[c * cc:(c + 1) * cc, :]
        qc = q_s[c * cc:(c + 1) * cc, :]
        vb = v_s[c * cc:(c + 1) * cc, :].astype(BF16)
        kb = kc.astype(BF16)
        pairs = []
        for p in range(PAIRS):
            lo = p * PAIR_W
            k0, k1 = kb[:, lo:lo + HEAD_W], kb[:, lo + HEAD_W:lo + PAIR_W]
            kbd = jnp.concatenate([jnp.concatenate([k0, zero_k], axis=1),
                                   jnp.concatenate([zero_k, k1], axis=1)], axis=0)
            vbd = jnp.concatenate([jnp.concatenate([vb[:, lo:lo + HEAD_W], zero_k], axis=1),
                                   jnp.concatenate([zero_k, vb[:, lo + HEAD_W:lo + PAIR_W]], axis=1)], axis=0)
            kq = lax.dot_general(jnp.concatenate([kb[:, lo:lo + PAIR_W], qc[:, lo:lo + PAIR_W].astype(BF16)], axis=0),
                                 kbd, (((1,), (1,)), ((), ())), preferred_element_type=F32)
            kt = jnp.concatenate([kc[:, lo:lo + HEAD_W], kc[:, lo + HEAD_W:lo + PAIR_W]], axis=0).T
            pairs.append((kbd, vbd, kq[:cc], kq[cc:], kt, qc[:, lo:lo + HEAD_W], qc[:, lo + HEAD_W:lo + PAIR_W]))
        probs = []
        for d in range(2):
            incl = (ii >= jl) if d == 0 else (ii <= jl)
            strict = (ii > jl) if d == 0 else (ii < jl)
            for p in range(PAIRS):
                kbd, vbd, kkt, qk, kt, q0, q1 = pairs[p]
                rb = 8 * c + 2 * d + p
                rg = rb + 4
                grow = gam_all[rg:rg + 1, :]
                brow = beta_all[rb:rb + 1, :]
                gc0 = jnp.broadcast_to(gcols[:cc, rg:rg + 1], (cc, LANES))
                gc1 = jnp.broadcast_to(gcols[cc:, rg:rg + 1], (cc, LANES))
                dm = jnp.exp(jnp.where(incl, jnp.where(first_head, gc0, gc1) - grow, -1e30))
                ld = jnp.where(strict, kkt * dm, 0.0) * brow
                top = jnp.where(incl, qk * dm, 0.0) * brow
                probs.append(dict(c=c, d=d, p=p, ld=ld, top=top, grow=grow, brow=brow, gc0=gc0, gc1=gc1,
                                  kbd=kbd, vbd=vbd, kt=kt, q0=q0, q1=q1))
        return probs

    def emit(pr, x):
        c, d, p, grow, brow = pr["c"], pr["d"], pr["p"], pr["grow"], pr["brow"]
        wk = jnp.dot((x * jnp.exp(grow)).astype(BF16), pr["kbd"], preferred_element_type=F32)
        u_ref[d, c, :, p * PAIR_W:(p + 1) * PAIR_W] = jnp.dot(x.astype(BF16), pr["vbd"],
                                                              preferred_element_type=F32).astype(BF16)
        l1_ref[d, c, 2 * p] = jnp.concatenate([wk[:, :HEAD_W], pr["q0"] * jnp.exp(pr["gc0"])], axis=0).astype(BF16)
        l1_ref[d, c, 2 * p + 1] = jnp.concatenate([wk[:, HEAD_W:], pr["q1"] * jnp.exp(pr["gc1"])], axis=0).astype(BF16)
        last = cc - 1 if d == 0 else 0
        gl0 = grow[:, last:last + 1]
        gl1 = grow[:, cc + last:cc + last + 1]
        tail = jnp.exp(jnp.where(lane_row < cc, gl0, gl1) - grow) * brow
        l2_ref[d, c, p] = jnp.concatenate([pr["top"], pr["kt"] * tail], axis=0).astype(BF16)
        dec_ref[d, c, :, p * PAIR_W:(p + 1) * PAIR_W] = jnp.concatenate(
            [jnp.broadcast_to(jnp.exp(gl0), (1, HEAD_W)), jnp.broadcast_to(jnp.exp(gl1), (1, HEAD_W))], axis=1)

    group = min(cpt, PREP_GROUP)
    for c0 in range(0, cpt, group):
        probs = []
        for c in range(c0, c0 + group):
            probs += setup(c)
        xs = _unit_tri_inverses([pr["ld"] for pr in probs], ii, jl)
        for pr, xi in zip(probs, xs):
            emit(pr, xi)


def _gdn_tile_chunks(t):
    return min(t // GDN_CHUNK, 8)


def _gdn_prep(pq, pab, conv_qkv, a_log, dt_bias):
    b, t, w3 = pq.shape
    nc = t // GDN_CHUNK
    cpt = _gdn_tile_chunks(t)
    tt = cpt * GDN_CHUNK
    nt = t // tt
    hb = tt // HALO
    def gate_rows(a):
        rows = jnp.repeat(a.reshape(2 * PAIRS, 2), GDN_CHUNK, axis=1)
        return jnp.tile(jnp.concatenate([jnp.zeros_like(rows), rows], axis=0), (cpt, 1))

    alog_t, dtb_t = gate_rows(a_log), gate_rows(dt_bias)
    const = lambda bi, j: (0, 0)
    tok = lambda bi, j: (bi, j, 0)
    return pl.pallas_call(
        functools.partial(_gdn_prep_kernel, cpt=cpt, nt=nt),
        grid=(b, nt),
        in_specs=[pl.BlockSpec((None, tt, w3), tok),
                  pl.BlockSpec((None, HALO, w3), lambda bi, j: (bi, jnp.maximum(j * hb - 1, 0), 0)),
                  pl.BlockSpec((None, HALO, w3), lambda bi, j: (bi, jnp.minimum((j + 1) * hb, t // HALO - 1), 0)),
                  pl.BlockSpec((None, tt, LANES), tok),
                  pl.BlockSpec((3, w3), const),
                  pl.BlockSpec((8 * cpt, LANES), const), pl.BlockSpec((8 * cpt, LANES), const)],
        out_specs=[pl.BlockSpec((None, 2, cpt, N_HEADS, 2 * GDN_CHUNK, HEAD_W), lambda bi, j: (bi, 0, j, 0, 0, 0)),
                   pl.BlockSpec((None, 2, cpt, PAIRS, 3 * GDN_CHUNK, HEAD_W), lambda bi, j: (bi, 0, j, 0, 0, 0)),
                   pl.BlockSpec((None, 2, cpt, GDN_CHUNK, GDN_W), lambda bi, j: (bi, 0, j, 0, 0)),
                   pl.BlockSpec((None, 2, cpt, 1, GDN_W), lambda bi, j: (bi, 0, j, 0, 0))],
        out_shape=[jax.ShapeDtypeStruct((b, 2, nc, N_HEADS, 2 * GDN_CHUNK, HEAD_W), BF16),
                   jax.ShapeDtypeStruct((b, 2, nc, PAIRS, 3 * GDN_CHUNK, HEAD_W), BF16),
                   jax.ShapeDtypeStruct((b, 2, nc, GDN_CHUNK, GDN_W), BF16),
                   jax.ShapeDtypeStruct((b, 2, nc, 1, GDN_W), F32)],
        scratch_shapes=[pltpu.VMEM((tt, GDN_W), F32)] * 3,
        compiler_params=_cparams("parallel", "parallel"),
        name="gdn_prep",
    )(pq, pq, pq, pab, conv_qkv, alog_t, dtb_t)


SCAN_BATCH = 4
SCAN_CHUNKS = 4


def _gdn_scan_kernel(l1f, l1b, l2f, l2b, uf, ub, decf, decb, s0_ref, of_ref, ob_ref, sfin_ref, s_ref,
                     *, cpg, nj, nb):
    j = pl.program_id(1)

    @pl.when(j == 0)
    def _():
        s_ref[...] = s0_ref[...]

    zero = jnp.zeros((GDN_CHUNK, HEAD_W), BF16)

    def step(i, carry):
        chains = []
        for n in range(nb):
            chains.append((n, 0, l1f, l2f, uf, decf, of_ref, i))
            chains.append((n, 1, l1b, l2b, ub, decb, ob_ref, cpg - 1 - i))
        first = []
        for n, d, l1, l2, u, dec, o_ref, c in chains:
            vts, qss = [], []
            for h in range(N_HEADS):
                sh = s_ref[n, d, :, h * HEAD_W:(h + 1) * HEAD_W].astype(BF16)
                r1 = jnp.dot(l1[n, c, h], sh, preferred_element_type=F32)
                vts.append((u[n, c, :, h * HEAD_W:(h + 1) * HEAD_W].astype(F32) - r1[:GDN_CHUNK]).astype(BF16))
                qss.append(r1[GDN_CHUNK:])
            first.append((vts, qss))
        for (n, d, l1, l2, u, dec, o_ref, c), (vts, qss) in zip(chains, first):
            intra = []
            for p in range(N_HEADS // 2):
                rhs = jnp.concatenate([jnp.concatenate([vts[2 * p], zero], axis=-1),
                                       jnp.concatenate([zero, vts[2 * p + 1]], axis=-1)], axis=0)
                r2 = jnp.dot(l2[n, c, p], rhs, preferred_element_type=F32)
                intra.append(r2[:GDN_CHUNK])
                lo, hi = 2 * p * HEAD_W, 2 * (p + 1) * HEAD_W
                s_ref[n, d, :, lo:hi] = s_ref[n, d, :, lo:hi] * dec[n, c][:, lo:hi] + r2[GDN_CHUNK:]
            o = jnp.concatenate(qss, axis=-1) + jnp.concatenate(intra, axis=-1)
            o_ref[n, pl.ds(pl.multiple_of(c * GDN_CHUNK, GDN_CHUNK), GDN_CHUNK), :] = o.astype(o_ref.dtype)
        return carry

    lax.fori_loop(0, cpg, step, 0)

    @pl.when(j == nj - 1)
    def _():
        sfin_ref[...] = s_ref[...]


def _gdn_scan(l1, l2, u, dec, s0):
    b, _, nc = l1.shape[:3]
    cpg = min(nc, SCAN_CHUNKS)
    nj = nc // cpg
    t = nc * GDN_CHUNK
    fwd5 = lambda bi, j: (bi, 0, j, 0, 0, 0)
    bwd5 = lambda bi, j: (bi, 1, nj - 1 - j, 0, 0, 0)
    fwd4 = lambda bi, j: (bi, 0, j, 0, 0)
    bwd4 = lambda bi, j: (bi, 1, nj - 1 - j, 0, 0)
    nb = SCAN_BATCH if b % SCAN_BATCH == 0 else 1
    b1 = (nb, None, cpg, N_HEADS, 2 * GDN_CHUNK, HEAD_W)
    b2 = (nb, None, cpg, N_HEADS // 2, 3 * GDN_CHUNK, HEAD_W)
    bu = (nb, None, cpg, GDN_CHUNK, GDN_W)
    bd = (nb, None, cpg, 1, GDN_W)
    st = pl.BlockSpec((nb, 2, HEAD_W, GDN_W), lambda bi, j: (bi, 0, 0, 0))
    return pl.pallas_call(
        functools.partial(_gdn_scan_kernel, cpg=cpg, nj=nj, nb=nb),
        grid=(b // nb, nj),
        in_specs=[pl.BlockSpec(b1, fwd5), pl.BlockSpec(b1, bwd5), pl.BlockSpec(b2, fwd5), pl.BlockSpec(b2, bwd5),
                  pl.BlockSpec(bu, fwd4), pl.BlockSpec(bu, bwd4), pl.BlockSpec(bd, fwd4), pl.BlockSpec(bd, bwd4), st],
        out_specs=[pl.BlockSpec((nb, cpg * GDN_CHUNK, GDN_W), lambda bi, j: (bi, j, 0)),
                   pl.BlockSpec((nb, cpg * GDN_CHUNK, GDN_W), lambda bi, j: (bi, nj - 1 - j, 0)), st],
        out_shape=[jax.ShapeDtypeStruct((b, t, GDN_W), BF16), jax.ShapeDtypeStruct((b, t, GDN_W), BF16),
                   jax.ShapeDtypeStruct((b, 2, HEAD_W, GDN_W), F32)],
        scratch_shapes=[pltpu.VMEM((nb, 2, HEAD_W, GDN_W), F32)],
        compiler_params=_cparams("parallel", "arbitrary"),
        name="gdn_scan",
    )(l1, l1, l2, l2, u, u, dec, dec, s0)


FFN_TILE = 256


def _post_kernel(x_ref, of_ref, ob_ref, z_ref, pa_ref, prev_ref, next_ref, pc_ref, mod_ref, cw_ref, lng_ref, lnb_ref,
                 ws_ref, bsm_ref, gon_ref, wo_ref, gqm_ref, gpf_ref, w1_ref, w2_ref, gqf_ref, o_ref,
                 *, nj, grid_conv, hidden):
    j = pl.program_id(1)
    o = of_ref[...].astype(F32) + ob_ref[...].astype(F32)
    z = z_ref[...].astype(F32)
    ybs = []
    for h in range(N_HEADS):
        oh = o[:, h * HEAD_W:(h + 1) * HEAD_W]
        oh = oh * lax.rsqrt(jnp.mean(oh * oh, axis=-1, keepdims=True) + EPS) * gon_ref[...]
        ybs.append((oh * _silu(z[:, h * HEAD_W:(h + 1) * HEAD_W])).astype(BF16))
    r = jnp.dot(jnp.concatenate(ybs, axis=-1), wo_ref[A_W:A_W + GDN_W, :], preferred_element_type=F32)
    pa = pa_ref[...].astype(F32)
    if grid_conv:
        ya = _conv_grid(pa, prev_ref[...].astype(F32), next_ref[...].astype(F32), cw_ref[...],
                        jnp.where(j > 0, 1.0, 0.0), jnp.where(j < nj - 1, 1.0, 0.0))
    else:
        ya = _conv_seq(pa, cw_ref[...])
    r = r + jnp.dot(ya.astype(BF16), wo_ref[:A_W, :], preferred_element_type=F32)
    yc = _sgu(pc_ref[...].astype(F32), lng_ref[...], lnb_ref[...], ws_ref[...], bsm_ref[...])
    r = r + jnp.dot(yc.astype(BF16), wo_ref[A_W + GDN_W:, :], preferred_element_type=F32)
    x = x_ref[...] + _rms(r, gqm_ref[...] * mod_ref[2:3, :])
    hb = (_rms(x, gpf_ref[...] * (1.0 + mod_ref[4:5, :])) + mod_ref[3:4, :]).astype(BF16)
    acc = None
    for c0 in range(0, hidden, FFN_TILE):
        g = jnp.dot(hb, w1_ref[:, c0:c0 + FFN_TILE], preferred_element_type=F32)
        u = jnp.dot(hb, w1_ref[:, hidden + c0:hidden + c0 + FFN_TILE], preferred_element_type=F32)
        part = jnp.dot((_silu(g) * u).astype(BF16), w2_ref[c0:c0 + FFN_TILE, :], preferred_element_type=F32)
        acc = part if acc is None else acc + part
    o_ref[...] = x + _rms(acc, gqf_ref[...] * mod_ref[5:6, :])


def _post(x, of, ob, z, pa, pc, mod, conv_a, ln_g, ln_b, ws, bsm, g_onorm, w_o, gqm, gpf, w1, w2, gqf, *, grid_conv):
    b, t, d = x.shape
    hidden = w2.shape[0]
    tm = _row_tile(t)
    nj = t // tm
    assert grid_conv or nj == 1
    rows = tm // GRID_W
    last = t // GRID_W - 1
    const = lambda bi, j: (0, 0)
    tok = lambda bi, j: (bi, j, 0)
    full = lambda a: pl.BlockSpec(a.shape, const)
    resident = lambda a: pl.BlockSpec(a.shape, const, pipeline_mode=pl.Buffered(1))
    return pl.pallas_call(
        functools.partial(_post_kernel, nj=nj, grid_conv=grid_conv, hidden=hidden),
        grid=(b, nj),
        in_specs=[pl.BlockSpec((None, tm, d), tok),
                  pl.BlockSpec((None, tm, GDN_W), tok), pl.BlockSpec((None, tm, GDN_W), tok),
                  pl.BlockSpec((None, tm, GDN_W), tok),
                  pl.BlockSpec((None, tm, 3 * A_W), tok),
                  pl.BlockSpec((None, GRID_W, 3 * A_W), lambda bi, j: (bi, jnp.maximum(j * rows - 1, 0), 0)),
                  pl.BlockSpec((None, GRID_W, 3 * A_W), lambda bi, j: (bi, jnp.minimum((j + 1) * rows, last), 0)),
                  pl.BlockSpec((None, tm, 2 * C_W), tok),
                  pl.BlockSpec((None, 8, d), lambda bi, j: (bi, 0, 0)),
                  full(conv_a), full(ln_g), full(ln_b), full(ws), full(bsm), full(g_onorm), resident(w_o),
                  full(gqm), full(gpf), resident(w1), resident(w2), full(gqf)],
        out_specs=pl.BlockSpec((None, tm, d), tok),
        out_shape=jax.ShapeDtypeStruct((b, t, d), F32),
        compiler_params=_cparams("parallel", "parallel"),
        name="post_grid" if grid_conv else "post_seq",
    )(x, of, ob, z, pa, pa, pa, pc, mod, conv_a, ln_g, ln_b, ws, bsm, g_onorm, w_o, gqm, gpf, w1, w2, gqf)


OFF_A, OFF_Q, OFF_AB, OFF_Z, OFF_C = 0, 3 * A_W, 3 * A_W + 3 * GDN_W, 3 * A_W + 3 * GDN_W + 16, 3 * A_W + 4 * GDN_W + 16
GATE_PERM = [4 * ((n % 8) // 2) + 2 * (n % 2) + n // 8 for n in range(16)]


def _mod_rows(rows, b, d):
    m = jnp.broadcast_to(rows.reshape(-1, 6, d), (b, 6, d))
    return jnp.pad(m, ((0, 0), (0, 2), (0, 0)))


def kernel(x, c, ctx, c_ctx, w_mod, b_mod, g_pre_mix, g_post_mix, g_pre_ffn, g_post_ffn, w_in, conv_a, conv_qkv,
           a_log, dt_bias, g_onorm, ln_c_g, ln_c_b, w_s, b_s, w_o, w_ffn_in, w_ffn_out):
    b, t, d = x.shape
    depth = w_mod.shape[0]
    cc = jnp.concatenate([c, c_ctx[None, :], jnp.zeros((16 - b - 1, d), F32)], axis=0)
    mod = _modulation(cc, w_mod, b_mod)
    s_zero = jnp.zeros((b, 2, HEAD_W, GDN_W), F32)
    for l in range(depth):
        last = l == depth - 1
        modx = _mod_rows(mod[l, :b], b, d)
        modc = _mod_rows(mod[l, b:b + 1], b, d)
        wl = w_in[l].astype(BF16)
        w_gate = wl[:, OFF_AB:OFF_Z][:, jnp.array(GATE_PERM)]
        ws_in = [wl[:, OFF_Q:OFF_AB], wl[:, OFF_Z:OFF_C], wl[:, OFF_A:OFF_Q], wl[:, OFF_C:],
                 jnp.pad(w_gate, ((0, 0), (0, LANES - 16)))]
        gpm, gqm = g_pre_mix[l][None, :], g_post_mix[l][None, :]
        gpf, gqf = g_pre_ffn[l][None, :], g_post_ffn[l][None, :]
        sgu_w = w_s[l].reshape(C_GROUPS * C_CHUNK, C_CHUNK).astype(BF16)
        sgu_b = jnp.repeat(b_s[l].T, C_GD, axis=1)
        lng, lnb = ln_c_g[l][None, :], ln_c_b[l][None, :]
        wo = w_o[l].astype(BF16)
        w1, w2 = w_ffn_in[l].astype(BF16), w_ffn_out[l].astype(BF16)
        gon = g_onorm[l][None, :]

        def mixer_inputs(stream, m):
            pq, pz, pa, pc, pab = _in_proj(stream, m, gpm, ws_in)
            return pq, pz, pa, pc, _gdn_prep(pq, pab, conv_qkv[l], a_log[l], dt_bias[l])

        def finish(stream, m, of, ob, pz, pa, pc, grid_conv):
            return _post(stream, of, ob, pz, pa, pc, m, conv_a[l], lng, lnb, sgu_w, sgu_b, gon, wo, gqm, gpf, w1, w2,
                         gqf, grid_conv=grid_conv)

        pq, pz, pa, pc, ops = mixer_inputs(ctx, modc)
        of, ob, s_ctx = _gdn_scan(*ops, s_zero)
        if not last:
            ctx = finish(ctx, modc, of, ob, pz, pa, pc, False)
        pq, pz, pa, pc, ops = mixer_inputs(x, modx)
        of, ob, _ = _gdn_scan(*ops, s_ctx)
        x = finish(x, modx, of, ob, pz, pa, pc, True)
    return x
```

```python
import functools

import jax
import jax.numpy as jnp
from jax import lax
from jax.experimental import pallas as pl
from jax.experimental.pallas import tpu as pltpu

F32 = jnp.float32
BF16 = jnp.bfloat16

EPS = 1e-6
GRID_W = 64
N_HEADS = 4
HEAD_W = 128
GDN_W = N_HEADS * HEAD_W
GDN_CHUNK = 64
A_W = 256
A_HORIZ = 128
C_W = 256
C_GROUPS = 4
C_GD = C_W // C_GROUPS
C_CHUNK = 128
LANES = 128
SUBLANES = 8
VMEM_LIMIT = 56 * 1024 * 1024
ROW_TILE = 512
IN_ROW_TILE = 1024
MOD_ROWS = SUBLANES


def _cparams(*sem):
    return pltpu.CompilerParams(dimension_semantics=sem, vmem_limit_bytes=VMEM_LIMIT)


def _dot3(a, b):
    ah, bh = a.astype(BF16), b.astype(BF16)
    al, bl = (a - ah.astype(F32)).astype(BF16), (b - bh.astype(F32)).astype(BF16)
    d = functools.partial(jnp.dot, preferred_element_type=F32)
    return d(ah, bh) + (d(ah, bl) + d(al, bh))


def _rms(x, g):
    return x * lax.rsqrt(jnp.mean(x * x, axis=-1, keepdims=True) + EPS) * g


def _silu(x):
    return x * jax.nn.sigmoid(x)


def _gelu(x):
    return 0.5 * x * (1.0 + lax.erf(x * 0.7071067811865476))


def _softplus(x):
    return jnp.maximum(x, 0.0) + jnp.log1p(jnp.exp(-jnp.abs(x)))


def _mod_kernel(c_ref, w_ref, b_ref, o_ref):
    o_ref[...] = _dot3(_silu(c_ref[...]), w_ref[...]) + b_ref[...]


def _modulation(cc, w_mod, b_mod):
    depth, d, six_d = w_mod.shape
    r = cc.shape[0]
    return pl.pallas_call(
        _mod_kernel,
        grid=(depth, six_d // d),
        in_specs=[pl.BlockSpec((r, d), lambda l, j: (0, 0)),
                  pl.BlockSpec((None, d, d), lambda l, j: (l, 0, j)),
                  pl.BlockSpec((None, 1, d), lambda l, j: (l, 0, j))],
        out_specs=pl.BlockSpec((None, r, d), lambda l, j: (l, 0, j)),
        out_shape=jax.ShapeDtypeStruct((depth, r, six_d), F32),
        compiler_params=_cparams("parallel", "parallel"),
        name="modulation",
    )(cc, w_mod, b_mod.reshape(depth, 1, six_d))


def _in_kernel(x_ref, mod_ref, g_ref, *refs):
    w_refs, o_refs = refs[:len(refs) // 2], refs[len(refs) // 2:]
    h = _rms(x_ref[...], g_ref[...] * (1.0 + mod_ref[1:2, :])) + mod_ref[0:1, :]
    hb = h.astype(BF16)
    for w, o in zip(w_refs, o_refs):
        o[...] = jnp.dot(hb, w[...], preferred_element_type=F32).astype(o.dtype)


def _in_proj(x, mod, g_pre, ws):
    b, t, d = x.shape
    tm = min(t, IN_ROW_TILE)
    widths = [w.shape[1] for w in ws]
    const = lambda bi, j: (0, 0)
    tok = lambda bi, j: (bi, j, 0)
    return pl.pallas_call(
        _in_kernel,
        grid=(b, t // tm),
        in_specs=[pl.BlockSpec((None, tm, d), tok),
                  pl.BlockSpec((None, MOD_ROWS, d), lambda bi, j: (bi, 0, 0)),
                  pl.BlockSpec((1, d), const)]
                 + [pl.BlockSpec((d, n), const) for n in widths],
        out_specs=[pl.BlockSpec((None, tm, n), tok) for n in widths],
        out_shape=[jax.ShapeDtypeStruct((b, t, n), BF16) for n in widths[:-1]]
                  + [jax.ShapeDtypeStruct((b, t, widths[-1]), F32)],
        compiler_params=_cparams("parallel", "parallel"),
        name="in_proj",
    )(x, mod, g_pre, *ws)


def _sgu(pc, lng, lnb, ws, bsm):
    tt = pc.shape[0]
    u = _gelu(pc[:, :C_W])
    v = _gelu(pc[:, C_W:])
    mu = jnp.mean(v, axis=-1, keepdims=True)
    vc = v - mu
    var = jnp.mean(vc * vc, axis=-1, keepdims=True)
    vb = (vc * lax.rsqrt(var + EPS) * lng + lnb).astype(BF16)
    grp = lax.broadcasted_iota(jnp.int32, (C_CHUNK, C_W), 1) // C_GD
    outs = []
    for n in range(tt // C_CHUNK):
        r = jnp.dot(ws, vb[n * C_CHUNK:(n + 1) * C_CHUNK, :], preferred_element_type=F32)
        mixed = r[3 * C_CHUNK:]
        for g in (2, 1, 0):
            mixed = jnp.where(grp == g, r[g * C_CHUNK:(g + 1) * C_CHUNK], mixed)
        outs.append(mixed + bsm)
    return u * jnp.concatenate(outs, axis=0)


def _conv_grid(pa, pv, nv, cw, has_prev, has_next):
    tt = pa.shape[0]
    gate = pa[:, :A_W]
    m = pa[:, A_W:2 * A_W] * pa[:, 2 * A_W:]
    col = lax.broadcasted_iota(jnp.int32, (tt, A_HORIZ), 0) % GRID_W
    mh = m[:, :A_HORIZ]
    left = jnp.where(col == 0, 0.0, pltpu.roll(mh, 1, 0))
    right = jnp.where(col == GRID_W - 1, 0.0, pltpu.roll(mh, tt - 1, 0))
    yh = cw[0:1, :A_HORIZ] * left + cw[1:2, :A_HORIZ] * mh + cw[2:3, :A_HORIZ] * right
    mv = m[:, A_HORIZ:]
    pm = pv[:, A_W + A_HORIZ:2 * A_W] * pv[:, 2 * A_W + A_HORIZ:] * has_prev
    nm = nv[:, A_W + A_HORIZ:2 * A_W] * nv[:, 2 * A_W + A_HORIZ:] * has_next
    up = jnp.concatenate([pm, mv[:tt - GRID_W]], axis=0)
    down = jnp.concatenate([mv[GRID_W:], nm], axis=0)
    yv = cw[0:1, A_HORIZ:] * up + cw[1:2, A_HORIZ:] * mv + cw[2:3, A_HORIZ:] * down
    return gate * jnp.concatenate([yh, yv], axis=-1)


def _conv_seq(pa, cw):
    tt = pa.shape[0]
    m = pa[:, A_W:2 * A_W] * pa[:, 2 * A_W:]
    row = lax.broadcasted_iota(jnp.int32, (tt, A_W), 0)
    left = jnp.where(row == 0, 0.0, pltpu.roll(m, 1, 0))
    right = jnp.where(row == tt - 1, 0.0, pltpu.roll(m, tt - 1, 0))
    return pa[:, :A_W] * (cw[0:1] * left + cw[1:2] * m + cw[2:3] * right)


HALO = 16
PAIRS = N_HEADS // 2
PAIR_W = 2 * HEAD_W
PREP_GROUP = 4


def _block_diag2(y):
    lane = lax.broadcasted_iota(jnp.int32, y.shape, 1)
    return jnp.concatenate([jnp.where(lane < GDN_CHUNK, y, 0.0), jnp.where(lane >= GDN_CHUNK, y, 0.0)],
                           axis=0).astype(BF16)


def _pdot(a, b):
    return jnp.dot(a.astype(BF16), _block_diag2(b), preferred_element_type=F32)


def _unit_tri_inverses(lds, ii, jl):
    xs = [jnp.where(ii == jl, 1.0, 0.0) - jnp.where((ii >> 1) == (jl >> 1), ld, 0.0) for ld in lds]
    for k in range(1, 6):
        mk = ((ii >> (k + 1)) == (jl >> (k + 1))) & ((ii >> k) != (jl >> k))
        ts = [_pdot(jnp.where(mk, ld, 0.0), x) for ld, x in zip(lds, xs)]
        xs = [x - _pdot(x, t) for x, t in zip(xs, ts)]
    return xs


def _gdn_prep_kernel(pq_ref, pqp_ref, pqn_ref, ab_ref, cw_ref, alog_t, dtb_t,
                     l1_ref, l2_ref, u_ref, dec_ref, q_s, k_s, v_s, *, cpt, nt):
    j = pl.program_id(1)
    tt = cpt * GDN_CHUNK
    x = pq_ref[...].astype(F32)
    w3 = 3 * GDN_W
    has_prev = jnp.where(j > 0, 1.0, 0.0)
    has_next = jnp.where(j < nt - 1, 1.0, 0.0)
    prev_row = pqp_ref[HALO - 1:HALO, :].astype(F32) * has_prev
    next_row = pqn_ref[0:1, :].astype(F32) * has_next
    row8 = lax.broadcasted_iota(jnp.int32, (8, w3), 0)
    xm1 = pltpu.roll(x, 1, 0)
    xm1 = jnp.concatenate([jnp.where(row8 == 0, prev_row, xm1[:8]), xm1[8:]], axis=0)
    xp1 = pltpu.roll(x, tt - 1, 0)
    xp1 = jnp.concatenate([xp1[:tt - 8], jnp.where(row8 == 7, next_row, xp1[tt - 8:])], axis=0)
    cw = cw_ref[...]
    s = _silu(cw[0:1] * xm1 + cw[1:2] * x + cw[2:3] * xp1)
    for h in range(N_HEADS):
        lo, hi = h * HEAD_W, (h + 1) * HEAD_W
        qh = s[:, lo:hi]
        q_s[:, lo:hi] = qh * (lax.rsqrt(jnp.sum(qh * qh, axis=-1, keepdims=True) + EPS) * HEAD_W ** -0.5)
        kh = s[:, GDN_W + lo:GDN_W + hi]
        k_s[:, lo:hi] = kh * lax.rsqrt(jnp.sum(kh * kh, axis=-1, keepdims=True) + EPS)
    v_s[...] = s[:, 2 * GDN_W:]

    cc = GDN_CHUNK
    ii = lax.broadcasted_iota(jnp.int32, (cc, LANES), 0)
    lane = lax.broadcasted_iota(jnp.int32, (cc, LANES), 1)
    jl = lane & (cc - 1)
    first_head = lane < cc
    lane_row = lax.broadcasted_iota(jnp.int32, (1, LANES), 1)
    zero_k = jnp.zeros((cc, HEAD_W), BF16)

    gts = []
    for c in range(cpt):
        abn = ab_ref[c * cc:(c + 1) * cc, :]
        gts.append(jnp.concatenate([abn, pltpu.roll(abn, LANES - 8, 1)], axis=0).T[0:8, :])
    gt = jnp.concatenate(gts, axis=0)
    beta_all = jax.nn.sigmoid(gt)
    g_all = -jnp.exp(alog_t[...]) * _softplus(gt + dtb_t[...])
    tok = lax.broadcasted_iota(jnp.int32, gt.shape, 1) & (cc - 1)
    pre = suf = g_all
    sh = 1
    while sh < cc:
        pre = pre + jnp.where(tok >= sh, pltpu.roll(pre, sh, 1), 0.0)
        suf = suf + jnp.where(tok < cc - sh, pltpu.roll(suf, LANES - sh, 1), 0.0)
        sh *= 2
    gam_all = jnp.where((lax.broadcasted_iota(jnp.int32, gt.shape, 0) & 7) < 6, pre, suf)
    gcols = jnp.concatenate([gam_all, jnp.zeros((LANES - 8 * cpt, LANES), F32)], axis=0).T

    def setup(c):
        kc = k_s[c * cc:(c + 1) * cc, :]
        qc = q_s[c * cc:(c + 1) * cc, :]
        vb = v_s[c * cc:(c + 1) * cc, :].astype(BF16)
        kb = kc.astype(BF16)
        pairs = []
        for p in range(PAIRS):
            lo = p * PAIR_W
            k0, k1 = kb[:, lo:lo + HEAD_W], kb[:, lo + HEAD_W:lo + PAIR_W]
            kbd = jnp.concatenate([jnp.concatenate([k0, zero_k], axis=1),
                                   jnp.concatenate([zero_k, k1], axis=1)], axis=0)
            vbd = jnp.concatenate([jnp.concatenate([vb[:, lo:lo + HEAD_W], zero_k], axis=1),
                                   jnp.concatenate([zero_k, vb[:, lo + HEAD_W:lo + PAIR_W]], axis=1)], axis=0)
            kq = lax.dot_general(jnp.concatenate([kb[:, lo:lo + PAIR_W], qc[:, lo:lo + PAIR_W].astype(BF16)], axis=0),
                                 kbd, (((1,), (1,)), ((), ())), preferred_element_type=F32)
            kt = jnp.concatenate([kc[:, lo:lo + HEAD_W], kc[:, lo + HEAD_W:lo + PAIR_W]], axis=0).T
            pairs.append((kbd, vbd, kq[:cc], kq[cc:], kt, qc[:, lo:lo + HEAD_W], qc[:, lo + HEAD_W:lo + PAIR_W]))
        probs = []
        for d in range(2):
            incl = (ii >= jl) if d == 0 else (ii <= jl)
            strict = (ii > jl) if d == 0 else (ii < jl)
            for p in range(PAIRS):
                kbd, vbd, kkt, qk, kt, q0, q1 = pairs[p]
                rb = 8 * c + 2 * d + p
                rg = rb + 4
                grow = gam_all[rg:rg + 1, :]
                brow = beta_all[rb:rb + 1, :]
                gc0 = jnp.broadcast_to(gcols[:cc, rg:rg + 1], (cc, LANES))
                gc1 = jnp.broadcast_to(gcols[cc:, rg:rg + 1], (cc, LANES))
                dm = jnp.exp(jnp.where(incl, jnp.where(first_head, gc0, gc1) - grow, -1e30))
                ld = jnp.where(strict, kkt * dm, 0.0) * brow
                top = jnp.where(incl, qk * dm, 0.0) * brow
                probs.append(dict(c=c, d=d, p=p, ld=ld, top=top, grow=grow, brow=brow, gc0=gc0, gc1=gc1,
                                  kbd=kbd, vbd=vbd, kt=kt, q0=q0, q1=q1))
        return probs

    def emit(pr, x):
        c, d, p, grow, brow = pr["c"], pr["d"], pr["p"], pr["grow"], pr["brow"]
        wk = jnp.dot((x * jnp.exp(grow)).astype(BF16), pr["kbd"], preferred_element_type=F32)
        u_ref[d, c, :, p * PAIR_W:(p + 1) * PAIR_W] = jnp.dot(x.astype(BF16), pr["vbd"],
                                                              preferred_element_type=F32).astype(BF16)
        l1_ref[d, c, 2 * p] = jnp.concatenate([wk[:, :HEAD_W], pr["q0"] * jnp.exp(pr["gc0"])], axis=0).astype(BF16)
        l1_ref[d, c, 2 * p + 1] = jnp.concatenate([wk[:, HEAD_W:], pr["q1"] * jnp.exp(pr["gc1"])], axis=0).astype(BF16)
        last = cc - 1 if d == 0 else 0
        gl0 = grow[:, last:last + 1]
        gl1 = grow[:, cc + last:cc + last + 1]
        tail = jnp.exp(jnp.where(lane_row < cc, gl0, gl1) - grow) * brow
        l2_ref[d, c, p] = jnp.concatenate([pr["top"], pr["kt"] * tail], axis=0).astype(BF16)
        dec_ref[d, c, :, p * PAIR_W:(p + 1) * PAIR_W] = jnp.concatenate(
            [jnp.broadcast_to(jnp.exp(gl0), (1, HEAD_W)), jnp.broadcast_to(jnp.exp(gl1), (1, HEAD_W))], axis=1)

    group = min(cpt, PREP_GROUP)
    for c0 in range(0, cpt, group):
        probs = []
        for c in range(c0, c0 + group):
            probs += setup(c)
        xs = _unit_tri_inverses([pr["ld"] for pr in probs], ii, jl)
        for pr, xi in zip(probs, xs):
            emit(pr, xi)


PREP_CHUNKS = 8


def _gdn_prep(pq, pab, conv_qkv, a_log, dt_bias):
    b, t, w3 = pq.shape
    nc = t // GDN_CHUNK
    cpt = min(nc, PREP_CHUNKS)
    tt = cpt * GDN_CHUNK
    nt = t // tt
    hb = tt // HALO
    def gate_rows(a):
        rows = jnp.repeat(a.reshape(2 * PAIRS, 2), GDN_CHUNK, axis=1)
        return jnp.tile(jnp.concatenate([jnp.zeros_like(rows), rows], axis=0), (cpt, 1))

    alog_t, dtb_t = gate_rows(a_log), gate_rows(dt_bias)
    const = lambda bi, j: (0, 0)
    tok = lambda bi, j: (bi, j, 0)
    return pl.pallas_call(
        functools.partial(_gdn_prep_kernel, cpt=cpt, nt=nt),
        grid=(b, nt),
        in_specs=[pl.BlockSpec((None, tt, w3), tok),
                  pl.BlockSpec((None, HALO, w3), lambda bi, j: (bi, jnp.maximum(j * hb - 1, 0), 0)),
                  pl.BlockSpec((None, HALO, w3), lambda bi, j: (bi, jnp.minimum((j + 1) * hb, t // HALO - 1), 0)),
                  pl.BlockSpec((None, tt, LANES), tok),
                  pl.BlockSpec((3, w3), const),
                  pl.BlockSpec((8 * cpt, LANES), const), pl.BlockSpec((8 * cpt, LANES), const)],
        out_specs=[pl.BlockSpec((None, 2, cpt, N_HEADS, 2 * GDN_CHUNK, HEAD_W), lambda bi, j: (bi, 0, j, 0, 0, 0)),
                   pl.BlockSpec((None, 2, cpt, PAIRS, 3 * GDN_CHUNK, HEAD_W), lambda bi, j: (bi, 0, j, 0, 0, 0)),
                   pl.BlockSpec((None, 2, cpt, GDN_CHUNK, GDN_W), lambda bi, j: (bi, 0, j, 0, 0)),
                   pl.BlockSpec((None, 2, cpt, 1, GDN_W), lambda bi, j: (bi, 0, j, 0, 0))],
        out_shape=[jax.ShapeDtypeStruct((b, 2, nc, N_HEADS, 2 * GDN_CHUNK, HEAD_W), BF16),
                   jax.ShapeDtypeStruct((b, 2, nc, PAIRS, 3 * GDN_CHUNK, HEAD_W), BF16),
                   jax.ShapeDtypeStruct((b, 2, nc, GDN_CHUNK, GDN_W), BF16),
                   jax.ShapeDtypeStruct((b, 2, nc, 1, GDN_W), F32)],
        scratch_shapes=[pltpu.VMEM((tt, GDN_W), F32)] * 3,
        compiler_params=_cparams("parallel", "parallel"),
        name="gdn_prep",
    )(pq, pq, pq, pab, conv_qkv, alog_t, dtb_t)


SCAN_BATCH = 4
SCAN_CHUNKS = 4


def _gdn_scan_kernel(l1f, l1b, l2f, l2b, uf, ub, decf, decb, s0_ref, of_ref, ob_ref, sfin_ref, s_ref,
                     *, cpg, nj, nb):
    j = pl.program_id(1)

    @pl.when(j == 0)
    def _():
        s_ref[...] = s0_ref[...]

    zero = jnp.zeros((GDN_CHUNK, HEAD_W), BF16)

    def step(i, carry):
        chains = []
        for n in range(nb):
            chains.append((n, 0, l1f, l2f, uf, decf, of_ref, i))
            chains.append((n, 1, l1b, l2b, ub, decb, ob_ref, cpg - 1 - i))
        first = []
        for n, d, l1, l2, u, dec, o_ref, c in chains:
            vts, qss = [], []
            for h in range(N_HEADS):
                sh = s_ref[n, d, :, h * HEAD_W:(h + 1) * HEAD_W].astype(BF16)
                r1 = jnp.dot(l1[n, c, h], sh, preferred_element_type=F32)
                vts.append((u[n, c, :, h * HEAD_W:(h + 1) * HEAD_W].astype(F32) - r1[:GDN_CHUNK]).astype(BF16))
                qss.append(r1[GDN_CHUNK:])
            first.append((vts, qss))
        for (n, d, l1, l2, u, dec, o_ref, c), (vts, qss) in zip(chains, first):
            intra = []
            for p in range(N_HEADS // 2):
                rhs = jnp.concatenate([jnp.concatenate([vts[2 * p], zero], axis=-1),
                                       jnp.concatenate([zero, vts[2 * p + 1]], axis=-1)], axis=0)
                r2 = jnp.dot(l2[n, c, p], rhs, preferred_element_type=F32)
                intra.append(r2[:GDN_CHUNK])
                lo, hi = 2 * p * HEAD_W, 2 * (p + 1) * HEAD_W
                s_ref[n, d, :, lo:hi] = s_ref[n, d, :, lo:hi] * dec[n, c][:, lo:hi] + r2[GDN_CHUNK:]
            o = jnp.concatenate(qss, axis=-1) + jnp.concatenate(intra, axis=-1)
            o_ref[n, pl.ds(pl.multiple_of(c * GDN_CHUNK, GDN_CHUNK), GDN_CHUNK), :] = o.astype(o_ref.dtype)
        return carry

    lax.fori_loop(0, cpg, step, 0)

    @pl.when(j == nj - 1)
    def _():
        sfin_ref[...] = s_ref[...]


def _gdn_scan(l1, l2, u, dec, s0):
    b, _, nc = l1.shape[:3]
    cpg = min(nc, SCAN_CHUNKS)
    nj = nc // cpg
    t = nc * GDN_CHUNK
    fwd5 = lambda bi, j: (bi, 0, j, 0, 0, 0)
    bwd5 = lambda bi, j: (bi, 1, nj - 1 - j, 0, 0, 0)
    fwd4 = lambda bi, j: (bi, 0, j, 0, 0)
    bwd4 = lambda bi, j: (bi, 1, nj - 1 - j, 0, 0)
    nb = SCAN_BATCH if b % SCAN_BATCH == 0 else 1
    b1 = (nb, None, cpg, N_HEADS, 2 * GDN_CHUNK, HEAD_W)
    b2 = (nb, None, cpg, N_HEADS // 2, 3 * GDN_CHUNK, HEAD_W)
    bu = (nb, None, cpg, GDN_CHUNK, GDN_W)
    bd = (nb, None, cpg, 1, GDN_W)
    st = pl.BlockSpec((nb, 2, HEAD_W, GDN_W), lambda bi, j: (bi, 0, 0, 0))
    return pl.pallas_call(
        functools.partial(_gdn_scan_kernel, cpg=cpg, nj=nj, nb=nb),
        grid=(b // nb, nj),
        in_specs=[pl.BlockSpec(b1, fwd5), pl.BlockSpec(b1, bwd5), pl.BlockSpec(b2, fwd5), pl.BlockSpec(b2, bwd5),
                  pl.BlockSpec(bu, fwd4), pl.BlockSpec(bu, bwd4), pl.BlockSpec(bd, fwd4), pl.BlockSpec(bd, bwd4), st],
        out_specs=[pl.BlockSpec((nb, cpg * GDN_CHUNK, GDN_W), lambda bi, j: (bi, j, 0)),
                   pl.BlockSpec((nb, cpg * GDN_CHUNK, GDN_W), lambda bi, j: (bi, nj - 1 - j, 0)), st],
        out_shape=[jax.ShapeDtypeStruct((b, t, GDN_W), BF16), jax.ShapeDtypeStruct((b, t, GDN_W), BF16),
                   jax.ShapeDtypeStruct((b, 2, HEAD_W, GDN_W), F32)],
        scratch_shapes=[pltpu.VMEM((nb, 2, HEAD_W, GDN_W), F32)],
        compiler_params=_cparams("parallel", "arbitrary"),
        name="gdn_scan",
    )(l1, l1, l2, l2, u, u, dec, dec, s0)


FFN_TILE = 256


def _post_kernel(x_ref, of_ref, ob_ref, z_ref, pa_ref, prev_ref, next_ref, pc_ref, mod_ref, cw_ref, lng_ref, lnb_ref,
                 ws_ref, bsm_ref, gon_ref, wo_ref, gqm_ref, gpf_ref, w1_ref, w2_ref, gqf_ref, o_ref,
                 *, nj, grid_conv, hidden):
    j = pl.program_id(1)
    o = of_ref[...].astype(F32) + ob_ref[...].astype(F32)
    z = z_ref[...].astype(F32)
    ybs = []
    for h in range(N_HEADS):
        oh = o[:, h * HEAD_W:(h + 1) * HEAD_W]
        oh = oh * lax.rsqrt(jnp.mean(oh * oh, axis=-1, keepdims=True) + EPS) * gon_ref[...]
        ybs.append((oh * _silu(z[:, h * HEAD_W:(h + 1) * HEAD_W])).astype(BF16))
    r = jnp.dot(jnp.concatenate(ybs, axis=-1), wo_ref[A_W:A_W + GDN_W, :], preferred_element_type=F32)
    pa = pa_ref[...].astype(F32)
    if grid_conv:
        ya = _conv_grid(pa, prev_ref[...].astype(F32), next_ref[...].astype(F32), cw_ref[...],
                        jnp.where(j > 0, 1.0, 0.0), jnp.where(j < nj - 1, 1.0, 0.0))
    else:
        ya = _conv_seq(pa, cw_ref[...])
    r = r + jnp.dot(ya.astype(BF16), wo_ref[:A_W, :], preferred_element_type=F32)
    yc = _sgu(pc_ref[...].astype(F32), lng_ref[...], lnb_ref[...], ws_ref[...], bsm_ref[...])
    r = r + jnp.dot(yc.astype(BF16), wo_ref[A_W + GDN_W:, :], preferred_element_type=F32)
    x = x_ref[...] + _rms(r, gqm_ref[...] * mod_ref[2:3, :])
    hb = (_rms(x, gpf_ref[...] * (1.0 + mod_ref[4:5, :])) + mod_ref[3:4, :]).astype(BF16)
    acc = None
    for c0 in range(0, hidden, FFN_TILE):
        g = jnp.dot(hb, w1_ref[:, c0:c0 + FFN_TILE], preferred_element_type=F32)
        u = jnp.dot(hb, w1_ref[:, hidden + c0:hidden + c0 + FFN_TILE], preferred_element_type=F32)
        part = jnp.dot((_silu(g) * u).astype(BF16), w2_ref[c0:c0 + FFN_TILE, :], preferred_element_type=F32)
        acc = part if acc is None else acc + part
    o_ref[...] = x + _rms(acc, gqf_ref[...] * mod_ref[5:6, :])


def _post(x, of, ob, z, pa, pc, mod, conv_a, ln_g, ln_b, ws, bsm, g_onorm, w_o, gqm, gpf, w1, w2, gqf, *, grid_conv):
    b, t, d = x.shape
    hidden = w2.shape[0]
    tm = min(t, ROW_TILE)
    nj = t // tm
    assert grid_conv or nj == 1
    rows = tm // GRID_W
    last = t // GRID_W - 1
    const = lambda bi, j: (0, 0)
    tok = lambda bi, j: (bi, j, 0)
    full = lambda a: pl.BlockSpec(a.shape, const)
    resident = lambda a: pl.BlockSpec(a.shape, const, pipeline_mode=pl.Buffered(1))
    return pl.pallas_call(
        functools.partial(_post_kernel, nj=nj, grid_conv=grid_conv, hidden=hidden),
        grid=(b, nj),
        in_specs=[pl.BlockSpec((None, tm, d), tok),
                  pl.BlockSpec((None, tm, GDN_W), tok), pl.BlockSpec((None, tm, GDN_W), tok),
                  pl.BlockSpec((None, tm, GDN_W), tok),
                  pl.BlockSpec((None, tm, 3 * A_W), tok),
                  pl.BlockSpec((None, GRID_W, 3 * A_W), lambda bi, j: (bi, jnp.maximum(j * rows - 1, 0), 0)),
                  pl.BlockSpec((None, GRID_W, 3 * A_W), lambda bi, j: (bi, jnp.minimum((j + 1) * rows, last), 0)),
                  pl.BlockSpec((None, tm, 2 * C_W), tok),
                  pl.BlockSpec((None, MOD_ROWS, d), lambda bi, j: (bi, 0, 0)),
                  full(conv_a), full(ln_g), full(ln_b), full(ws), full(bsm), full(g_onorm), resident(w_o),
                  full(gqm), full(gpf), resident(w1), resident(w2), full(gqf)],
        out_specs=pl.BlockSpec((None, tm, d), tok),
        out_shape=jax.ShapeDtypeStruct((b, t, d), F32),
        compiler_params=_cparams("parallel", "parallel"),
        name="post_grid" if grid_conv else "post_seq",
    )(x, of, ob, z, pa, pa, pa, pc, mod, conv_a, ln_g, ln_b, ws, bsm, g_onorm, w_o, gqm, gpf, w1, w2, gqf)


OFF_A, OFF_Q, OFF_AB, OFF_Z, OFF_C = 0, 3 * A_W, 3 * A_W + 3 * GDN_W, 3 * A_W + 3 * GDN_W + 16, 3 * A_W + 4 * GDN_W + 16
GATE_PERM = [4 * ((n % 8) // 2) + 2 * (n % 2) + n // 8 for n in range(16)]


def _mod_rows(rows, b, d):
    m = jnp.broadcast_to(rows.reshape(-1, 6, d), (b, 6, d))
    return jnp.pad(m, ((0, 0), (0, MOD_ROWS - 6), (0, 0)))


def kernel(x, c, ctx, c_ctx, w_mod, b_mod, g_pre_mix, g_post_mix, g_pre_ffn, g_post_ffn, w_in, conv_a, conv_qkv,
           a_log, dt_bias, g_onorm, ln_c_g, ln_c_b, w_s, b_s, w_o, w_ffn_in, w_ffn_out):
    b, t, d = x.shape
    depth = w_mod.shape[0]
    pad_rows = -(b + 1) % SUBLANES
    cc = jnp.concatenate([c, c_ctx[None, :], jnp.zeros((pad_rows, d), F32)], axis=0)
    mod = _modulation(cc, w_mod, b_mod)
    s_zero = jnp.zeros((b, 2, HEAD_W, GDN_W), F32)
    for l in range(depth):
        last = l == depth - 1
        modx = _mod_rows(mod[l, :b], b, d)
        modc = _mod_rows(mod[l, b:b + 1], b, d)
        wl = w_in[l].astype(BF16)
        w_gate = wl[:, OFF_AB:OFF_Z][:, jnp.array(GATE_PERM)]
        ws_in = [wl[:, OFF_Q:OFF_AB], wl[:, OFF_Z:OFF_C], wl[:, OFF_A:OFF_Q], wl[:, OFF_C:],
                 jnp.pad(w_gate, ((0, 0), (0, LANES - 16)))]
        gpm, gqm = g_pre_mix[l][None, :], g_post_mix[l][None, :]
        gpf, gqf = g_pre_ffn[l][None, :], g_post_ffn[l][None, :]
        sgu_w = w_s[l].reshape(C_GROUPS * C_CHUNK, C_CHUNK).astype(BF16)
        sgu_b = jnp.repeat(b_s[l].T, C_GD, axis=1)
        lng, lnb = ln_c_g[l][None, :], ln_c_b[l][None, :]
        wo = w_o[l].astype(BF16)
        w1, w2 = w_ffn_in[l].astype(BF16), w_ffn_out[l].astype(BF16)
        gon = g_onorm[l][None, :]

        def mixer_inputs(stream, m):
            pq, pz, pa, pc, pab = _in_proj(stream, m, gpm, ws_in)
            return pq, pz, pa, pc, _gdn_prep(pq, pab, conv_qkv[l], a_log[l], dt_bias[l])

        def finish(stream, m, of, ob, pz, pa, pc, grid_conv):
            return _post(stream, of, ob, pz, pa, pc, m, conv_a[l], lng, lnb, sgu_w, sgu_b, gon, wo, gqm, gpf, w1, w2,
                         gqf, grid_conv=grid_conv)

        if last:
            pq, pab = _in_proj(ctx, modc, gpm, [ws_in[0], ws_in[-1]])
            _, _, s_ctx = _gdn_scan(*_gdn_prep(pq, pab, conv_qkv[l], a_log[l], dt_bias[l]), s_zero)
        else:
            pq, pz, pa, pc, ops = mixer_inputs(ctx, modc)
            of, ob, s_ctx = _gdn_scan(*ops, s_zero)
            ctx = finish(ctx, modc, of, ob, pz, pa, pc, False)
        pq, pz, pa, pc, ops = mixer_inputs(x, modx)
        of, ob, _ = _gdn_scan(*ops, s_ctx)
        x = finish(x, modx, of, ob, pz, pa, pc, True)
    return x
```

```python
import functools

import jax
import jax.numpy as jnp
from jax import lax
from jax.experimental import pallas as pl
from jax.experimental.pallas import tpu as pltpu

F32 = jnp.float32
BF16 = jnp.bfloat16

EPS = 1e-6
GRID_W = 64
N_HEADS = 4
HEAD_W = 128
GDN_W = N_HEADS * HEAD_W
GDN_CHUNK = 64
A_W = 256
A_HORIZ = 128
C_W = 256
C_GROUPS = 4
C_GD = C_W // C_GROUPS
C_CHUNK = 128
LANES = 128
SUBLANES = 8
VMEM_LIMIT = 56 * 1024 * 1024
ROW_TILE = 512
IN_ROW_TILE = 1024
MOD_ROWS = SUBLANES


def _cparams(*sem):
    return pltpu.CompilerParams(dimension_semantics=sem, vmem_limit_bytes=VMEM_LIMIT)


def _dot3(a, b):
    ah, bh = a.astype(BF16), b.astype(BF16)
    al, bl = (a - ah.astype(F32)).astype(BF16), (b - bh.astype(F32)).astype(BF16)
    d = functools.partial(jnp.dot, preferred_element_type=F32)
    return d(ah, bh) + (d(ah, bl) + d(al, bh))


def _rms(x, g):
    return x * lax.rsqrt(jnp.mean(x * x, axis=-1, keepdims=True) + EPS) * g


def _silu(x):
    return x * jax.nn.sigmoid(x)


def _gelu(x):
    return 0.5 * x * (1.0 + lax.erf(x * 0.7071067811865476))


def _softplus(x):
    return jnp.maximum(x, 0.0) + jnp.log1p(jnp.exp(-jnp.abs(x)))


def _mod_kernel(c_ref, w_ref, b_ref, o_ref):
    o_ref[...] = _dot3(_silu(c_ref[...]), w_ref[...]) + b_ref[...]


def _modulation(cc, w_mod, b_mod):
    depth, d, six_d = w_mod.shape
    r = cc.shape[0]
    return pl.pallas_call(
        _mod_kernel,
        grid=(depth, six_d // d),
        in_specs=[pl.BlockSpec((r, d), lambda l, j: (0, 0)),
                  pl.BlockSpec((None, d, d), lambda l, j: (l, 0, j)),
                  pl.BlockSpec((None, 1, d), lambda l, j: (l, 0, j))],
        out_specs=pl.BlockSpec((None, r, d), lambda l, j: (l, 0, j)),
        out_shape=jax.ShapeDtypeStruct((depth, r, six_d), F32),
        compiler_params=_cparams("parallel", "parallel"),
        name="modulation",
    )(cc, w_mod, b_mod.reshape(depth, 1, six_d))


def _in_kernel(x_ref, mod_ref, g_ref, *refs):
    w_refs, o_refs = refs[:len(refs) // 2], refs[len(refs) // 2:]
    h = _rms(x_ref[...], g_ref[...] * (1.0 + mod_ref[1:2, :])) + mod_ref[0:1, :]
    hb = h.astype(BF16)
    for w, o in zip(w_refs, o_refs):
        o[...] = jnp.dot(hb, w[...], preferred_element_type=F32).astype(o.dtype)


def _in_proj(x, mod, g_pre, ws):
    b, t, d = x.shape
    tm = min(t, IN_ROW_TILE)
    widths = [w.shape[1] for w in ws]
    const = lambda bi, j: (0, 0)
    tok = lambda bi, j: (bi, j, 0)
    return pl.pallas_call(
        _in_kernel,
        grid=(b, t // tm),
        in_specs=[pl.BlockSpec((None, tm, d), tok),
                  pl.BlockSpec((None, MOD_ROWS, d), lambda bi, j: (bi, 0, 0)),
                  pl.BlockSpec((1, d), const)]
                 + [pl.BlockSpec((d, n), const) for n in widths],
        out_specs=[pl.BlockSpec((None, tm, n), tok) for n in widths],
        out_shape=[jax.ShapeDtypeStruct((b, t, n), BF16) for n in widths[:-1]]
                  + [jax.ShapeDtypeStruct((b, t, widths[-1]), F32)],
        compiler_params=_cparams("parallel", "parallel"),
        name="in_proj",
    )(x, mod, g_pre, *ws)


def _sgu(pc, lng, lnb, ws, bsm):
    tt = pc.shape[0]
    u = _gelu(pc[:, :C_W])
    v = _gelu(pc[:, C_W:])
    mu = jnp.mean(v, axis=-1, keepdims=True)
    vc = v - mu
    var = jnp.mean(vc * vc, axis=-1, keepdims=True)
    vb = (vc * lax.rsqrt(var + EPS) * lng + lnb).astype(BF16)
    grp = lax.broadcasted_iota(jnp.int32, (C_CHUNK, C_W), 1) // C_GD
    outs = []
    for n in range(tt // C_CHUNK):
        r = jnp.dot(ws, vb[n * C_CHUNK:(n + 1) * C_CHUNK, :], preferred_element_type=F32)
        mixed = r[3 * C_CHUNK:]
        for g in (2, 1, 0):
            mixed = jnp.where(grp == g, r[g * C_CHUNK:(g + 1) * C_CHUNK], mixed)
        outs.append(mixed + bsm)
    return u * jnp.concatenate(outs, axis=0)


def _conv_grid(pa, pv, nv, cw, has_prev, has_next):
    tt = pa.shape[0]
    gate = pa[:, :A_W]
    m = pa[:, A_W:2 * A_W] * pa[:, 2 * A_W:]
    col = lax.broadcasted_iota(jnp.int32, (tt, A_HORIZ), 0) % GRID_W
    mh = m[:, :A_HORIZ]
    left = jnp.where(col == 0, 0.0, pltpu.roll(mh, 1, 0))
    right = jnp.where(col == GRID_W - 1, 0.0, pltpu.roll(mh, tt - 1, 0))
    yh = cw[0:1, :A_HORIZ] * left + cw[1:2, :A_HORIZ] * mh + cw[2:3, :A_HORIZ] * right
    mv = m[:, A_HORIZ:]
    pm = pv[:, A_W + A_HORIZ:2 * A_W] * pv[:, 2 * A_W + A_HORIZ:] * has_prev
    nm = nv[:, A_W + A_HORIZ:2 * A_W] * nv[:, 2 * A_W + A_HORIZ:] * has_next
    up = jnp.concatenate([pm, mv[:tt - GRID_W]], axis=0)
    down = jnp.concatenate([mv[GRID_W:], nm], axis=0)
    yv = cw[0:1, A_HORIZ:] * up + cw[1:2, A_HORIZ:] * mv + cw[2:3, A_HORIZ:] * down
    return gate * jnp.concatenate([yh, yv], axis=-1)


def _conv_seq(pa, cw, seq_len):
    tt = pa.shape[0]
    m = pa[:, A_W:2 * A_W] * pa[:, 2 * A_W:]
    row = lax.broadcasted_iota(jnp.int32, (tt, A_W), 0) % seq_len
    left = jnp.where(row == 0, 0.0, pltpu.roll(m, 1, 0))
    right = jnp.where(row == seq_len - 1, 0.0, pltpu.roll(m, tt - 1, 0))
    return pa[:, :A_W] * (cw[0:1] * left + cw[1:2] * m + cw[2:3] * right)


HALO = 16
PAIRS = N_HEADS // 2
PAIR_W = 2 * HEAD_W
PREP_GROUP = 4


def _block_diag2(y):
    lane = lax.broadcasted_iota(jnp.int32, y.shape, 1)
    return jnp.concatenate([jnp.where(lane < GDN_CHUNK, y, 0.0), jnp.where(lane >= GDN_CHUNK, y, 0.0)],
                           axis=0).astype(BF16)


def _pdot(a, b):
    return jnp.dot(a.astype(BF16), _block_diag2(b), preferred_element_type=F32)


def _unit_tri_inverses(lds, ii, jl):
    xs = [jnp.where(ii == jl, 1.0, 0.0) - jnp.where((ii >> 1) == (jl >> 1), ld, 0.0) for ld in lds]
    for k in range(1, 6):
        mk = ((ii >> (k + 1)) == (jl >> (k + 1))) & ((ii >> k) != (jl >> k))
        ts = [_pdot(jnp.where(mk, ld, 0.0), x) for ld, x in zip(lds, xs)]
        xs = [x - _pdot(x, t) for x, t in zip(xs, ts)]
    return xs


def _gdn_prep_kernel(pq_ref, pqp_ref, pqn_ref, ab_ref, cw_ref, alog_t, dtb_t,
                     l1_ref, l2_ref, u_ref, dec_ref, q_s, k_s, v_s, *, cpt, nt):
    j = pl.program_id(1)
    tt = cpt * GDN_CHUNK
    x = pq_ref[...].astype(F32)
    w3 = 3 * GDN_W
    has_prev = jnp.where(j > 0, 1.0, 0.0)
    has_next = jnp.where(j < nt - 1, 1.0, 0.0)
    prev_row = pqp_ref[HALO - 1:HALO, :].astype(F32) * has_prev
    next_row = pqn_ref[0:1, :].astype(F32) * has_next
    row8 = lax.broadcasted_iota(jnp.int32, (8, w3), 0)
    xm1 = pltpu.roll(x, 1, 0)
    xm1 = jnp.concatenate([jnp.where(row8 == 0, prev_row, xm1[:8]), xm1[8:]], axis=0)
    xp1 = pltpu.roll(x, tt - 1, 0)
    xp1 = jnp.concatenate([xp1[:tt - 8], jnp.where(row8 == 7, next_row, xp1[tt - 8:])], axis=0)
    cw = cw_ref[...]
    s = _silu(cw[0:1] * xm1 + cw[1:2] * x + cw[2:3] * xp1)
    for h in range(N_HEADS):
        lo, hi = h * HEAD_W, (h + 1) * HEAD_W
        qh = s[:, lo:hi]
        q_s[:, lo:hi] = qh * (lax.rsqrt(jnp.sum(qh * qh, axis=-1, keepdims=True) + EPS) * HEAD_W ** -0.5)
        kh = s[:, GDN_W + lo:GDN_W + hi]
        k_s[:, lo:hi] = kh * lax.rsqrt(jnp.sum(kh * kh, axis=-1, keepdims=True) + EPS)
    v_s[...] = s[:, 2 * GDN_W:]

    cc = GDN_CHUNK
    ii = lax.broadcasted_iota(jnp.int32, (cc, LANES), 0)
    lane = lax.broadcasted_iota(jnp.int32, (cc, LANES), 1)
    jl = lane & (cc - 1)
    first_head = lane < cc
    lane_row = lax.broadcasted_iota(jnp.int32, (1, LANES), 1)
    zero_k = jnp.zeros((cc, HEAD_W), BF16)

    gts = []
    for c in range(cpt):
        abn = ab_ref[c * cc:(c + 1) * cc, :]
        gts.append(jnp.concatenate([abn, pltpu.roll(abn, LANES - 8, 1)], axis=0).T[0:8, :])
    gt = jnp.concatenate(gts, axis=0)
    beta_all = jax.nn.sigmoid(gt)
    g_all = -jnp.exp(alog_t[...]) * _softplus(gt + dtb_t[...])
    tok = lax.broadcasted_iota(jnp.int32, gt.shape, 1) & (cc - 1)
    pre = suf = g_all
    sh = 1
    while sh < cc:
        pre = pre + jnp.where(tok >= sh, pltpu.roll(pre, sh, 1), 0.0)
        suf = suf + jnp.where(tok < cc - sh, pltpu.roll(suf, LANES - sh, 1), 0.0)
        sh *= 2
    gam_all = jnp.where((lax.broadcasted_iota(jnp.int32, gt.shape, 0) & 7) < 6, pre, suf)
    gcols = jnp.concatenate([gam_all, jnp.zeros((LANES - 8 * cpt, LANES), F32)], axis=0).T

    def setup(c):
        kc = k_s[c * cc:(c + 1) * cc, :]
        qc = q_s[c * cc:(c + 1) * cc, :]
        vb = v_s[c * cc:(c + 1) * cc, :].astype(BF16)
        kb = kc.astype(BF16)
        pairs = []
        for p in range(PAIRS):
            lo = p * PAIR_W
            k0, k1 = kb[:, lo:lo + HEAD_W], kb[:, lo + HEAD_W:lo + PAIR_W]
            kbd = jnp.concatenate([jnp.concatenate([k0, zero_k], axis=1),
                                   jnp.concatenate([zero_k, k1], axis=1)], axis=0)
            vbd = jnp.concatenate([jnp.concatenate([vb[:, lo:lo + HEAD_W], zero_k], axis=1),
                                   jnp.concatenate([zero_k, vb[:, lo + HEAD_W:lo + PAIR_W]], axis=1)], axis=0)
            kq = lax.dot_general(jnp.concatenate([kb[:, lo:lo + PAIR_W], qc[:, lo:lo + PAIR_W].astype(BF16)], axis=0),
                                 kbd, (((1,), (1,)), ((), ())), preferred_element_type=F32)
            kt = jnp.concatenate([kc[:, lo:lo + HEAD_W], kc[:, lo + HEAD_W:lo + PAIR_W]], axis=0).T
            pairs.append((kbd, vbd, kq[:cc], kq[cc:], kt, qc[:, lo:lo + HEAD_W], qc[:, lo + HEAD_W:lo + PAIR_W]))
        probs = []
        for d in range(2):
            incl = (ii >= jl) if d == 0 else (ii <= jl)
            strict = (ii > jl) if d == 0 else (ii < jl)
            for p in range(PAIRS):
                kbd, vbd, kkt, qk, kt, q0, q1 = pairs[p]
                rb = 8 * c + 2 * d + p
                rg = rb + 4
                grow = gam_all[rg:rg + 1, :]
                brow = beta_all[rb:rb + 1, :]
                gc0 = jnp.broadcast_to(gcols[:cc, rg:rg + 1], (cc, LANES))
                gc1 = jnp.broadcast_to(gcols[cc:, rg:rg + 1], (cc, LANES))
                dm = jnp.exp(jnp.where(incl, jnp.where(first_head, gc0, gc1) - grow, -1e30))
                ld = jnp.where(strict, kkt * dm, 0.0) * brow
                top = jnp.where(incl, qk * dm, 0.0) * brow
                probs.append(dict(c=c, d=d, p=p, ld=ld, top=top, grow=grow, brow=brow, gc0=gc0, gc1=gc1,
                                  kbd=kbd, vbd=vbd, kt=kt, q0=q0, q1=q1))
        return probs

    def emit(pr, x):
        c, d, p, grow, brow = pr["c"], pr["d"], pr["p"], pr["grow"], pr["brow"]
        wk = jnp.dot((x * jnp.exp(grow)).astype(BF16), pr["kbd"], preferred_element_type=F32)
        u_ref[d, c, :, p * PAIR_W:(p + 1) * PAIR_W] = jnp.dot(x.astype(BF16), pr["vbd"],
                                                              preferred_element_type=F32).astype(BF16)
        l1_ref[d, c, 2 * p] = jnp.concatenate([wk[:, :HEAD_W], pr["q0"] * jnp.exp(pr["gc0"])], axis=0).astype(BF16)
        l1_ref[d, c, 2 * p + 1] = jnp.concatenate([wk[:, HEAD_W:], pr["q1"] * jnp.exp(pr["gc1"])], axis=0).astype(BF16)
        last = cc - 1 if d == 0 else 0
        gl0 = grow[:, last:last + 1]
        gl1 = grow[:, cc + last:cc + last + 1]
        tail = jnp.exp(jnp.where(lane_row < cc, gl0, gl1) - grow) * brow
        l2_ref[d, c, p] = jnp.concatenate([pr["top"], pr["kt"] * tail], axis=0).astype(BF16)
        dec_ref[d, c, :, p * PAIR_W:(p + 1) * PAIR_W] = jnp.concatenate(
            [jnp.broadcast_to(jnp.exp(gl0), (1, HEAD_W)), jnp.broadcast_to(jnp.exp(gl1), (1, HEAD_W))], axis=1)

    group = min(cpt, PREP_GROUP)
    for c0 in range(0, cpt, group):
        probs = []
        for c in range(c0, c0 + group):
            probs += setup(c)
        xs = _unit_tri_inverses([pr["ld"] for pr in probs], ii, jl)
        for pr, xi in zip(probs, xs):
            emit(pr, xi)


PREP_CHUNKS = 8


def _gdn_prep(pq, pab, conv_qkv, a_log, dt_bias):
    b, t, w3 = pq.shape
    nc = t // GDN_CHUNK
    cpt = min(nc, PREP_CHUNKS)
    tt = cpt * GDN_CHUNK
    nt = t // tt
    hb = tt // HALO
    def gate_rows(a):
        rows = jnp.repeat(a.reshape(2 * PAIRS, 2), GDN_CHUNK, axis=1)
        return jnp.tile(jnp.concatenate([jnp.zeros_like(rows), rows], axis=0), (cpt, 1))

    alog_t, dtb_t = gate_rows(a_log), gate_rows(dt_bias)
    const = lambda bi, j: (0, 0)
    tok = lambda bi, j: (bi, j, 0)
    return pl.pallas_call(
        functools.partial(_gdn_prep_kernel, cpt=cpt, nt=nt),
        grid=(b, nt),
        in_specs=[pl.BlockSpec((None, tt, w3), tok),
                  pl.BlockSpec((None, HALO, w3), lambda bi, j: (bi, jnp.maximum(j * hb - 1, 0), 0)),
                  pl.BlockSpec((None, HALO, w3), lambda bi, j: (bi, jnp.minimum((j + 1) * hb, t // HALO - 1), 0)),
                  pl.BlockSpec((None, tt, LANES), tok),
                  pl.BlockSpec((3, w3), const),
                  pl.BlockSpec((8 * cpt, LANES), const), pl.BlockSpec((8 * cpt, LANES), const)],
        out_specs=[pl.BlockSpec((None, 2, cpt, N_HEADS, 2 * GDN_CHUNK, HEAD_W), lambda bi, j: (bi, 0, j, 0, 0, 0)),
                   pl.BlockSpec((None, 2, cpt, PAIRS, 3 * GDN_CHUNK, HEAD_W), lambda bi, j: (bi, 0, j, 0, 0, 0)),
                   pl.BlockSpec((None, 2, cpt, GDN_CHUNK, GDN_W), lambda bi, j: (bi, 0, j, 0, 0)),
                   pl.BlockSpec((None, 2, cpt, 1, GDN_W), lambda bi, j: (bi, 0, j, 0, 0))],
        out_shape=[jax.ShapeDtypeStruct((b, 2, nc, N_HEADS, 2 * GDN_CHUNK, HEAD_W), BF16),
                   jax.ShapeDtypeStruct((b, 2, nc, PAIRS, 3 * GDN_CHUNK, HEAD_W), BF16),
                   jax.ShapeDtypeStruct((b, 2, nc, GDN_CHUNK, GDN_W), BF16),
                   jax.ShapeDtypeStruct((b, 2, nc, 1, GDN_W), F32)],
        scratch_shapes=[pltpu.VMEM((tt, GDN_W), F32)] * 3,
        compiler_params=_cparams("parallel", "parallel"),
        name="gdn_prep",
    )(pq, pq, pq, pab, conv_qkv, alog_t, dtb_t)


SCAN_BATCH = 4
SCAN_CHUNKS = 4


def _gdn_scan_kernel(l1f, l1b, l2f, l2b, uf, ub, decf, decb, s0_ref, of_ref, ob_ref, sfin_ref, s_ref,
                     *, cpg, nj, nb):
    j = pl.program_id(1)

    @pl.when(j == 0)
    def _():
        s_ref[...] = s0_ref[...]

    zero = jnp.zeros((GDN_CHUNK, HEAD_W), BF16)

    def step(i, carry):
        chains = []
        for n in range(nb):
            chains.append((n, 0, l1f, l2f, uf, decf, of_ref, i))
            chains.append((n, 1, l1b, l2b, ub, decb, ob_ref, cpg - 1 - i))
        first = []
        for n, d, l1, l2, u, dec, o_ref, c in chains:
            vts, qss = [], []
            for h in range(N_HEADS):
                sh = s_ref[n, d, :, h * HEAD_W:(h + 1) * HEAD_W].astype(BF16)
                r1 = jnp.dot(l1[n, c, h], sh, preferred_element_type=F32)
                vts.append((u[n, c, :, h * HEAD_W:(h + 1) * HEAD_W].astype(F32) - r1[:GDN_CHUNK]).astype(BF16))
                qss.append(r1[GDN_CHUNK:])
            first.append((vts, qss))
        for (n, d, l1, l2, u, dec, o_ref, c), (vts, qss) in zip(chains, first):
            intra = []
            for p in range(N_HEADS // 2):
                rhs = jnp.concatenate([jnp.concatenate([vts[2 * p], zero], axis=-1),
                                       jnp.concatenate([zero, vts[2 * p + 1]], axis=-1)], axis=0)
                r2 = jnp.dot(l2[n, c, p], rhs, preferred_element_type=F32)
                intra.append(r2[:GDN_CHUNK])
                lo, hi = 2 * p * HEAD_W, 2 * (p + 1) * HEAD_W
                s_ref[n, d, :, lo:hi] = s_ref[n, d, :, lo:hi] * dec[n, c][:, lo:hi] + r2[GDN_CHUNK:]
            o = jnp.concatenate(qss, axis=-1) + jnp.concatenate(intra, axis=-1)
            o_ref[n, pl.ds(pl.multiple_of(c * GDN_CHUNK, GDN_CHUNK), GDN_CHUNK), :] = o.astype(o_ref.dtype)
        return carry

    lax.fori_loop(0, cpg, step, 0)

    @pl.when(j == nj - 1)
    def _():
        sfin_ref[...] = s_ref[...]


def _gdn_scan(l1, l2, u, dec, s0):
    b, _, nc = l1.shape[:3]
    cpg = min(nc, SCAN_CHUNKS)
    nj = nc // cpg
    t = nc * GDN_CHUNK
    fwd5 = lambda bi, j: (bi, 0, j, 0, 0, 0)
    bwd5 = lambda bi, j: (bi, 1, nj - 1 - j, 0, 0, 0)
    fwd4 = lambda bi, j: (bi, 0, j, 0, 0)
    bwd4 = lambda bi, j: (bi, 1, nj - 1 - j, 0, 0)
    nb = SCAN_BATCH if b % SCAN_BATCH == 0 else 1
    b1 = (nb, None, cpg, N_HEADS, 2 * GDN_CHUNK, HEAD_W)
    b2 = (nb, None, cpg, N_HEADS // 2, 3 * GDN_CHUNK, HEAD_W)
    bu = (nb, None, cpg, GDN_CHUNK, GDN_W)
    bd = (nb, None, cpg, 1, GDN_W)
    st = pl.BlockSpec((nb, 2, HEAD_W, GDN_W), lambda bi, j: (bi, 0, 0, 0))
    return pl.pallas_call(
        functools.partial(_gdn_scan_kernel, cpg=cpg, nj=nj, nb=nb),
        grid=(b // nb, nj),
        in_specs=[pl.BlockSpec(b1, fwd5), pl.BlockSpec(b1, bwd5), pl.BlockSpec(b2, fwd5), pl.BlockSpec(b2, bwd5),
                  pl.BlockSpec(bu, fwd4), pl.BlockSpec(bu, bwd4), pl.BlockSpec(bd, fwd4), pl.BlockSpec(bd, bwd4), st],
        out_specs=[pl.BlockSpec((nb, cpg * GDN_CHUNK, GDN_W), lambda bi, j: (bi, j, 0)),
                   pl.BlockSpec((nb, cpg * GDN_CHUNK, GDN_W), lambda bi, j: (bi, nj - 1 - j, 0)), st],
        out_shape=[jax.ShapeDtypeStruct((b, t, GDN_W), BF16), jax.ShapeDtypeStruct((b, t, GDN_W), BF16),
                   jax.ShapeDtypeStruct((b, 2, HEAD_W, GDN_W), F32)],
        scratch_shapes=[pltpu.VMEM((nb, 2, HEAD_W, GDN_W), F32)],
        compiler_params=_cparams("parallel", "arbitrary"),
        name="gdn_scan",
    )(l1, l1, l2, l2, u, u, dec, dec, s0)


FFN_TILE = 256


def _post_kernel(x_ref, of_ref, ob_ref, z_ref, pa_ref, prev_ref, next_ref, pc_ref, mod_ref, cw_ref, lng_ref, lnb_ref,
                 ws_ref, bsm_ref, gon_ref, wo_ref, gqm_ref, gpf_ref, w1_ref, w2_ref, gqf_ref, o_ref,
                 *, nj, seq_len, hidden):
    grid_conv = seq_len is None
    j = pl.program_id(1)
    o = of_ref[...].astype(F32) + ob_ref[...].astype(F32)
    z = z_ref[...].astype(F32)
    ybs = []
    for h in range(N_HEADS):
        oh = o[:, h * HEAD_W:(h + 1) * HEAD_W]
        oh = oh * lax.rsqrt(jnp.mean(oh * oh, axis=-1, keepdims=True) + EPS) * gon_ref[...]
        ybs.append((oh * _silu(z[:, h * HEAD_W:(h + 1) * HEAD_W])).astype(BF16))
    r = jnp.dot(jnp.concatenate(ybs, axis=-1), wo_ref[A_W:A_W + GDN_W, :], preferred_element_type=F32)
    pa = pa_ref[...].astype(F32)
    if grid_conv:
        ya = _conv_grid(pa, prev_ref[...].astype(F32), next_ref[...].astype(F32), cw_ref[...],
                        jnp.where(j > 0, 1.0, 0.0), jnp.where(j < nj - 1, 1.0, 0.0))
    else:
        ya = _conv_seq(pa, cw_ref[...], seq_len)
    r = r + jnp.dot(ya.astype(BF16), wo_ref[:A_W, :], preferred_element_type=F32)
    yc = _sgu(pc_ref[...].astype(F32), lng_ref[...], lnb_ref[...], ws_ref[...], bsm_ref[...])
    r = r + jnp.dot(yc.astype(BF16), wo_ref[A_W + GDN_W:, :], preferred_element_type=F32)
    x = x_ref[...] + _rms(r, gqm_ref[...] * mod_ref[2:3, :])
    hb = (_rms(x, gpf_ref[...] * (1.0 + mod_ref[4:5, :])) + mod_ref[3:4, :]).astype(BF16)
    acc = None
    for c0 in range(0, hidden, FFN_TILE):
        g = jnp.dot(hb, w1_ref[:, c0:c0 + FFN_TILE], preferred_element_type=F32)
        u = jnp.dot(hb, w1_ref[:, hidden + c0:hidden + c0 + FFN_TILE], preferred_element_type=F32)
        part = jnp.dot((_silu(g) * u).astype(BF16), w2_ref[c0:c0 + FFN_TILE, :], preferred_element_type=F32)
        acc = part if acc is None else acc + part
    o_ref[...] = x + _rms(acc, gqf_ref[...] * mod_ref[5:6, :])


def _post(x, of, ob, z, pa, pc, mod, conv_a, ln_g, ln_b, ws, bsm, g_onorm, w_o, gqm, gpf, w1, w2, gqf, *, seq_len=None):
    b, t, d = x.shape
    hidden = w2.shape[0]
    tm = min(t, ROW_TILE)
    nj = t // tm
    grid_conv = seq_len is None
    assert grid_conv or tm % seq_len == 0
    rows = tm // GRID_W
    last = t // GRID_W - 1
    const = lambda bi, j: (0, 0)
    tok = lambda bi, j: (bi, j, 0)
    full = lambda a: pl.BlockSpec(a.shape, const)
    resident = lambda a: pl.BlockSpec(a.shape, const, pipeline_mode=pl.Buffered(1))
    return pl.pallas_call(
        functools.partial(_post_kernel, nj=nj, seq_len=seq_len, hidden=hidden),
        grid=(b, nj),
        in_specs=[pl.BlockSpec((None, tm, d), tok),
                  pl.BlockSpec((None, tm, GDN_W), tok), pl.BlockSpec((None, tm, GDN_W), tok),
                  pl.BlockSpec((None, tm, GDN_W), tok),
                  pl.BlockSpec((None, tm, 3 * A_W), tok),
                  pl.BlockSpec((None, GRID_W, 3 * A_W), lambda bi, j: (bi, jnp.maximum(j * rows - 1, 0), 0)),
                  pl.BlockSpec((None, GRID_W, 3 * A_W), lambda bi, j: (bi, jnp.minimum((j + 1) * rows, last), 0)),
                  pl.BlockSpec((None, tm, 2 * C_W), tok),
                  pl.BlockSpec((None, MOD_ROWS, d), lambda bi, j: (bi, 0, 0)),
                  full(conv_a), full(ln_g), full(ln_b), full(ws), full(bsm), full(g_onorm), resident(w_o),
                  full(gqm), full(gpf), resident(w1), resident(w2), full(gqf)],
        out_specs=pl.BlockSpec((None, tm, d), tok),
        out_shape=jax.ShapeDtypeStruct((b, t, d), F32),
        compiler_params=_cparams("parallel", "parallel"),
        name="post_grid" if grid_conv else "post_seq",
    )(x, of, ob, z, pa, pa, pa, pc, mod, conv_a, ln_g, ln_b, ws, bsm, g_onorm, w_o, gqm, gpf, w1, w2, gqf)


OFF_A, OFF_Q, OFF_AB, OFF_Z, OFF_C = 0, 3 * A_W, 3 * A_W + 3 * GDN_W, 3 * A_W + 3 * GDN_W + 16, 3 * A_W + 4 * GDN_W + 16
GATE_PERM = [4 * ((n % 8) // 2) + 2 * (n % 2) + n // 8 for n in range(16)]


def _mod_rows(rows, b, d):
    m = jnp.broadcast_to(rows.reshape(-1, 6, d), (b, 6, d))
    return jnp.pad(m, ((0, 0), (0, MOD_ROWS - 6), (0, 0)))


def kernel(x, c, ctx, c_ctx, w_mod, b_mod, g_pre_mix, g_post_mix, g_pre_ffn, g_post_ffn, w_in, conv_a, conv_qkv,
           a_log, dt_bias, g_onorm, ln_c_g, ln_c_b, w_s, b_s, w_o, w_ffn_in, w_ffn_out):
    b, t, d = x.shape
    depth = w_mod.shape[0]
    pad_rows = -(b + 1) % SUBLANES
    cc = jnp.concatenate([c, c_ctx[None, :], jnp.zeros((pad_rows, d), F32)], axis=0)
    mod = _modulation(cc, w_mod, b_mod)
    s_zero = jnp.zeros((b, 2, HEAD_W, GDN_W), F32)
    tc = ctx.shape[1]
    ctx = ctx.reshape(1, b * tc, d)
    per_seq = lambda a: a.reshape(b, tc, a.shape[-1])
    stacked = lambda a: a.reshape(1, b * tc, a.shape[-1])
    for l in range(depth):
        last = l == depth - 1
        modx = _mod_rows(mod[l, :b], b, d)
        modc = _mod_rows(mod[l, b:b + 1], 1, d)
        wl = w_in[l].astype(BF16)
        w_gate = wl[:, OFF_AB:OFF_Z][:, jnp.array(GATE_PERM)]
        ws_in = [wl[:, OFF_Q:OFF_AB], wl[:, OFF_Z:OFF_C], wl[:, OFF_A:OFF_Q], wl[:, OFF_C:],
                 jnp.pad(w_gate, ((0, 0), (0, LANES - 16)))]
        gpm, gqm = g_pre_mix[l][None, :], g_post_mix[l][None, :]
        gpf, gqf = g_pre_ffn[l][None, :], g_post_ffn[l][None, :]
        sgu_w = w_s[l].reshape(C_GROUPS * C_CHUNK, C_CHUNK).astype(BF16)
        sgu_b = jnp.repeat(b_s[l].T, C_GD, axis=1)
        lng, lnb = ln_c_g[l][None, :], ln_c_b[l][None, :]
        wo = w_o[l].astype(BF16)
        w1, w2 = w_ffn_in[l].astype(BF16), w_ffn_out[l].astype(BF16)
        gon = g_onorm[l][None, :]

        def deltanet(pq, pab, s0):
            return _gdn_scan(*_gdn_prep(pq, pab, conv_qkv[l], a_log[l], dt_bias[l]), s0)

        def finish(stream, m, of, ob, pz, pa, pc, seq_len):
            return _post(stream, of, ob, pz, pa, pc, m, conv_a[l], lng, lnb, sgu_w, sgu_b, gon, wo, gqm, gpf, w1, w2,
                         gqf, seq_len=seq_len)

        if last:
            pq, pab = _in_proj(ctx, modc, gpm, [ws_in[0], ws_in[-1]])
            _, _, s_ctx = deltanet(per_seq(pq), per_seq(pab), s_zero)
        else:
            pq, pz, pa, pc, pab = _in_proj(ctx, modc, gpm, ws_in)
            of, ob, s_ctx = deltanet(per_seq(pq), per_seq(pab), s_zero)
            ctx = finish(ctx, modc, stacked(of), stacked(ob), pz, pa, pc, tc)
        pq, pz, pa, pc, pab = _in_proj(x, modx, gpm, ws_in)
        of, ob, _ = deltanet(pq, pab, s_ctx)
        x = finish(x, modx, of, ob, pz, pa, pc, None)
    return x
```

```python
import functools

import jax
import jax.numpy as jnp
from jax import lax
from jax.experimental import pallas as pl
from jax.experimental.pallas import tpu as pltpu

F32 = jnp.float32
BF16 = jnp.bfloat16

EPS = 1e-6
GRID_W = 64
N_HEADS = 4
HEAD_W = 128
GDN_W = N_HEADS * HEAD_W
GDN_CHUNK = 64
A_W = 256
A_HORIZ = 128
C_W = 256
C_GROUPS = 4
C_GD = C_W // C_GROUPS
C_CHUNK = 128
LANES = 128
SUBLANES = 8
VMEM_LIMIT = 56 * 1024 * 1024
ROW_TILE = 512
IN_ROW_TILE = 1024
MOD_ROWS = SUBLANES


def _cparams(*sem):
    return pltpu.CompilerParams(dimension_semantics=sem, vmem_limit_bytes=VMEM_LIMIT)


def _dot3(a, b):
    ah, bh = a.astype(BF16), b.astype(BF16)
    al, bl = (a - ah.astype(F32)).astype(BF16), (b - bh.astype(F32)).astype(BF16)
    d = functools.partial(jnp.dot, preferred_element_type=F32)
    return d(ah, bh) + (d(ah, bl) + d(al, bh))


def _rms(x, g):
    return x * lax.rsqrt(jnp.mean(x * x, axis=-1, keepdims=True) + EPS) * g


def _silu(x):
    h = 0.5 * x
    return h + h * jnp.tanh(h)


def _gelu(x):
    return 0.5 * x * (1.0 + lax.erf(x * 0.7071067811865476))


def _softplus(x):
    return jnp.maximum(x, 0.0) + jnp.log1p(jnp.exp(-jnp.abs(x)))


def _mod_kernel(c_ref, w_ref, b_ref, o_ref):
    o_ref[...] = _dot3(_silu(c_ref[...]), w_ref[...]) + b_ref[...]


def _modulation(cc, w_mod, b_mod):
    depth, d, six_d = w_mod.shape
    r = cc.shape[0]
    return pl.pallas_call(
        _mod_kernel,
        grid=(depth, six_d // d),
        in_specs=[pl.BlockSpec((r, d), lambda l, j: (0, 0)),
                  pl.BlockSpec((None, d, d), lambda l, j: (l, 0, j)),
                  pl.BlockSpec((None, 1, d), lambda l, j: (l, 0, j))],
        out_specs=pl.BlockSpec((None, r, d), lambda l, j: (l, 0, j)),
        out_shape=jax.ShapeDtypeStruct((depth, r, six_d), F32),
        compiler_params=_cparams("parallel", "parallel"),
        name="modulation",
    )(cc, w_mod, b_mod.reshape(depth, 1, six_d))


def _in_kernel(x_ref, mod_ref, g_ref, *refs):
    w_refs, o_refs = refs[:len(refs) // 2], refs[len(refs) // 2:]
    h = _rms(x_ref[...], g_ref[...] * (1.0 + mod_ref[1:2, :])) + mod_ref[0:1, :]
    hb = h.astype(BF16)
    for w, o in zip(w_refs, o_refs):
        o[...] = jnp.dot(hb, w[...], preferred_element_type=F32).astype(o.dtype)


def _in_proj(x, mod, g_pre, ws):
    b, t, d = x.shape
    tm = min(t, IN_ROW_TILE)
    widths = [w.shape[1] for w in ws]
    const = lambda bi, j: (0, 0)
    tok = lambda bi, j: (bi, j, 0)
    return pl.pallas_call(
        _in_kernel,
        grid=(b, t // tm),
        in_specs=[pl.BlockSpec((None, tm, d), tok),
                  pl.BlockSpec((None, MOD_ROWS, d), lambda bi, j: (bi, 0, 0)),
                  pl.BlockSpec((1, d), const)]
                 + [pl.BlockSpec((d, n), const) for n in widths],
        out_specs=[pl.BlockSpec((None, tm, n), tok) for n in widths],
        out_shape=[jax.ShapeDtypeStruct((b, t, n), BF16) for n in widths[:-1]]
                  + [jax.ShapeDtypeStruct((b, t, widths[-1]), F32)],
        compiler_params=_cparams("parallel", "parallel"),
        name="in_proj",
    )(x, mod, g_pre, *ws)


def _sgu(pc, lng, lnb, ws, bsm):
    tt = pc.shape[0]
    u = _gelu(pc[:, :C_W])
    v = _gelu(pc[:, C_W:])
    mu = jnp.mean(v, axis=-1, keepdims=True)
    vc = v - mu
    var = jnp.mean(vc * vc, axis=-1, keepdims=True)
    vb = (vc * lax.rsqrt(var + EPS) * lng + lnb).astype(BF16)
    grp = lax.broadcasted_iota(jnp.int32, (C_CHUNK, C_W), 1) // C_GD
    outs = []
    for n in range(tt // C_CHUNK):
        r = jnp.dot(ws, vb[n * C_CHUNK:(n + 1) * C_CHUNK, :], preferred_element_type=F32)
        mixed = r[3 * C_CHUNK:]
        for g in (2, 1, 0):
            mixed = jnp.where(grp == g, r[g * C_CHUNK:(g + 1) * C_CHUNK], mixed)
        outs.append(mixed + bsm)
    return u * jnp.concatenate(outs, axis=0)


def _conv_grid(pa, pv, nv, cw, has_prev, has_next):
    tt = pa.shape[0]
    gate = pa[:, :A_W]
    m = pa[:, A_W:2 * A_W] * pa[:, 2 * A_W:]
    col = lax.broadcasted_iota(jnp.int32, (tt, A_HORIZ), 0) % GRID_W
    mh = m[:, :A_HORIZ]
    left = jnp.where(col == 0, 0.0, pltpu.roll(mh, 1, 0))
    right = jnp.where(col == GRID_W - 1, 0.0, pltpu.roll(mh, tt - 1, 0))
    yh = cw[0:1, :A_HORIZ] * left + cw[1:2, :A_HORIZ] * mh + cw[2:3, :A_HORIZ] * right
    mv = m[:, A_HORIZ:]
    pm = pv[:, A_W + A_HORIZ:2 * A_W] * pv[:, 2 * A_W + A_HORIZ:] * has_prev
    nm = nv[:, A_W + A_HORIZ:2 * A_W] * nv[:, 2 * A_W + A_HORIZ:] * has_next
    up = jnp.concatenate([pm, mv[:tt - GRID_W]], axis=0)
    down = jnp.concatenate([mv[GRID_W:], nm], axis=0)
    yv = cw[0:1, A_HORIZ:] * up + cw[1:2, A_HORIZ:] * mv + cw[2:3, A_HORIZ:] * down
    return gate * jnp.concatenate([yh, yv], axis=-1)


def _conv_seq(pa, cw, seq_len):
    tt = pa.shape[0]
    m = pa[:, A_W:2 * A_W] * pa[:, 2 * A_W:]
    row = lax.broadcasted_iota(jnp.int32, (tt, A_W), 0) % seq_len
    left = jnp.where(row == 0, 0.0, pltpu.roll(m, 1, 0))
    right = jnp.where(row == seq_len - 1, 0.0, pltpu.roll(m, tt - 1, 0))
    return pa[:, :A_W] * (cw[0:1] * left + cw[1:2] * m + cw[2:3] * right)


HALO = 16
PAIRS = N_HEADS // 2
PAIR_W = 2 * HEAD_W
PREP_GROUP = 4


def _block_diag2(y):
    lane = lax.broadcasted_iota(jnp.int32, y.shape, 1)
    return jnp.concatenate([jnp.where(lane < GDN_CHUNK, y, 0.0), jnp.where(lane >= GDN_CHUNK, y, 0.0)],
                           axis=0).astype(BF16)


def _pdot(a, b):
    return jnp.dot(a.astype(BF16), _block_diag2(b), preferred_element_type=F32)


def _unit_tri_inverses(lds, ii, jl):
    xs = [jnp.where(ii == jl, 1.0, 0.0) - jnp.where((ii >> 1) == (jl >> 1), ld, 0.0) for ld in lds]
    for k in range(1, 6):
        mk = ((ii >> (k + 1)) == (jl >> (k + 1))) & ((ii >> k) != (jl >> k))
        ts = [_pdot(jnp.where(mk, ld, 0.0), x) for ld, x in zip(lds, xs)]
        xs = [x - _pdot(x, t) for x, t in zip(xs, ts)]
    return xs


def _gdn_prep_kernel(pq_ref, pqp_ref, pqn_ref, ab_ref, cw_ref, alog_t, dtb_t,
                     l1_ref, l2_ref, u_ref, dec_ref, q_s, k_s, v_s, *, cpt, nt):
    j = pl.program_id(1)
    tt = cpt * GDN_CHUNK
    x = pq_ref[...].astype(F32)
    w3 = 3 * GDN_W
    has_prev = jnp.where(j > 0, 1.0, 0.0)
    has_next = jnp.where(j < nt - 1, 1.0, 0.0)
    prev_row = pqp_ref[HALO - 1:HALO, :].astype(F32) * has_prev
    next_row = pqn_ref[0:1, :].astype(F32) * has_next
    row8 = lax.broadcasted_iota(jnp.int32, (8, w3), 0)
    xm1 = pltpu.roll(x, 1, 0)
    xm1 = jnp.concatenate([jnp.where(row8 == 0, prev_row, xm1[:8]), xm1[8:]], axis=0)
    xp1 = pltpu.roll(x, tt - 1, 0)
    xp1 = jnp.concatenate([xp1[:tt - 8], jnp.where(row8 == 7, next_row, xp1[tt - 8:])], axis=0)
    cw = cw_ref[...]
    s = _silu(cw[0:1] * xm1 + cw[1:2] * x + cw[2:3] * xp1)
    for h in range(N_HEADS):
        lo, hi = h * HEAD_W, (h + 1) * HEAD_W
        qh = s[:, lo:hi]
        q_s[:, lo:hi] = qh * (lax.rsqrt(jnp.sum(qh * qh, axis=-1, keepdims=True) + EPS) * HEAD_W ** -0.5)
        kh = s[:, GDN_W + lo:GDN_W + hi]
        k_s[:, lo:hi] = kh * lax.rsqrt(jnp.sum(kh * kh, axis=-1, keepdims=True) + EPS)
    v_s[...] = s[:, 2 * GDN_W:]

    cc = GDN_CHUNK
    ii = lax.broadcasted_iota(jnp.int32, (cc, LANES), 0)
    lane = lax.broadcasted_iota(jnp.int32, (cc, LANES), 1)
    jl = lane & (cc - 1)
    first_head = lane < cc
    lane_row = lax.broadcasted_iota(jnp.int32, (1, LANES), 1)
    zero_k = jnp.zeros((cc, HEAD_W), BF16)

    gts = []
    for c in range(cpt):
        abn = ab_ref[c * cc:(c + 1) * cc, :]
        gts.append(jnp.concatenate([abn, pltpu.roll(abn, LANES - 8, 1)], axis=0).T[0:8, :])
    gt = jnp.concatenate(gts, axis=0)
    beta_all = jax.nn.sigmoid(gt)
    g_all = -jnp.exp(alog_t[...]) * _softplus(gt + dtb_t[...])
    tok = lax.broadcasted_iota(jnp.int32, gt.shape, 1) & (cc - 1)
    pre = suf = g_all
    sh = 1
    while sh < cc:
        pre = pre + jnp.where(tok >= sh, pltpu.roll(pre, sh, 1), 0.0)
        suf = suf + jnp.where(tok < cc - sh, pltpu.roll(suf, LANES - sh, 1), 0.0)
        sh *= 2
    gam_all = jnp.where((lax.broadcasted_iota(jnp.int32, gt.shape, 0) & 7) < 6, pre, suf)
    gcols = jnp.concatenate([gam_all, jnp.zeros((LANES - 8 * cpt, LANES), F32)], axis=0).T

    def setup(c):
        kc = k_s[c * cc:(c + 1) * cc, :]
        qc = q_s[c * cc:(c + 1) * cc, :]
        vb = v_s[c * cc:(c + 1) * cc, :].astype(BF16)
        kb = kc.astype(BF16)
        pairs = []
        for p in range(PAIRS):
            lo = p * PAIR_W
            k0, k1 = kb[:, lo:lo + HEAD_W], kb[:, lo + HEAD_W:lo + PAIR_W]
            kbd = jnp.concatenate([jnp.concatenate([k0, zero_k], axis=1),
                                   jnp.concatenate([zero_k, k1], axis=1)], axis=0)
            vbd = jnp.concatenate([jnp.concatenate([vb[:, lo:lo + HEAD_W], zero_k], axis=1),
                                   jnp.concatenate([zero_k, vb[:, lo + HEAD_W:lo + PAIR_W]], axis=1)], axis=0)
            kq = lax.dot_general(jnp.concatenate([kb[:, lo:lo + PAIR_W], qc[:, lo:lo + PAIR_W].astype(BF16)], axis=0),
                                 kbd, (((1,), (1,)), ((), ())), preferred_element_type=F32)
            kt = jnp.concatenate([kc[:, lo:lo + HEAD_W], kc[:, lo + HEAD_W:lo + PAIR_W]], axis=0).T
            pairs.append((kbd, vbd, kq[:cc], kq[cc:], kt, qc[:, lo:lo + HEAD_W], qc[:, lo + HEAD_W:lo + PAIR_W]))
        probs = []
        for d in range(2):
            incl = (ii >= jl) if d == 0 else (ii <= jl)
            strict = (ii > jl) if d == 0 else (ii < jl)
            for p in range(PAIRS):
                kbd, vbd, kkt, qk, kt, q0, q1 = pairs[p]
                rb = 8 * c + 2 * d + p
                rg = rb + 4
                grow = gam_all[rg:rg + 1, :]
                brow = beta_all[rb:rb + 1, :]
                gc0 = jnp.broadcast_to(gcols[:cc, rg:rg + 1], (cc, LANES))
                gc1 = jnp.broadcast_to(gcols[cc:, rg:rg + 1], (cc, LANES))
                dm = jnp.exp(jnp.where(incl, jnp.where(first_head, gc0, gc1) - grow, -1e30))
                ld = jnp.where(strict, kkt * dm, 0.0) * brow
                top = jnp.where(incl, qk * dm, 0.0) * brow
                probs.append(dict(c=c, d=d, p=p, ld=ld, top=top, grow=grow, brow=brow, gc0=gc0, gc1=gc1,
                                  kbd=kbd, vbd=vbd, kt=kt, q0=q0, q1=q1))
        return probs

    def emit(pr, x):
        c, d, p, grow, brow = pr["c"], pr["d"], pr["p"], pr["grow"], pr["brow"]
        wk = jnp.dot((x * jnp.exp(grow)).astype(BF16), pr["kbd"], preferred_element_type=F32)
        u_ref[d, c, :, p * PAIR_W:(p + 1) * PAIR_W] = jnp.dot(x.astype(BF16), pr["vbd"],
                                                              preferred_element_type=F32).astype(BF16)
        l1_ref[d, c, 2 * p] = jnp.concatenate([wk[:, :HEAD_W], pr["q0"] * jnp.exp(pr["gc0"])], axis=0).astype(BF16)
        l1_ref[d, c, 2 * p + 1] = jnp.concatenate([wk[:, HEAD_W:], pr["q1"] * jnp.exp(pr["gc1"])], axis=0).astype(BF16)
        last = cc - 1 if d == 0 else 0
        gl0 = grow[:, last:last + 1]
        gl1 = grow[:, cc + last:cc + last + 1]
        tail = jnp.exp(jnp.where(lane_row < cc, gl0, gl1) - grow) * brow
        l2_ref[d, c, p] = jnp.concatenate([pr["top"], pr["kt"] * tail], axis=0).astype(BF16)
        dec_ref[d, c, :, p * PAIR_W:(p + 1) * PAIR_W] = jnp.concatenate(
            [jnp.broadcast_to(jnp.exp(gl0), (1, HEAD_W)), jnp.broadcast_to(jnp.exp(gl1), (1, HEAD_W))], axis=1)

    group = min(cpt, PREP_GROUP)
    for c0 in range(0, cpt, group):
        probs = []
        for c in range(c0, c0 + group):
            probs += setup(c)
        xs = _unit_tri_inverses([pr["ld"] for pr in probs], ii, jl)
        for pr, xi in zip(probs, xs):
            emit(pr, xi)


PREP_CHUNKS = 8


def _gdn_prep(pq, pab, conv_qkv, a_log, dt_bias):
    b, t, w3 = pq.shape
    nc = t // GDN_CHUNK
    cpt = min(nc, PREP_CHUNKS)
    tt = cpt * GDN_CHUNK
    nt = t // tt
    hb = tt // HALO
    def gate_rows(a):
        rows = jnp.repeat(a.reshape(2 * PAIRS, 2), GDN_CHUNK, axis=1)
        return jnp.tile(jnp.concatenate([jnp.zeros_like(rows), rows], axis=0), (cpt, 1))

    alog_t, dtb_t = gate_rows(a_log), gate_rows(dt_bias)
    const = lambda bi, j: (0, 0)
    tok = lambda bi, j: (bi, j, 0)
    return pl.pallas_call(
        functools.partial(_gdn_prep_kernel, cpt=cpt, nt=nt),
        grid=(b, nt),
        in_specs=[pl.BlockSpec((None, tt, w3), tok),
                  pl.BlockSpec((None, HALO, w3), lambda bi, j: (bi, jnp.maximum(j * hb - 1, 0), 0)),
                  pl.BlockSpec((None, HALO, w3), lambda bi, j: (bi, jnp.minimum((j + 1) * hb, t // HALO - 1), 0)),
                  pl.BlockSpec((None, tt, LANES), tok),
                  pl.BlockSpec((3, w3), const),
                  pl.BlockSpec((8 * cpt, LANES), const), pl.BlockSpec((8 * cpt, LANES), const)],
        out_specs=[pl.BlockSpec((None, 2, cpt, N_HEADS, 2 * GDN_CHUNK, HEAD_W), lambda bi, j: (bi, 0, j, 0, 0, 0)),
                   pl.BlockSpec((None, 2, cpt, PAIRS, 3 * GDN_CHUNK, HEAD_W), lambda bi, j: (bi, 0, j, 0, 0, 0)),
                   pl.BlockSpec((None, 2, cpt, GDN_CHUNK, GDN_W), lambda bi, j: (bi, 0, j, 0, 0)),
                   pl.BlockSpec((None, 2, cpt, 1, GDN_W), lambda bi, j: (bi, 0, j, 0, 0))],
        out_shape=[jax.ShapeDtypeStruct((b, 2, nc, N_HEADS, 2 * GDN_CHUNK, HEAD_W), BF16),
                   jax.ShapeDtypeStruct((b, 2, nc, PAIRS, 3 * GDN_CHUNK, HEAD_W), BF16),
                   jax.ShapeDtypeStruct((b, 2, nc, GDN_CHUNK, GDN_W), BF16),
                   jax.ShapeDtypeStruct((b, 2, nc, 1, GDN_W), F32)],
        scratch_shapes=[pltpu.VMEM((tt, GDN_W), F32)] * 3,
        compiler_params=_cparams("parallel", "parallel"),
        name="gdn_prep",
    )(pq, pq, pq, pab, conv_qkv, alog_t, dtb_t)


SCAN_BATCH = 4
SCAN_CHUNKS = 4


def _gdn_scan_kernel(l1f, l1b, l2f, l2b, uf, ub, decf, decb, s0_ref, of_ref, ob_ref, sfin_ref, s_ref,
                     *, cpg, nj, nb):
    j = pl.program_id(1)

    @pl.when(j == 0)
    def _():
        s_ref[...] = s0_ref[...]

    zero = jnp.zeros((GDN_CHUNK, HEAD_W), BF16)

    def step(i, carry):
        chains = []
        for n in range(nb):
            chains.append((n, 0, l1f, l2f, uf, decf, of_ref, i))
            chains.append((n, 1, l1b, l2b, ub, decb, ob_ref, cpg - 1 - i))
        first = []
        for n, d, l1, l2, u, dec, o_ref, c in chains:
            vts, qss = [], []
            for h in range(N_HEADS):
                sh = s_ref[n, d, :, h * HEAD_W:(h + 1) * HEAD_W].astype(BF16)
                r1 = jnp.dot(l1[n, c, h], sh, preferred_element_type=F32)
                vts.append((u[n, c, :, h * HEAD_W:(h + 1) * HEAD_W].astype(F32) - r1[:GDN_CHUNK]).astype(BF16))
                qss.append(r1[GDN_CHUNK:])
            first.append((vts, qss))
        for (n, d, l1, l2, u, dec, o_ref, c), (vts, qss) in zip(chains, first):
            intra = []
            for p in range(N_HEADS // 2):
                rhs = jnp.concatenate([jnp.concatenate([vts[2 * p], zero], axis=-1),
                                       jnp.concatenate([zero, vts[2 * p + 1]], axis=-1)], axis=0)
                r2 = jnp.dot(l2[n, c, p], rhs, preferred_element_type=F32)
                intra.append(r2[:GDN_CHUNK])
                lo, hi = 2 * p * HEAD_W, 2 * (p + 1) * HEAD_W
                s_ref[n, d, :, lo:hi] = s_ref[n, d, :, lo:hi] * dec[n, c][:, lo:hi] + r2[GDN_CHUNK:]
            o = jnp.concatenate(qss, axis=-1) + jnp.concatenate(intra, axis=-1)
            o_ref[n, pl.ds(pl.multiple_of(c * GDN_CHUNK, GDN_CHUNK), GDN_CHUNK), :] = o.astype(o_ref.dtype)
        return carry

    lax.fori_loop(0, cpg, step, 0)

    @pl.when(j == nj - 1)
    def _():
        sfin_ref[...] = s_ref[...]


def _gdn_scan(l1, l2, u, dec, s0):
    b, _, nc = l1.shape[:3]
    cpg = min(nc, SCAN_CHUNKS)
    nj = nc // cpg
    t = nc * GDN_CHUNK
    fwd5 = lambda bi, j: (bi, 0, j, 0, 0, 0)
    bwd5 = lambda bi, j: (bi, 1, nj - 1 - j, 0, 0, 0)
    fwd4 = lambda bi, j: (bi, 0, j, 0, 0)
    bwd4 = lambda bi, j: (bi, 1, nj - 1 - j, 0, 0)
    nb = SCAN_BATCH if b % SCAN_BATCH == 0 else 1
    b1 = (nb, None, cpg, N_HEADS, 2 * GDN_CHUNK, HEAD_W)
    b2 = (nb, None, cpg, N_HEADS // 2, 3 * GDN_CHUNK, HEAD_W)
    bu = (nb, None, cpg, GDN_CHUNK, GDN_W)
    bd = (nb, None, cpg, 1, GDN_W)
    st = pl.BlockSpec((nb, 2, HEAD_W, GDN_W), lambda bi, j: (bi, 0, 0, 0))
    return pl.pallas_call(
        functools.partial(_gdn_scan_kernel, cpg=cpg, nj=nj, nb=nb),
        grid=(b // nb, nj),
        in_specs=[pl.BlockSpec(b1, fwd5), pl.BlockSpec(b1, bwd5), pl.BlockSpec(b2, fwd5), pl.BlockSpec(b2, bwd5),
                  pl.BlockSpec(bu, fwd4), pl.BlockSpec(bu, bwd4), pl.BlockSpec(bd, fwd4), pl.BlockSpec(bd, bwd4), st],
        out_specs=[pl.BlockSpec((nb, cpg * GDN_CHUNK, GDN_W), lambda bi, j: (bi, j, 0)),
                   pl.BlockSpec((nb, cpg * GDN_CHUNK, GDN_W), lambda bi, j: (bi, nj - 1 - j, 0)), st],
        out_shape=[jax.ShapeDtypeStruct((b, t, GDN_W), BF16), jax.ShapeDtypeStruct((b, t, GDN_W), BF16),
                   jax.ShapeDtypeStruct((b, 2, HEAD_W, GDN_W), F32)],
        scratch_shapes=[pltpu.VMEM((nb, 2, HEAD_W, GDN_W), F32)],
        compiler_params=_cparams("parallel", "arbitrary"),
        name="gdn_scan",
    )(l1, l1, l2, l2, u, u, dec, dec, s0)


FFN_TILE = 256


def _post_kernel(x_ref, of_ref, ob_ref, z_ref, pa_ref, prev_ref, next_ref, pc_ref, mod_ref, cw_ref, lng_ref, lnb_ref,
                 ws_ref, bsm_ref, gon_ref, wo_ref, gqm_ref, gpf_ref, w1_ref, w2_ref, gqf_ref, o_ref,
                 *, nj, seq_len, hidden):
    grid_conv = seq_len is None
    j = pl.program_id(1)
    o = of_ref[...].astype(F32) + ob_ref[...].astype(F32)
    z = z_ref[...].astype(F32)
    ybs = []
    for h in range(N_HEADS):
        oh = o[:, h * HEAD_W:(h + 1) * HEAD_W]
        oh = oh * lax.rsqrt(jnp.mean(oh * oh, axis=-1, keepdims=True) + EPS) * gon_ref[...]
        ybs.append((oh * _silu(z[:, h * HEAD_W:(h + 1) * HEAD_W])).astype(BF16))
    r = jnp.dot(jnp.concatenate(ybs, axis=-1), wo_ref[A_W:A_W + GDN_W, :], preferred_element_type=F32)
    pa = pa_ref[...].astype(F32)
    if grid_conv:
        ya = _conv_grid(pa, prev_ref[...].astype(F32), next_ref[...].astype(F32), cw_ref[...],
                        jnp.where(j > 0, 1.0, 0.0), jnp.where(j < nj - 1, 1.0, 0.0))
    else:
        ya = _conv_seq(pa, cw_ref[...], seq_len)
    r = r + jnp.dot(ya.astype(BF16), wo_ref[:A_W, :], preferred_element_type=F32)
    yc = _sgu(pc_ref[...].astype(F32), lng_ref[...], lnb_ref[...], ws_ref[...], bsm_ref[...])
    r = r + jnp.dot(yc.astype(BF16), wo_ref[A_W + GDN_W:, :], preferred_element_type=F32)
    x = x_ref[...] + _rms(r, gqm_ref[...] * mod_ref[2:3, :])
    hb = (_rms(x, gpf_ref[...] * (1.0 + mod_ref[4:5, :])) + mod_ref[3:4, :]).astype(BF16)
    acc = None
    for c0 in range(0, hidden, FFN_TILE):
        g = jnp.dot(hb, w1_ref[:, c0:c0 + FFN_TILE], preferred_element_type=F32)
        u = jnp.dot(hb, w1_ref[:, hidden + c0:hidden + c0 + FFN_TILE], preferred_element_type=F32)
        part = jnp.dot((_silu(g) * u).astype(BF16), w2_ref[c0:c0 + FFN_TILE, :], preferred_element_type=F32)
        acc = part if acc is None else acc + part
    o_ref[...] = x + _rms(acc, gqf_ref[...] * mod_ref[5:6, :])


def _post(x, of, ob, z, pa, pc, mod, conv_a, ln_g, ln_b, ws, bsm, g_onorm, w_o, gqm, gpf, w1, w2, gqf, *, seq_len=None):
    b, t, d = x.shape
    hidden = w2.shape[0]
    tm = min(t, ROW_TILE)
    nj = t // tm
    grid_conv = seq_len is None
    assert grid_conv or tm % seq_len == 0
    rows = tm // GRID_W
    last = t // GRID_W - 1
    const = lambda bi, j: (0, 0)
    tok = lambda bi, j: (bi, j, 0)
    full = lambda a: pl.BlockSpec(a.shape, const)
    resident = lambda a: pl.BlockSpec(a.shape, const, pipeline_mode=pl.Buffered(1))
    return pl.pallas_call(
        functools.partial(_post_kernel, nj=nj, seq_len=seq_len, hidden=hidden),
        grid=(b, nj),
        in_specs=[pl.BlockSpec((None, tm, d), tok),
                  pl.BlockSpec((None, tm, GDN_W), tok), pl.BlockSpec((None, tm, GDN_W), tok),
                  pl.BlockSpec((None, tm, GDN_W), tok),
                  pl.BlockSpec((None, tm, 3 * A_W), tok),
                  pl.BlockSpec((None, GRID_W, 3 * A_W), lambda bi, j: (bi, jnp.maximum(j * rows - 1, 0), 0)),
                  pl.BlockSpec((None, GRID_W, 3 * A_W), lambda bi, j: (bi, jnp.minimum((j + 1) * rows, last), 0)),
                  pl.BlockSpec((None, tm, 2 * C_W), tok),
                  pl.BlockSpec((None, MOD_ROWS, d), lambda bi, j: (bi, 0, 0)),
                  full(conv_a), full(ln_g), full(ln_b), full(ws), full(bsm), full(g_onorm), resident(w_o),
                  full(gqm), full(gpf), resident(w1), resident(w2), full(gqf)],
        out_specs=pl.BlockSpec((None, tm, d), tok),
        out_shape=jax.ShapeDtypeStruct((b, t, d), F32),
        compiler_params=_cparams("parallel", "parallel"),
        name="post_grid" if grid_conv else "post_seq",
    )(x, of, ob, z, pa, pa, pa, pc, mod, conv_a, ln_g, ln_b, ws, bsm, g_onorm, w_o, gqm, gpf, w1, w2, gqf)


OFF_A, OFF_Q, OFF_AB, OFF_Z, OFF_C = 0, 3 * A_W, 3 * A_W + 3 * GDN_W, 3 * A_W + 3 * GDN_W + 16, 3 * A_W + 4 * GDN_W + 16
GATE_PERM = [4 * ((n % 8) // 2) + 2 * (n % 2) + n // 8 for n in range(16)]


def _mod_rows(rows, b, d):
    m = jnp.broadcast_to(rows.reshape(-1, 6, d), (b, 6, d))
    return jnp.pad(m, ((0, 0), (0, MOD_ROWS - 6), (0, 0)))


def kernel(x, c, ctx, c_ctx, w_mod, b_mod, g_pre_mix, g_post_mix, g_pre_ffn, g_post_ffn, w_in, conv_a, conv_qkv,
           a_log, dt_bias, g_onorm, ln_c_g, ln_c_b, w_s, b_s, w_o, w_ffn_in, w_ffn_out):
    b, t, d = x.shape
    depth = w_mod.shape[0]
    pad_rows = -(b + 1) % SUBLANES
    cc = jnp.concatenate([c, c_ctx[None, :], jnp.zeros((pad_rows, d), F32)], axis=0)
    mod = _modulation(cc, w_mod, b_mod)
    s_zero = jnp.zeros((b, 2, HEAD_W, GDN_W), F32)
    tc = ctx.shape[1]
    ctx = ctx.reshape(1, b * tc, d)
    per_seq = lambda a: a.reshape(b, tc, a.shape[-1])
    stacked = lambda a: a.reshape(1, b * tc, a.shape[-1])
    for l in range(depth):
        last = l == depth - 1
        modx = _mod_rows(mod[l, :b], b, d)
        modc = _mod_rows(mod[l, b:b + 1], 1, d)
        wl = w_in[l].astype(BF16)
        w_gate = wl[:, OFF_AB:OFF_Z][:, jnp.array(GATE_PERM)]
        ws_in = [wl[:, OFF_Q:OFF_AB], wl[:, OFF_Z:OFF_C], wl[:, OFF_A:OFF_Q], wl[:, OFF_C:],
                 jnp.pad(w_gate, ((0, 0), (0, LANES - 16)))]
        gpm, gqm = g_pre_mix[l][None, :], g_post_mix[l][None, :]
        gpf, gqf = g_pre_ffn[l][None, :], g_post_ffn[l][None, :]
        sgu_w = w_s[l].reshape(C_GROUPS * C_CHUNK, C_CHUNK).astype(BF16)
        sgu_b = jnp.repeat(b_s[l].T, C_GD, axis=1)
        lng, lnb = ln_c_g[l][None, :], ln_c_b[l][None, :]
        wo = w_o[l].astype(BF16)
        w1, w2 = w_ffn_in[l].astype(BF16), w_ffn_out[l].astype(BF16)
        gon = g_onorm[l][None, :]

        def deltanet(pq, pab, s0):
            return _gdn_scan(*_gdn_prep(pq, pab, conv_qkv[l], a_log[l], dt_bias[l]), s0)

        def finish(stream, m, of, ob, pz, pa, pc, seq_len):
            return _post(stream, of, ob, pz, pa, pc, m, conv_a[l], lng, lnb, sgu_w, sgu_b, gon, wo, gqm, gpf, w1, w2,
                         gqf, seq_len=seq_len)

        if last:
            pq, pab = _in_proj(ctx, modc, gpm, [ws_in[0], ws_in[-1]])
            _, _, s_ctx = deltanet(per_seq(pq), per_seq(pab), s_zero)
        else:
            pq, pz, pa, pc, pab = _in_proj(ctx, modc, gpm, ws_in)
            of, ob, s_ctx = deltanet(per_seq(pq), per_seq(pab), s_zero)
            ctx = finish(ctx, modc, stacked(of), stacked(ob), pz, pa, pc, tc)
        pq, pz, pa, pc, pab = _in_proj(x, modx, gpm, ws_in)
        of, ob, _ = deltanet(pq, pab, s_ctx)
        x = finish(x, modx, of, ob, pz, pa, pc, None)
    return x
```

```python
import functools

import jax
import jax.numpy as jnp
from jax import lax
from jax.experimental import pallas as pl
from jax.experimental.pallas import tpu as pltpu

F32 = jnp.float32
BF16 = jnp.bfloat16

EPS = 1e-6
GRID_W = 64
N_HEADS = 4
HEAD_W = 128
GDN_W = N_HEADS * HEAD_W
GDN_CHUNK = 64
A_W = 256
A_HORIZ = 128
C_W = 256
C_GROUPS = 4
C_GD = C_W // C_GROUPS
C_CHUNK = 128
LANES = 128
SUBLANES = 8
VMEM_LIMIT = 56 * 1024 * 1024
ROW_TILE = 512
IN_ROW_TILE = 1024
MOD_ROWS = SUBLANES


def _cparams(*sem):
    return pltpu.CompilerParams(dimension_semantics=sem, vmem_limit_bytes=VMEM_LIMIT)


def _dot3(a, b):
    ah, bh = a.astype(BF16), b.astype(BF16)
    al, bl = (a - ah.astype(F32)).astype(BF16), (b - bh.astype(F32)).astype(BF16)
    d = functools.partial(jnp.dot, preferred_element_type=F32)
    return d(ah, bh) + (d(ah, bl) + d(al, bh))


def _rms(x, g):
    return x * lax.rsqrt(jnp.mean(x * x, axis=-1, keepdims=True) + EPS) * g


def _silu(x):
    h = 0.5 * x
    return h + h * jnp.tanh(h)


def _gelu(x):
    return 0.5 * x * (1.0 + lax.erf(x * 0.7071067811865476))


def _softplus(x):
    return jnp.maximum(x, 0.0) + jnp.log1p(jnp.exp(-jnp.abs(x)))


def _mod_kernel(c_ref, w_ref, b_ref, o_ref):
    o_ref[...] = _dot3(_silu(c_ref[...]), w_ref[...]) + b_ref[...]


def _modulation(cc, w_mod, b_mod):
    depth, d, six_d = w_mod.shape
    r = cc.shape[0]
    return pl.pallas_call(
        _mod_kernel,
        grid=(depth, six_d // d),
        in_specs=[pl.BlockSpec((r, d), lambda l, j: (0, 0)),
                  pl.BlockSpec((None, d, d), lambda l, j: (l, 0, j)),
                  pl.BlockSpec((None, 1, d), lambda l, j: (l, 0, j))],
        out_specs=pl.BlockSpec((None, r, d), lambda l, j: (l, 0, j)),
        out_shape=jax.ShapeDtypeStruct((depth, r, six_d), F32),
        compiler_params=_cparams("parallel", "parallel"),
        name="modulation",
    )(cc, w_mod, b_mod.reshape(depth, 1, six_d))


def _in_kernel(x_ref, mod_ref, g_ref, *refs):
    w_refs, o_refs = refs[:len(refs) // 2], refs[len(refs) // 2:]
    h = _rms(x_ref[...], g_ref[...] * (1.0 + mod_ref[1:2, :])) + mod_ref[0:1, :]
    hb = h.astype(BF16)
    for w, o in zip(w_refs, o_refs):
        o[...] = jnp.dot(hb, w[...], preferred_element_type=F32).astype(o.dtype)


def _in_proj(x, mod, g_pre, ws):
    b, t, d = x.shape
    tm = min(t, IN_ROW_TILE)
    widths = [w.shape[1] for w in ws]
    const = lambda bi, j: (0, 0)
    tok = lambda bi, j: (bi, j, 0)
    return pl.pallas_call(
        _in_kernel,
        grid=(b, t // tm),
        in_specs=[pl.BlockSpec((None, tm, d), tok),
                  pl.BlockSpec((None, MOD_ROWS, d), lambda bi, j: (bi, 0, 0)),
                  pl.BlockSpec((1, d), const)]
                 + [pl.BlockSpec((d, n), const) for n in widths],
        out_specs=[pl.BlockSpec((None, tm, n), tok) for n in widths],
        out_shape=[jax.ShapeDtypeStruct((b, t, n), BF16) for n in widths[:-1]]
                  + [jax.ShapeDtypeStruct((b, t, widths[-1]), F32)],
        compiler_params=_cparams("parallel", "parallel"),
        name="in_proj",
    )(x, mod, g_pre, *ws)


def _sgu(pc, lng, lnb, ws, bsm):
    tt = pc.shape[0]
    u = _gelu(pc[:, :C_W])
    v = _gelu(pc[:, C_W:])
    mu = jnp.mean(v, axis=-1, keepdims=True)
    vc = v - mu
    var = jnp.mean(vc * vc, axis=-1, keepdims=True)
    vb = (vc * lax.rsqrt(var + EPS) * lng + lnb).astype(BF16)
    grp = lax.broadcasted_iota(jnp.int32, (C_CHUNK, C_W), 1) // C_GD
    outs = []
    for n in range(tt // C_CHUNK):
        r = jnp.dot(ws, vb[n * C_CHUNK:(n + 1) * C_CHUNK, :], preferred_element_type=F32)
        mixed = r[3 * C_CHUNK:]
        for g in (2, 1, 0):
            mixed = jnp.where(grp == g, r[g * C_CHUNK:(g + 1) * C_CHUNK], mixed)
        outs.append(mixed + bsm)
    return u * jnp.concatenate(outs, axis=0)


def _conv_grid(pa, pv, nv, cw, has_prev, has_next):
    tt = pa.shape[0]
    gate = pa[:, :A_W]
    m = pa[:, A_W:2 * A_W] * pa[:, 2 * A_W:]
    col = lax.broadcasted_iota(jnp.int32, (tt, A_HORIZ), 0) % GRID_W
    mh = m[:, :A_HORIZ]
    left = jnp.where(col == 0, 0.0, pltpu.roll(mh, 1, 0))
    right = jnp.where(col == GRID_W - 1, 0.0, pltpu.roll(mh, tt - 1, 0))
    yh = cw[0:1, :A_HORIZ] * left + cw[1:2, :A_HORIZ] * mh + cw[2:3, :A_HORIZ] * right
    mv = m[:, A_HORIZ:]
    pm = pv[:, A_W + A_HORIZ:2 * A_W] * pv[:, 2 * A_W + A_HORIZ:] * has_prev
    nm = nv[:, A_W + A_HORIZ:2 * A_W] * nv[:, 2 * A_W + A_HORIZ:] * has_next
    up = jnp.concatenate([pm, mv[:tt - GRID_W]], axis=0)
    down = jnp.concatenate([mv[GRID_W:], nm], axis=0)
    yv = cw[0:1, A_HORIZ:] * up + cw[1:2, A_HORIZ:] * mv + cw[2:3, A_HORIZ:] * down
    return gate * jnp.concatenate([yh, yv], axis=-1)


def _conv_seq(pa, cw, seq_len):
    tt = pa.shape[0]
    m = pa[:, A_W:2 * A_W] * pa[:, 2 * A_W:]
    row = lax.broadcasted_iota(jnp.int32, (tt, A_W), 0) % seq_len
    left = jnp.where(row == 0, 0.0, pltpu.roll(m, 1, 0))
    right = jnp.where(row == seq_len - 1, 0.0, pltpu.roll(m, tt - 1, 0))
    return pa[:, :A_W] * (cw[0:1] * left + cw[1:2] * m + cw[2:3] * right)


HALO = 16
PAIRS = N_HEADS // 2
PAIR_W = 2 * HEAD_W
PREP_GROUP = 4


def _block_diag2(y):
    lane = lax.broadcasted_iota(jnp.int32, y.shape, 1)
    return jnp.concatenate([jnp.where(lane < GDN_CHUNK, y, 0.0), jnp.where(lane >= GDN_CHUNK, y, 0.0)],
                           axis=0).astype(BF16)


def _pdot(a, b):
    return jnp.dot(a.astype(BF16), _block_diag2(b), preferred_element_type=F32)


def _unit_tri_inverses(lds, ii, jl):
    xs = [jnp.where(ii == jl, 1.0, 0.0) - jnp.where((ii >> 1) == (jl >> 1), ld, 0.0) for ld in lds]
    for k in range(1, 6):
        mk = ((ii >> (k + 1)) == (jl >> (k + 1))) & ((ii >> k) != (jl >> k))
        ts = [_pdot(jnp.where(mk, ld, 0.0), x) for ld, x in zip(lds, xs)]
        xs = [x - _pdot(x, t) for x, t in zip(xs, ts)]
    return xs


def _gdn_prep_kernel(pq_ref, pqp_ref, pqn_ref, ab_ref, cw_ref, alog_t, dtb_t,
                     l1_ref, l2_ref, u_ref, dec_ref, q_s, k_s, v_s, *, cpt, nt):
    j = pl.program_id(1)
    tt = cpt * GDN_CHUNK
    x = pq_ref[...].astype(F32)
    w3 = 3 * GDN_W
    has_prev = jnp.where(j > 0, 1.0, 0.0)
    has_next = jnp.where(j < nt - 1, 1.0, 0.0)
    prev_row = pqp_ref[HALO - 1:HALO, :].astype(F32) * has_prev
    next_row = pqn_ref[0:1, :].astype(F32) * has_next
    row8 = lax.broadcasted_iota(jnp.int32, (8, w3), 0)
    xm1 = pltpu.roll(x, 1, 0)
    xm1 = jnp.concatenate([jnp.where(row8 == 0, prev_row, xm1[:8]), xm1[8:]], axis=0)
    xp1 = pltpu.roll(x, tt - 1, 0)
    xp1 = jnp.concatenate([xp1[:tt - 8], jnp.where(row8 == 7, next_row, xp1[tt - 8:])], axis=0)
    cw = 0.5 * cw_ref[...]
    hs = cw[0:1] * xm1 + cw[1:2] * x + cw[2:3] * xp1
    s = hs + hs * jnp.tanh(hs)
    for h in range(N_HEADS):
        lo, hi = h * HEAD_W, (h + 1) * HEAD_W
        qh = s[:, lo:hi]
        q_s[:, lo:hi] = qh * (lax.rsqrt(jnp.sum(qh * qh, axis=-1, keepdims=True) + EPS) * HEAD_W ** -0.5)
        kh = s[:, GDN_W + lo:GDN_W + hi]
        k_s[:, lo:hi] = kh * lax.rsqrt(jnp.sum(kh * kh, axis=-1, keepdims=True) + EPS)
    v_s[...] = s[:, 2 * GDN_W:]

    cc = GDN_CHUNK
    ii = lax.broadcasted_iota(jnp.int32, (cc, LANES), 0)
    lane = lax.broadcasted_iota(jnp.int32, (cc, LANES), 1)
    jl = lane & (cc - 1)
    first_head = lane < cc
    lane_row = lax.broadcasted_iota(jnp.int32, (1, LANES), 1)
    zero_k = jnp.zeros((cc, HEAD_W), BF16)

    gts = []
    for c in range(cpt):
        abn = ab_ref[c * cc:(c + 1) * cc, :]
        gts.append(jnp.concatenate([abn, pltpu.roll(abn, LANES - 8, 1)], axis=0).T[0:8, :])
    gt = jnp.concatenate(gts, axis=0)
    beta_all = jax.nn.sigmoid(gt)
    g_all = -jnp.exp(alog_t[...]) * _softplus(gt + dtb_t[...])
    tok = lax.broadcasted_iota(jnp.int32, gt.shape, 1) & (cc - 1)
    pre = suf = g_all
    sh = 1
    while sh < cc:
        pre = pre + jnp.where(tok >= sh, pltpu.roll(pre, sh, 1), 0.0)
        suf = suf + jnp.where(tok < cc - sh, pltpu.roll(suf, LANES - sh, 1), 0.0)
        sh *= 2
    gam_all = jnp.where((lax.broadcasted_iota(jnp.int32, gt.shape, 0) & 7) < 6, pre, suf)
    gcols = jnp.concatenate([gam_all, jnp.zeros((LANES - 8 * cpt, LANES), F32)], axis=0).T

    def setup(c):
        kc = k_s[c * cc:(c + 1) * cc, :]
        qc = q_s[c * cc:(c + 1) * cc, :]
        vb = v_s[c * cc:(c + 1) * cc, :].astype(BF16)
        kb = kc.astype(BF16)
        pairs = []
        for p in range(PAIRS):
            lo = p * PAIR_W
            k0, k1 = kb[:, lo:lo + HEAD_W], kb[:, lo + HEAD_W:lo + PAIR_W]
            kbd = jnp.concatenate([jnp.concatenate([k0, zero_k], axis=1),
                                   jnp.concatenate([zero_k, k1], axis=1)], axis=0)
            vbd = jnp.concatenate([jnp.concatenate([vb[:, lo:lo + HEAD_W], zero_k], axis=1),
                                   jnp.concatenate([zero_k, vb[:, lo + HEAD_W:lo + PAIR_W]], axis=1)], axis=0)
            kq = lax.dot_general(jnp.concatenate([kb[:, lo:lo + PAIR_W], qc[:, lo:lo + PAIR_W].astype(BF16)], axis=0),
                                 kbd, (((1,), (1,)), ((), ())), preferred_element_type=F32)
            kt = jnp.concatenate([kc[:, lo:lo + HEAD_W], kc[:, lo + HEAD_W:lo + PAIR_W]], axis=0).T
            pairs.append((kbd, vbd, kq[:cc], kq[cc:], kt, qc[:, lo:lo + HEAD_W], qc[:, lo + HEAD_W:lo + PAIR_W]))
        probs = []
        for d in range(2):
            incl = (ii >= jl) if d == 0 else (ii <= jl)
            strict = (ii > jl) if d == 0 else (ii < jl)
            for p in range(PAIRS):
                kbd, vbd, kkt, qk, kt, q0, q1 = pairs[p]
                rb = 8 * c + 2 * d + p
                rg = rb + 4
                grow = gam_all[rg:rg + 1, :]
                brow = beta_all[rb:rb + 1, :]
                gc0 = jnp.broadcast_to(gcols[:cc, rg:rg + 1], (cc, LANES))
                gc1 = jnp.broadcast_to(gcols[cc:, rg:rg + 1], (cc, LANES))
                dm = jnp.exp(jnp.where(incl, jnp.where(first_head, gc0, gc1) - grow, -1e30))
                ld = jnp.where(strict, kkt * dm, 0.0) * brow
                top = jnp.where(incl, qk * dm, 0.0) * brow
                probs.append(dict(c=c, d=d, p=p, ld=ld, top=top, grow=grow, brow=brow, gc0=gc0, gc1=gc1,
                                  kbd=kbd, vbd=vbd, kt=kt, q0=q0, q1=q1))
        return probs

    def emit(pr, x):
        c, d, p, grow, brow = pr["c"], pr["d"], pr["p"], pr["grow"], pr["brow"]
        wk = jnp.dot((x * jnp.exp(grow)).astype(BF16), pr["kbd"], preferred_element_type=F32)
        u_ref[d, c, :, p * PAIR_W:(p + 1) * PAIR_W] = jnp.dot(x.astype(BF16), pr["vbd"],
                                                              preferred_element_type=F32).astype(BF16)
        l1_ref[d, c, 2 * p] = jnp.concatenate([wk[:, :HEAD_W], pr["q0"] * jnp.exp(pr["gc0"])], axis=0).astype(BF16)
        l1_ref[d, c, 2 * p + 1] = jnp.concatenate([wk[:, HEAD_W:], pr["q1"] * jnp.exp(pr["gc1"])], axis=0).astype(BF16)
        last = cc - 1 if d == 0 else 0
        gl0 = grow[:, last:last + 1]
        gl1 = grow[:, cc + last:cc + last + 1]
        tail = jnp.exp(jnp.where(lane_row < cc, gl0, gl1) - grow) * brow
        l2_ref[d, c, p] = jnp.concatenate([pr["top"], pr["kt"] * tail], axis=0).astype(BF16)
        dec_ref[d, c, :, p * PAIR_W:(p + 1) * PAIR_W] = jnp.concatenate(
            [jnp.broadcast_to(jnp.exp(gl0), (1, HEAD_W)), jnp.broadcast_to(jnp.exp(gl1), (1, HEAD_W))], axis=1)

    group = min(cpt, PREP_GROUP)
    for c0 in range(0, cpt, group):
        probs = []
        for c in range(c0, c0 + group):
            probs += setup(c)
        xs = _unit_tri_inverses([pr["ld"] for pr in probs], ii, jl)
        for pr, xi in zip(probs, xs):
            emit(pr, xi)


PREP_CHUNKS = 8


def _gdn_prep(pq, pab, conv_qkv, a_log, dt_bias):
    b, t, w3 = pq.shape
    nc = t // GDN_CHUNK
    cpt = min(nc, PREP_CHUNKS)
    tt = cpt * GDN_CHUNK
    nt = t // tt
    hb = tt // HALO
    def gate_rows(a):
        rows = jnp.repeat(a.reshape(2 * PAIRS, 2), GDN_CHUNK, axis=1)
        return jnp.tile(jnp.concatenate([jnp.zeros_like(rows), rows], axis=0), (cpt, 1))

    alog_t, dtb_t = gate_rows(a_log), gate_rows(dt_bias)
    const = lambda bi, j: (0, 0)
    tok = lambda bi, j: (bi, j, 0)
    return pl.pallas_call(
        functools.partial(_gdn_prep_kernel, cpt=cpt, nt=nt),
        grid=(b, nt),
        in_specs=[pl.BlockSpec((None, tt, w3), tok),
                  pl.BlockSpec((None, HALO, w3), lambda bi, j: (bi, jnp.maximum(j * hb - 1, 0), 0)),
                  pl.BlockSpec((None, HALO, w3), lambda bi, j: (bi, jnp.minimum((j + 1) * hb, t // HALO - 1), 0)),
                  pl.BlockSpec((None, tt, LANES), tok),
                  pl.BlockSpec((3, w3), const),
                  pl.BlockSpec((8 * cpt, LANES), const), pl.BlockSpec((8 * cpt, LANES), const)],
        out_specs=[pl.BlockSpec((None, 2, cpt, N_HEADS, 2 * GDN_CHUNK, HEAD_W), lambda bi, j: (bi, 0, j, 0, 0, 0)),
                   pl.BlockSpec((None, 2, cpt, PAIRS, 3 * GDN_CHUNK, HEAD_W), lambda bi, j: (bi, 0, j, 0, 0, 0)),
                   pl.BlockSpec((None, 2, cpt, GDN_CHUNK, GDN_W), lambda bi, j: (bi, 0, j, 0, 0)),
                   pl.BlockSpec((None, 2, cpt, 1, GDN_W), lambda bi, j: (bi, 0, j, 0, 0))],
        out_shape=[jax.ShapeDtypeStruct((b, 2, nc, N_HEADS, 2 * GDN_CHUNK, HEAD_W), BF16),
                   jax.ShapeDtypeStruct((b, 2, nc, PAIRS, 3 * GDN_CHUNK, HEAD_W), BF16),
                   jax.ShapeDtypeStruct((b, 2, nc, GDN_CHUNK, GDN_W), BF16),
                   jax.ShapeDtypeStruct((b, 2, nc, 1, GDN_W), F32)],
        scratch_shapes=[pltpu.VMEM((tt, GDN_W), F32)] * 3,
        compiler_params=_cparams("parallel", "parallel"),
        name="gdn_prep",
    )(pq, pq, pq, pab, conv_qkv, alog_t, dtb_t)


SCAN_BATCH = 4
SCAN_CHUNKS = 4


def _gdn_scan_kernel(l1f, l1b, l2f, l2b, uf, ub, decf, decb, s0_ref, of_ref, ob_ref, sfin_ref, s_ref,
                     *, cpg, nj, nb):
    j = pl.program_id(1)

    @pl.when(j == 0)
    def _():
        s_ref[...] = s0_ref[...]

    zero = jnp.zeros((GDN_CHUNK, HEAD_W), BF16)

    def step(i, carry):
        chains = []
        for n in range(nb):
            chains.append((n, 0, l1f, l2f, uf, decf, of_ref, i))
            chains.append((n, 1, l1b, l2b, ub, decb, ob_ref, cpg - 1 - i))
        first = []
        for n, d, l1, l2, u, dec, o_ref, c in chains:
            vts, qss = [], []
            for h in range(N_HEADS):
                sh = s_ref[n, d, :, h * HEAD_W:(h + 1) * HEAD_W].astype(BF16)
                r1 = jnp.dot(l1[n, c, h], sh, preferred_element_type=F32)
                vts.append((u[n, c, :, h * HEAD_W:(h + 1) * HEAD_W].astype(F32) - r1[:GDN_CHUNK]).astype(BF16))
                qss.append(r1[GDN_CHUNK:])
            first.append((vts, qss))
        for (n, d, l1, l2, u, dec, o_ref, c), (vts, qss) in zip(chains, first):
            intra = []
            for p in range(N_HEADS // 2):
                rhs = jnp.concatenate([jnp.concatenate([vts[2 * p], zero], axis=-1),
                                       jnp.concatenate([zero, vts[2 * p + 1]], axis=-1)], axis=0)
                r2 = jnp.dot(l2[n, c, p], rhs, preferred_element_type=F32)
                intra.append(r2[:GDN_CHUNK])
                lo, hi = 2 * p * HEAD_W, 2 * (p + 1) * HEAD_W
                s_ref[n, d, :, lo:hi] = s_ref[n, d, :, lo:hi] * dec[n, c][:, lo:hi] + r2[GDN_CHUNK:]
            o = jnp.concatenate(qss, axis=-1) + jnp.concatenate(intra, axis=-1)
            o_ref[n, pl.ds(pl.multiple_of(c * GDN_CHUNK, GDN_CHUNK), GDN_CHUNK), :] = o.astype(o_ref.dtype)
        return carry

    lax.fori_loop(0, cpg, step, 0)

    @pl.when(j == nj - 1)
    def _():
        sfin_ref[...] = s_ref[...]


def _gdn_scan(l1, l2, u, dec, s0):
    b, _, nc = l1.shape[:3]
    cpg = min(nc, SCAN_CHUNKS)
    nj = nc // cpg
    t = nc * GDN_CHUNK
    fwd5 = lambda bi, j: (bi, 0, j, 0, 0, 0)
    bwd5 = lambda bi, j: (bi, 1, nj - 1 - j, 0, 0, 0)
    fwd4 = lambda bi, j: (bi, 0, j, 0, 0)
    bwd4 = lambda bi, j: (bi, 1, nj - 1 - j, 0, 0)
    nb = SCAN_BATCH if b % SCAN_BATCH == 0 else 1
    b1 = (nb, None, cpg, N_HEADS, 2 * GDN_CHUNK, HEAD_W)
    b2 = (nb, None, cpg, N_HEADS // 2, 3 * GDN_CHUNK, HEAD_W)
    bu = (nb, None, cpg, GDN_CHUNK, GDN_W)
    bd = (nb, None, cpg, 1, GDN_W)
    st = pl.BlockSpec((nb, 2, HEAD_W, GDN_W), lambda bi, j: (bi, 0, 0, 0))
    return pl.pallas_call(
        functools.partial(_gdn_scan_kernel, cpg=cpg, nj=nj, nb=nb),
        grid=(b // nb, nj),
        in_specs=[pl.BlockSpec(b1, fwd5), pl.BlockSpec(b1, bwd5), pl.BlockSpec(b2, fwd5), pl.BlockSpec(b2, bwd5),
                  pl.BlockSpec(bu, fwd4), pl.BlockSpec(bu, bwd4), pl.BlockSpec(bd, fwd4), pl.BlockSpec(bd, bwd4), st],
        out_specs=[pl.BlockSpec((nb, cpg * GDN_CHUNK, GDN_W), lambda bi, j: (bi, j, 0)),
                   pl.BlockSpec((nb, cpg * GDN_CHUNK, GDN_W), lambda bi, j: (bi, nj - 1 - j, 0)), st],
        out_shape=[jax.ShapeDtypeStruct((b, t, GDN_W), BF16), jax.ShapeDtypeStruct((b, t, GDN_W), BF16),
                   jax.ShapeDtypeStruct((b, 2, HEAD_W, GDN_W), F32)],
        scratch_shapes=[pltpu.VMEM((nb, 2, HEAD_W, GDN_W), F32)],
        compiler_params=_cparams("parallel", "arbitrary"),
        name="gdn_scan",
    )(l1, l1, l2, l2, u, u, dec, dec, s0)


FFN_TILE = 256


def _post_kernel(x_ref, of_ref, ob_ref, z_ref, pa_ref, prev_ref, next_ref, pc_ref, mod_ref, cw_ref, lng_ref, lnb_ref,
                 ws_ref, bsm_ref, gon_ref, wo_ref, gqm_ref, gpf_ref, w1_ref, w2_ref, gqf_ref, o_ref,
                 *, nj, seq_len, hidden):
    grid_conv = seq_len is None
    j = pl.program_id(1)
    o = of_ref[...].astype(F32) + ob_ref[...].astype(F32)
    z = z_ref[...].astype(F32)
    ybs = []
    for h in range(N_HEADS):
        oh = o[:, h * HEAD_W:(h + 1) * HEAD_W]
        oh = oh * lax.rsqrt(jnp.mean(oh * oh, axis=-1, keepdims=True) + EPS) * gon_ref[...]
        ybs.append((oh * _silu(z[:, h * HEAD_W:(h + 1) * HEAD_W])).astype(BF16))
    r = jnp.dot(jnp.concatenate(ybs, axis=-1), wo_ref[A_W:A_W + GDN_W, :], preferred_element_type=F32)
    pa = pa_ref[...].astype(F32)
    if grid_conv:
        ya = _conv_grid(pa, prev_ref[...].astype(F32), next_ref[...].astype(F32), cw_ref[...],
                        jnp.where(j > 0, 1.0, 0.0), jnp.where(j < nj - 1, 1.0, 0.0))
    else:
        ya = _conv_seq(pa, cw_ref[...], seq_len)
    r = r + jnp.dot(ya.astype(BF16), wo_ref[:A_W, :], preferred_element_type=F32)
    yc = _sgu(pc_ref[...].astype(F32), lng_ref[...], lnb_ref[...], ws_ref[...], bsm_ref[...])
    r = r + jnp.dot(yc.astype(BF16), wo_ref[A_W + GDN_W:, :], preferred_element_type=F32)
    x = x_ref[...] + _rms(r, gqm_ref[...] * mod_ref[2:3, :])
    hb = (_rms(x, gpf_ref[...] * (1.0 + mod_ref[4:5, :])) + mod_ref[3:4, :]).astype(BF16)
    acc = None
    for c0 in range(0, hidden, FFN_TILE):
        g = jnp.dot(hb, w1_ref[:, c0:c0 + FFN_TILE], preferred_element_type=F32)
        u = jnp.dot(hb, w1_ref[:, hidden + c0:hidden + c0 + FFN_TILE], preferred_element_type=F32)
        part = jnp.dot((_silu(g) * u).astype(BF16), w2_ref[c0:c0 + FFN_TILE, :], preferred_element_type=F32)
        acc = part if acc is None else acc + part
    o_ref[...] = x + _rms(acc, gqf_ref[...] * mod_ref[5:6, :])


def _post(x, of, ob, z, pa, pc, mod, conv_a, ln_g, ln_b, ws, bsm, g_onorm, w_o, gqm, gpf, w1, w2, gqf, *, seq_len=None):
    b, t, d = x.shape
    hidden = w2.shape[0]
    tm = min(t, ROW_TILE)
    nj = t // tm
    grid_conv = seq_len is None
    assert grid_conv or tm % seq_len == 0
    rows = tm // GRID_W
    last = t // GRID_W - 1
    const = lambda bi, j: (0, 0)
    tok = lambda bi, j: (bi, j, 0)
    full = lambda a: pl.BlockSpec(a.shape, const)
    resident = lambda a: pl.BlockSpec(a.shape, const, pipeline_mode=pl.Buffered(1))
    return pl.pallas_call(
        functools.partial(_post_kernel, nj=nj, seq_len=seq_len, hidden=hidden),
        grid=(b, nj),
        in_specs=[pl.BlockSpec((None, tm, d), tok),
                  pl.BlockSpec((None, tm, GDN_W), tok), pl.BlockSpec((None, tm, GDN_W), tok),
                  pl.BlockSpec((None, tm, GDN_W), tok),
                  pl.BlockSpec((None, tm, 3 * A_W), tok),
                  pl.BlockSpec((None, GRID_W, 3 * A_W), lambda bi, j: (bi, jnp.maximum(j * rows - 1, 0), 0)),
                  pl.BlockSpec((None, GRID_W, 3 * A_W), lambda bi, j: (bi, jnp.minimum((j + 1) * rows, last), 0)),
                  pl.BlockSpec((None, tm, 2 * C_W), tok),
                  pl.BlockSpec((None, MOD_ROWS, d), lambda bi, j: (bi, 0, 0)),
                  full(conv_a), full(ln_g), full(ln_b), full(ws), full(bsm), full(g_onorm), resident(w_o),
                  full(gqm), full(gpf), resident(w1), resident(w2), full(gqf)],
        out_specs=pl.BlockSpec((None, tm, d), tok),
        out_shape=jax.ShapeDtypeStruct((b, t, d), F32),
        compiler_params=_cparams("parallel", "parallel"),
        name="post_grid" if grid_conv else "post_seq",
    )(x, of, ob, z, pa, pa, pa, pc, mod, conv_a, ln_g, ln_b, ws, bsm, g_onorm, w_o, gqm, gpf, w1, w2, gqf)


OFF_A, OFF_Q, OFF_AB, OFF_Z, OFF_C = 0, 3 * A_W, 3 * A_W + 3 * GDN_W, 3 * A_W + 3 * GDN_W + 16, 3 * A_W + 4 * GDN_W + 16
GATE_PERM = [4 * ((n % 8) // 2) + 2 * (n % 2) + n // 8 for n in range(16)]


def _mod_rows(rows, b, d):
    m = jnp.broadcast_to(rows.reshape(-1, 6, d), (b, 6, d))
    return jnp.pad(m, ((0, 0), (0, MOD_ROWS - 6), (0, 0)))


def kernel(x, c, ctx, c_ctx, w_mod, b_mod, g_pre_mix, g_post_mix, g_pre_ffn, g_post_ffn, w_in, conv_a, conv_qkv,
           a_log, dt_bias, g_onorm, ln_c_g, ln_c_b, w_s, b_s, w_o, w_ffn_in, w_ffn_out):
    b, t, d = x.shape
    depth = w_mod.shape[0]
    pad_rows = -(b + 1) % SUBLANES
    cc = jnp.concatenate([c, c_ctx[None, :], jnp.zeros((pad_rows, d), F32)], axis=0)
    mod = _modulation(cc, w_mod, b_mod)
    s_zero = jnp.zeros((b, 2, HEAD_W, GDN_W), F32)
    tc = ctx.shape[1]
    ctx = ctx.reshape(1, b * tc, d)
    per_seq = lambda a: a.reshape(b, tc, a.shape[-1])
    stacked = lambda a: a.reshape(1, b * tc, a.shape[-1])
    for l in range(depth):
        last = l == depth - 1
        modx = _mod_rows(mod[l, :b], b, d)
        modc = _mod_rows(mod[l, b:b + 1], 1, d)
        wl = w_in[l].astype(BF16)
        w_gate = wl[:, OFF_AB:OFF_Z][:, jnp.array(GATE_PERM)]
        ws_in = [wl[:, OFF_Q:OFF_AB], wl[:, OFF_Z:OFF_C], wl[:, OFF_A:OFF_Q], wl[:, OFF_C:],
                 jnp.pad(w_gate, ((0, 0), (0, LANES - 16)))]
        gpm, gqm = g_pre_mix[l][None, :], g_post_mix[l][None, :]
        gpf, gqf = g_pre_ffn[l][None, :], g_post_ffn[l][None, :]
        sgu_w = w_s[l].reshape(C_GROUPS * C_CHUNK, C_CHUNK).astype(BF16)
        sgu_b = jnp.repeat(b_s[l].T, C_GD, axis=1)
        lng, lnb = ln_c_g[l][None, :], ln_c_b[l][None, :]
        wo = w_o[l].astype(BF16)
        w1, w2 = w_ffn_in[l].astype(BF16), w_ffn_out[l].astype(BF16)
        gon = g_onorm[l][None, :]

        def deltanet(pq, pab, s0):
            return _gdn_scan(*_gdn_prep(pq, pab, conv_qkv[l], a_log[l], dt_bias[l]), s0)

        def finish(stream, m, of, ob, pz, pa, pc, seq_len):
            return _post(stream, of, ob, pz, pa, pc, m, conv_a[l], lng, lnb, sgu_w, sgu_b, gon, wo, gqm, gpf, w1, w2,
                         gqf, seq_len=seq_len)

        if last:
            pq, pab = _in_proj(ctx, modc, gpm, [ws_in[0], ws_in[-1]])
            _, _, s_ctx = deltanet(per_seq(pq), per_seq(pab), s_zero)
        else:
            pq, pz, pa, pc, pab = _in_proj(ctx, modc, gpm, ws_in)
            of, ob, s_ctx = deltanet(per_seq(pq), per_seq(pab), s_zero)
            ctx = finish(ctx, modc, stacked(of), stacked(ob), pz, pa, pc, tc)
        pq, pz, pa, pc, pab = _in_proj(x, modx, gpm, ws_in)
        of, ob, _ = deltanet(pq, pab, s_ctx)
        x = finish(x, modx, of, ob, pz, pa, pc, None)
    return x
```
